```python
import jax, jax.numpy as jnp
from jax import lax
import numpy as np

D_MODEL = 1024
BATCH = 8
SEQ = 4096
DEPTH = 4

GRID_W = 64
CTX_LEN = 256
HEAD_DIM = 64
MIX_WIDTH = D_MODEL
CONV_WIDTH_A = D_MODEL // 4
RWKV_WIDTH = 3 * D_MODEL // 8
LRU_WIDTH = MIX_WIDTH - CONV_WIDTH_A - RWKV_WIDTH
RWKV_HEADS = RWKV_WIDTH // HEAD_DIM
LRU_BLOCKS = LRU_WIDTH // HEAD_DIM
W_LORA = 64
A_LORA = 64
G_LORA = 128
RWKV_COLS = 3 * RWKV_WIDTH + W_LORA + A_LORA + G_LORA
IN_COLS = 3 * CONV_WIDTH_A + RWKV_COLS + 2 * LRU_WIDTH
D_FF = 2816
LRU_C = 8.0
LRU_CONV = 4
NORM_EPS = 1e-6
GN_EPS = 64e-5

kernel_name = 'hybrid_conv_rwkv7_rglru_macaron_dit'


def rmsnorm(x, g):
    xf = x.astype(jnp.float32)
    y = xf * lax.rsqrt(jnp.mean(xf * xf, axis=-1, keepdims=True) + NORM_EPS)
    return (y * g.astype(jnp.float32)).astype(x.dtype)


def adaln(cvec, w, b):
    return jnp.split(jax.nn.silu(cvec) @ w + b, 9, axis=-1)


def modulate(h, shift, scale):
    return h * (1.0 + scale) + shift


def macaron_ffn(s, m, g, w_gu, w_down):
    h = modulate(rmsnorm(s, g), m[0], m[1])
    gate, up = jnp.split(h @ w_gu, 2, axis=-1)
    return s + 0.5 * m[2] * ((jax.nn.silu(gate) * up) @ w_down)


def shift_seq(u, o):
    if o == 0:
        return u
    t_len = u.shape[-2]
    pad = [(0, 0)] * u.ndim
    if o > 0:
        pad[-2] = (0, o)
        return jnp.pad(u, pad)[..., o:o + t_len, :]
    pad[-2] = (-o, 0)
    return jnp.pad(u, pad)[..., :t_len, :]


def dwconv(u, w, offsets):
    out = w[0] * shift_seq(u, offsets[0])
    for j in range(1, len(offsets)):
        out = out + w[j] * shift_seq(u, offsets[j])
    return out


def to_col_major(t, rows):
    b, s, ch = t.shape
    return t.reshape(b, rows, GRID_W, ch).transpose(0, 2, 1, 3).reshape(b, s, ch)


def from_col_major(t, rows):
    b, s, ch = t.shape
    return t.reshape(b, GRID_W, rows, ch).transpose(0, 2, 1, 3).reshape(b, s, ch)


def short_conv_mixer(p, conv_w, line_len):
    bg, cg, xin = jnp.split(p, 3, axis=-1)
    u = cg * xin
    b, t_len, ch = u.shape
    u = u.reshape(b, t_len // line_len, line_len, ch)
    y = dwconv(u, conv_w, (-1, 0, 1)).reshape(b, t_len, ch)
    return bg * y


def rwkv_scan(s0, r, w, k, v, kk, a, reverse):
    def step(s, inp):
        r_t, w_t, k_t, v_t, kk_t, a_t = inp
        sa = jnp.einsum('bhvk,bhk->bhv', s, -kk_t)
        s = (s * w_t[:, :, None, :] + sa[..., None] * (kk_t * a_t)[:, :, None, :]
             + v_t[..., None] * k_t[:, :, None, :])
        return s, jnp.einsum('bhvk,bhk->bhv', s, r_t)
    xs = tuple(jnp.moveaxis(t, 1, 0) for t in (r, w, k, v, kk, a))
    s_fin, y = lax.scan(step, s0, xs, reverse=reverse)
    return s_fin, jnp.moveaxis(y, 0, 1)


def rwkv_mixer(p, mu, w0, w2, a0, a2, g2, k_k, k_a, r_k, lnx_g, lnx_b, init):
    dt = p.dtype
    bsz, t_len, _ = p.shape
    R = RWKV_WIDTH
    p = p + mu[0] * (shift_seq(p, -1) - p) + mu[1] * (shift_seq(p, 1) - p)
    r, k, v, dw, da, dg = jnp.split(p, [R, 2 * R, 3 * R, 3 * R + W_LORA, 3 * R + W_LORA + A_LORA], axis=-1)

    def heads(t):
        return t.astype(jnp.float32).reshape(bsz, t_len, RWKV_HEADS, HEAD_DIM)

    g = jax.nn.sigmoid(dg) @ g2
    kk = heads(k * k_k)
    kk = kk * lax.rsqrt(jnp.sum(kk * kk, axis=-1, keepdims=True) + 1e-12)
    rh, vh = heads(r), heads(v)
    rk = r_k.astype(jnp.float32).reshape(RWKV_HEADS, HEAD_DIM)
    tw = jnp.tanh(dw)
    ys, bonuses, finals = [], [], []
    for d in range(2):
        w_log = -jax.nn.softplus(-(w0[d] + tw @ w2[d]).astype(jnp.float32)) - 0.5
        decay = jnp.exp(-jnp.exp(w_log))
        a = jax.nn.sigmoid(a0[d] + da @ a2[d])
        kd = heads(k * (1.0 + (a - 1.0) * k_a))
        s_fin, y = rwkv_scan(init[d], rh, heads(decay), kd, vh, kk, heads(a), reverse=(d == 1))
        ys.append(y)
        bonuses.append(jnp.sum(rh * kd * rk, axis=-1, keepdims=True) * vh)
        finals.append(s_fin)
    y_sum = ys[0] + ys[1]
    mean = jnp.mean(y_sum, axis=-1, keepdims=True)
    var = jnp.mean(jnp.square(y_sum - mean), axis=-1, keepdims=True)
    gn = ((y_sum - mean) * lax.rsqrt(var + GN_EPS)).reshape(bsz, t_len, R) * lnx_g + lnx_b
    out = (gn + (bonuses[0] + bonuses[1]).reshape(bsz, t_len, R)) * g
    return out.astype(dt), (finals[0], finals[1])


def block_diag(u, w):
    b, t_len, _ = u.shape
    ub = u.reshape(b, t_len, LRU_BLOCKS, HEAD_DIM)
    return jnp.einsum('btni,nij->btnj', ub, w).reshape(b, t_len, LRU_WIDTH)


def _lru_combine(e1, e2):
    a1, b1 = e1
    a2, b2 = e2
    return a1 * a2, a2 * b1 + b2


def rglru_mixer(p, conv_w, conv_b, w_r, b_r, w_i, b_i, lam, init):
    dt = p.dtype
    xr, gr = jnp.split(p, 2, axis=-1)
    hs, finals = [], []
    for d, offs in enumerate(((-3, -2, -1, 0), (0, 1, 2, 3))):
        u = dwconv(xr, conv_w[d], offs) + conv_b[d]
        rg = jax.nn.sigmoid(block_diag(u, w_r[d]) + b_r[d])
        ig = jax.nn.sigmoid(block_diag(u, w_i[d]) + b_i[d])
        log_a = -LRU_C * (jax.nn.softplus(-lam[d]) * rg).astype(jnp.float32)
        a = jnp.exp(log_a)
        bterm = jnp.sqrt(-jnp.expm1(2.0 * log_a)) * (ig * u).astype(jnp.float32)
        a_cum, h = lax.associative_scan(_lru_combine, (a, bterm), reverse=(d == 1), axis=1)
        h = h + a_cum * init[d][:, None, :]
        hs.append(h)
        finals.append(h[:, -1] if d == 0 else h[:, 0])
    out = jax.nn.gelu(gr) * (hs[0] + hs[1]).astype(dt)
    return out, (finals[0], finals[1])


def token_mixers(p, line_len, conv_a, rwkv_params, lru_params, rwkv_init, lru_init):
    pa, pb, pc = jnp.split(p, [3 * CONV_WIDTH_A, 3 * CONV_WIDTH_A + RWKV_COLS], axis=-1)
    ya = short_conv_mixer(pa, conv_a, line_len)
    yb, rwkv_fin = rwkv_mixer(pb, *rwkv_params, rwkv_init)
    yc, lru_fin = rglru_mixer(pc, *lru_params, lru_init)
    return jnp.concatenate([ya, yb, yc], axis=-1), rwkv_fin, lru_fin


def setup_inputs(seed: int = 0) -> dict:
    key = jax.random.key(seed)
    ks = iter(jax.random.split(key, 48))
    L, D, R, LW = DEPTH, D_MODEL, RWKV_WIDTH, LRU_WIDTH

    def nrm(shape, scale):
        return jax.random.normal(next(ks), shape, jnp.float32) * scale

    def uni(shape, lo, hi):
        return jax.random.uniform(next(ks), shape, jnp.float32, lo, hi)

    lam_a = uni((L, 2, LW), 0.9, 0.999)
    return {
        'x': nrm((BATCH, SEQ, D), 1.0),
        'c': nrm((BATCH, D), 1.0),
        'ctx': nrm((BATCH, CTX_LEN, D), 1.0),
        'c_ctx': nrm((D,), 1.0),
        'w_mod': nrm((L, D, 9 * D), 0.5 * D ** -0.5),
        'b_mod': nrm((L, 9 * D), 0.01),
        'g_ffn1': 1.0 + nrm((L, D), 0.05),
        'w_gu1': nrm((L, D, 2 * D_FF), D ** -0.5),
        'w_down1': nrm((L, D_FF, D), D_FF ** -0.5),
        'g_mix': 1.0 + nrm((L, D), 0.05),
        'w_in': nrm((L, D, IN_COLS), D ** -0.5),
        'conv_a': nrm((L, 3, CONV_WIDTH_A), 3 ** -0.5),
        'rwkv_mu': uni((L, 2, RWKV_COLS), 0.0, 0.5),
        'rwkv_w0': uni((L, 2, R), -4.0, 1.0),
        'rwkv_w2': nrm((L, 2, W_LORA, R), 0.5 * W_LORA ** -0.5),
        'rwkv_a0': nrm((L, 2, R), 0.5),
        'rwkv_a2': nrm((L, 2, A_LORA, R), 0.5 * A_LORA ** -0.5),
        'rwkv_g2': nrm((L, G_LORA, R), G_LORA ** -0.5),
        'rwkv_kk': 0.85 + nrm((L, R), 0.05),
        'rwkv_ka': 1.0 + nrm((L, R), 0.05),
        'rwkv_rk': nrm((L, R), 0.1),
        'rwkv_lnx_g': 1.0 + nrm((L, R), 0.05),
        'rwkv_lnx_b': nrm((L, R), 0.01),
        'lru_conv_w': nrm((L, 2, LRU_CONV, LW), LRU_CONV ** -0.5),
        'lru_conv_b': nrm((L, 2, LW), 0.01),
        'lru_w_r': nrm((L, 2, LRU_BLOCKS, HEAD_DIM, HEAD_DIM), HEAD_DIM ** -0.5),
        'lru_b_r': nrm((L, 2, LW), 0.01),
        'lru_w_i': nrm((L, 2, LRU_BLOCKS, HEAD_DIM, HEAD_DIM), HEAD_DIM ** -0.5),
        'lru_b_i': nrm((L, 2, LW), 0.01),
        'lru_lam': jnp.log(lam_a) - jnp.log1p(-lam_a),
        'w_out': nrm((L, MIX_WIDTH, D), MIX_WIDTH ** -0.5),
        'g_ffn2': 1.0 + nrm((L, D), 0.05),
        'w_gu2': nrm((L, D, 2 * D_FF), D ** -0.5),
        'w_down2': nrm((L, D_FF, D), D_FF ** -0.5),
        'g_final': 1.0 + nrm((D,), 0.05),
    }


def reference(x, c, ctx, c_ctx, w_mod, b_mod, g_ffn1, w_gu1, w_down1, g_mix, w_in, conv_a,
              rwkv_mu, rwkv_w0, rwkv_w2, rwkv_a0, rwkv_a2, rwkv_g2, rwkv_kk, rwkv_ka, rwkv_rk,
              rwkv_lnx_g, rwkv_lnx_b, lru_conv_w, lru_conv_b, lru_w_r, lru_b_r, lru_w_i, lru_b_i,
              lru_lam, w_out, g_ffn2, w_gu2, w_down2, g_final):
    bsz = x.shape[0]
    rows = x.shape[1] // GRID_W
    s_ctx = ctx
    for l in range(DEPTH):
        last = l == DEPTH - 1
        m_lat = [t[:, None, :] for t in adaln(c, w_mod[l], b_mod[l])]
        m_ctx = adaln(c_ctx, w_mod[l], b_mod[l])
        x = macaron_ffn(x, m_lat[0:3], g_ffn1[l], w_gu1[l], w_down1[l])
        s_ctx = macaron_ffn(s_ctx, m_ctx[0:3], g_ffn1[l], w_gu1[l], w_down1[l])
        p_ctx = modulate(rmsnorm(s_ctx, g_mix[l]), m_ctx[3], m_ctx[4]) @ w_in[l]
        p_lat = modulate(rmsnorm(x, g_mix[l]), m_lat[3], m_lat[4]) @ w_in[l]
        col_major = l % 2 == 1
        if col_major:
            p_lat = to_col_major(p_lat, rows)
        line_len = rows if col_major else GRID_W
        rwkv_p = (rwkv_mu[l], rwkv_w0[l], rwkv_w2[l], rwkv_a0[l], rwkv_a2[l], rwkv_g2[l],
                  rwkv_kk[l], rwkv_ka[l], rwkv_rk[l], rwkv_lnx_g[l], rwkv_lnx_b[l])
        lru_p = (lru_conv_w[l], lru_conv_b[l], lru_w_r[l], lru_b_r[l], lru_w_i[l], lru_b_i[l], lru_lam[l])
        rwkv_zero = jnp.zeros((bsz, RWKV_HEADS, HEAD_DIM, HEAD_DIM), jnp.float32)
        lru_zero = jnp.zeros((bsz, LRU_WIDTH), jnp.float32)
        y_ctx, rwkv_fin, lru_fin = token_mixers(p_ctx, s_ctx.shape[1], conv_a[l], rwkv_p, lru_p,
                                                (rwkv_zero, rwkv_zero), (lru_zero, lru_zero))
        y_lat, _, _ = token_mixers(p_lat, line_len, conv_a[l], rwkv_p, lru_p, rwkv_fin, lru_fin)
        if col_major:
            y_lat = from_col_major(y_lat, rows)
        x = x + m_lat[5] * (y_lat @ w_out[l])
        x = macaron_ffn(x, m_lat[6:9], g_ffn2[l], w_gu2[l], w_down2[l])
        if not last:
            s_ctx = s_ctx + m_ctx[5] * (y_ctx @ w_out[l])
            s_ctx = macaron_ffn(s_ctx, m_ctx[6:9], g_ffn2[l], w_gu2[l], w_down2[l])
    return rmsnorm(x, g_final)
```

```python
import functools

import jax
import jax.numpy as jnp
from jax import lax
from jax.experimental import pallas as pl
from jax.experimental.pallas import tpu as pltpu

F32 = jnp.float32
BF16 = jnp.bfloat16

HEAD = 64
PAIR = 2 * HEAD
CHUNK = 64
HALO = 8
GRID_W = 64
NORM_EPS = 1e-6
GN_EPS = 64e-5
LRU_C = 8.0
VMEM_LIMIT = 56 * 1024 * 1024


def _dot(a, b):
    return jnp.dot(a.astype(BF16), b.astype(BF16), preferred_element_type=F32)


def _split2(a):
    hi = a.astype(BF16)
    lo = (a - hi.astype(F32)).astype(BF16)
    return hi, lo


_NN = (((1,), (0,)), ((), ()))
_NT = (((1,), (1,)), ((), ()))
_TN = (((0,), (0,)), ((), ()))


def _dot3(a, b, dims=_NN):
    ah, al = _split2(a)
    bh, bl = _split2(b)
    d = functools.partial(lax.dot_general, dimension_numbers=dims, preferred_element_type=F32)
    return d(ah, bh) + (d(ah, bl) + d(al, bh))


def _dot_exact_rhs(a, ones_rhs):
    a1 = a.astype(BF16)
    r1 = a - a1.astype(F32)
    a2 = r1.astype(BF16)
    a3 = (r1 - a2.astype(F32)).astype(BF16)
    d = functools.partial(jnp.dot, preferred_element_type=F32)
    return d(a1, ones_rhs) + (d(a2, ones_rhs) + d(a3, ones_rhs))


def _dot_exact_lhs(ones_lhs, b):
    b1 = b.astype(BF16)
    r1 = b - b1.astype(F32)
    b2 = r1.astype(BF16)
    b3 = (r1 - b2.astype(F32)).astype(BF16)
    d = functools.partial(jnp.dot, preferred_element_type=F32)
    return d(ones_lhs, b1) + (d(ones_lhs, b2) + d(ones_lhs, b3))


def _softplus(x):
    return jnp.maximum(x, 0.0) + jnp.log1p(jnp.exp(-jnp.abs(x)))


def _rms_modulate(s, g, shift, scale):
    y = s * lax.rsqrt(jnp.mean(s * s, axis=-1, keepdims=True) + NORM_EPS)
    return (y * g) * (1.0 + scale) + shift


def _shift_dn(x, prev, k):
    ch = x.shape[1]
    rolled = pltpu.roll(x, k, axis=0)
    pr = pltpu.roll(prev, k, axis=0)
    r8 = lax.broadcasted_iota(jnp.int32, (HALO, ch), 0)
    head = jnp.where(r8 < k, pr, rolled[:HALO])
    return jnp.concatenate([head, rolled[HALO:]], axis=0)


def _shift_up(x, nxt, k):
    n, ch = x.shape
    rolled = pltpu.roll(x, n - k, axis=0)
    nr = pltpu.roll(nxt, HALO - k, axis=0)
    r8 = lax.broadcasted_iota(jnp.int32, (HALO, ch), 0)
    tail = jnp.where(r8 >= HALO - k, nr, rolled[n - HALO:])
    return jnp.concatenate([rolled[:n - HALO], tail], axis=0)


def _scan_shift(x, k, fill, reverse):
    n, ch = x.shape
    rows = lax.broadcasted_iota(jnp.int32, (n, ch), 0)
    if not reverse:
        return jnp.where(rows >= k, pltpu.roll(x, k, axis=0), fill)
    return jnp.where(rows < n - k, pltpu.roll(x, n - k, axis=0), fill)


def _blockdiag(xp):
    n = xp.shape[0]
    lane = lax.broadcasted_iota(jnp.int32, (n, PAIR), 1)
    top = jnp.where(lane < HEAD, xp, 0.0)
    bot = jnp.where(lane >= HEAD, xp, 0.0)
    return jnp.concatenate([top, bot], axis=0)


def _mod_kernel(c_ref, w_ref, b_ref, o_ref):
    cc = c_ref[...]
    act = cc * jax.nn.sigmoid(cc)
    o_ref[0] = _dot(act, w_ref[0]) + b_ref[0]


def _modulation(cc, w_mod, b_mod):
    depth, d, n = w_mod.shape
    rows = cc.shape[0]
    tn = n // 8
    return pl.pallas_call(
        _mod_kernel,
        grid=(depth, n // tn),
        in_specs=[
            pl.BlockSpec((rows, d), lambda l, j: (0, 0)),
            pl.BlockSpec((1, d, tn), lambda l, j: (l, 0, j)),
            pl.BlockSpec((1, 1, tn), lambda l, j: (l, 0, j)),
        ],
        out_specs=pl.BlockSpec((1, rows, tn), lambda l, j: (l, 0, j)),
        out_shape=jax.ShapeDtypeStruct((depth, rows, n), F32),
        compiler_params=pltpu.CompilerParams(
            dimension_semantics=("parallel", "parallel"), vmem_limit_bytes=VMEM_LIMIT),
        name="modulation",
    )(cc, w_mod, b_mod.reshape(depth, 1, n))


def _ffn_kernel(*refs, mod_base, has_pre, final_norm, d_ff, tf):
    it = iter(refs)
    s_ref = next(it)
    if has_pre:
        y_ref = next(it)
        wout_ref = next(it)
    mod_ref = next(it)
    g_ref = next(it)
    wgu_ref = next(it)
    wdown_ref = next(it)
    if final_norm:
        gfin_ref = next(it)
    o_ref = next(it)
    acc_ref = next(it)

    s = s_ref[0]
    if has_pre:
        gate_mix = mod_ref[0, mod_base - 1:mod_base, :]
        s = s + gate_mix * _dot(y_ref[0], wout_ref[...])
    shift = mod_ref[0, mod_base:mod_base + 1, :]
    scale = mod_ref[0, mod_base + 1:mod_base + 2, :]
    gate = mod_ref[0, mod_base + 2:mod_base + 3, :]
    hb = _rms_modulate(s, g_ref[...], shift, scale).astype(BF16)
    for j in range(d_ff // tf):
        gt = jnp.dot(hb, wgu_ref[:, j * tf:(j + 1) * tf], preferred_element_type=F32)
        up = jnp.dot(hb, wgu_ref[:, d_ff + j * tf:d_ff + (j + 1) * tf], preferred_element_type=F32)
        act = ((gt * jax.nn.sigmoid(gt)) * up).astype(BF16)
        part = jnp.dot(act, wdown_ref[j * tf:(j + 1) * tf, :], preferred_element_type=F32)
        if j == 0:
            acc_ref[...] = part
        else:
            acc_ref[...] += part
    out = s + (0.5 * gate) * acc_ref[...]
    if final_norm:
        out = out * lax.rsqrt(jnp.mean(out * out, axis=-1, keepdims=True) + NORM_EPS) * gfin_ref[...]
    o_ref[0] = out


def _ffn(s, mod, mod_base, g, wgu, wdown, y=None, wout=None, gfin=None, tm=512):
    bv, tv, d = s.shape
    d_ff = wdown.shape[0]
    tf = 256
    has_pre = y is not None
    final_norm = gfin is not None
    const = lambda b, i: (0, 0)
    tile = lambda b, i: (b, i, 0)
    in_specs = [pl.BlockSpec((1, tm, d), tile)]
    args = [s]
    if has_pre:
        in_specs += [pl.BlockSpec((1, tm, y.shape[2]), tile), pl.BlockSpec(wout.shape, const)]
        args += [y, wout]
    in_specs += [
        pl.BlockSpec((1, mod.shape[1], d), lambda b, i: (b, 0, 0)),
        pl.BlockSpec((1, d), const),
        pl.BlockSpec(wgu.shape, const),
        pl.BlockSpec(wdown.shape, const),
    ]
    args += [mod, g.reshape(1, d), wgu, wdown]
    if final_norm:
        in_specs.append(pl.BlockSpec((1, d), const))
        args.append(gfin.reshape(1, d))
    body = functools.partial(_ffn_kernel, mod_base=mod_base, has_pre=has_pre,
                             final_norm=final_norm, d_ff=d_ff, tf=tf)
    return pl.pallas_call(
        body,
        grid=(bv, tv // tm),
        in_specs=in_specs,
        out_specs=pl.BlockSpec((1, tm, d), tile),
        out_shape=jax.ShapeDtypeStruct(s.shape, F32),
        scratch_shapes=[pltpu.VMEM((tm, d), F32)],
        compiler_params=pltpu.CompilerParams(
            dimension_semantics=("parallel", "parallel"), vmem_limit_bytes=VMEM_LIMIT),
        name="ffn",
    )(*args)


def _inproj_kernel(s_ref, mod_ref, g_ref, w_ref, o_ref, *, mod_base):
    shift = mod_ref[0, mod_base:mod_base + 1, :]
    scale = mod_ref[0, mod_base + 1:mod_base + 2, :]
    hb = _rms_modulate(s_ref[0], g_ref[...], shift, scale).astype(BF16)
    o_ref[0] = jnp.dot(hb, w_ref[...], preferred_element_type=F32)


def _inproj(s, mod, mod_base, g, w, tm=512):
    bv, tv, d = s.shape
    n = w.shape[1]
    const = lambda b, i: (0, 0)
    return pl.pallas_call(
        functools.partial(_inproj_kernel, mod_base=mod_base),
        grid=(bv, tv // tm),
        in_specs=[
            pl.BlockSpec((1, tm, d), lambda b, i: (b, i, 0)),
            pl.BlockSpec((1, mod.shape[1], d), lambda b, i: (b, 0, 0)),
            pl.BlockSpec((1, d), const),
            pl.BlockSpec(w.shape, const),
        ],
        out_specs=pl.BlockSpec((1, tm, n), lambda b, i: (b, i, 0)),
        out_shape=jax.ShapeDtypeStruct((bv, tv, n), F32),
        compiler_params=pltpu.CompilerParams(
            dimension_semantics=("parallel", "parallel"), vmem_limit_bytes=VMEM_LIMIT),
        name="inproj",
    )(s, mod, g.reshape(1, d), w)


CONV_W = 256
RW = 384
OFF_B = 3 * CONV_W
RWKV_COLS = 3 * RW + 64 + 64 + 128
OFF_C = OFF_B + RWKV_COLS
P_COLS = OFF_C + 2 * RW
F_COLS = 3 * RW
V_W0, V_A0, V_KK, V_KA, V_RK, V_LNG, V_LNB, V_CW, V_CB, V_BR, V_BI, V_LAM = 0, 1, 2, 3, 4, 5, 6, 7, 11, 12, 13, 14
VEC_ROWS = 16


def _mixer_kernel(*refs, d, n_chunks, line_is_chunk):
    it = iter(refs)
    p_ref, pp_ref, pn_ref = next(it), next(it), next(it)
    if d == 1:
        f_in_ref = next(it)
    s0_ref, h0_ref = next(it), next(it)
    vec_ref, mu_ref, conva_ref = next(it), next(it), next(it)
    w2_ref, a2_ref, g2_ref, wr_ref, wi_ref = next(it), next(it), next(it), next(it), next(it)
    out_ref, s_ref, h_ref = next(it), next(it), next(it)

    reverse = d == 1
    i = pl.program_id(1)
    ci = (n_chunks - 1 - i) if reverse else i
    keep_prev = jnp.where(ci == 0, 0.0, 1.0)
    keep_next = jnp.where(ci == n_chunks - 1, 0.0, 1.0)

    @pl.when(i == 0)
    def _():
        s_ref[...] = s0_ref[...]
        h_ref[...] = h0_ref[...]

    def vec(r, sl=slice(None)):
        return vec_ref[r:r + 1, sl]

    cur = p_ref[0, :, OFF_B:OFF_C]
    prev = pp_ref[0, :, OFF_B:OFF_C] * keep_prev
    nxt = pn_ref[0, :, OFF_B:OFF_C] * keep_next
    pb = (cur + mu_ref[0:1, :] * (_shift_dn(cur, prev, 1) - cur)
          + mu_ref[1:2, :] * (_shift_up(cur, nxt, 1) - cur))
    r = pb[:, 0:RW]
    k = pb[:, RW:2 * RW]
    v = pb[:, 2 * RW:3 * RW]
    dw = pb[:, 3 * RW:3 * RW + 64]
    da = pb[:, 3 * RW + 64:3 * RW + 128]
    dg = pb[:, 3 * RW + 128:3 * RW + 256]

    lane_p = lax.broadcasted_iota(jnp.int32, (PAIR, PAIR), 1)
    row_p = lax.broadcasted_iota(jnp.int32, (PAIR, PAIR), 0)
    same_head = (lane_p >= HEAD) == (row_p >= HEAD)
    head_ones = jnp.where(same_head, 1.0, 0.0).astype(BF16)

    def head_sum(x):
        return jnp.concatenate(
            [_dot_exact_rhs(x[:, PAIR * hp:PAIR * (hp + 1)], head_ones) for hp in range(RW // PAIR)],
            axis=1)

    kkr = k * vec(V_KK)
    kk = kkr * lax.rsqrt(head_sum(kkr * kkr) + 1e-12)
    w_log = -_softplus(-(vec(V_W0) + _dot(jnp.tanh(dw), w2_ref[...]))) - 0.5
    lw = -jnp.exp(w_log)
    a = jax.nn.sigmoid(vec(V_A0) + _dot(da, a2_ref[...]))
    kd = k * (1.0 + (a - 1.0) * vec(V_KA))
    beta = kk * a
    bonus = head_sum(r * kd * vec(V_RK)) * v

    row_c = lax.broadcasted_iota(jnp.int32, (CHUNK, CHUNK), 0)
    col_c = lax.broadcasted_iota(jnp.int32, (CHUNK, CHUNK), 1)
    incl_c = (col_c >= row_c) if reverse else (col_c <= row_c)
    c_inc = _dot_exact_lhs(jnp.where(incl_c, 1.0, 0.0).astype(BF16), lw)
    c_exc = c_inc - lw
    c_tot = c_inc[0:1, :] if reverse else c_inc[CHUNK - 1:CHUNK, :]
    c_mid = c_inc[CHUNK // 2:CHUNK // 2 + 1, :]
    e_exc = jnp.exp(c_exc)
    e_inc = jnp.exp(c_inc)
    e_rem = jnp.exp(c_tot - c_inc)
    e_neg_mid = jnp.exp(-c_mid)
    e_mid_tot = jnp.exp(c_mid - c_tot)
    gam = jnp.exp(c_tot)
    at_abs = -(kk * e_exc)
    rt_abs = r * e_inc
    bg = beta * e_rem
    kg = kd * e_rem
    at_off = at_abs * e_neg_mid
    rt_off = rt_abs * e_neg_mid
    bh_off = bg * e_mid_tot
    kh_off = kg * e_mid_tot

    row_h = lax.broadcasted_iota(jnp.int32, (CHUNK, PAIR), 0)
    col_h = lax.broadcasted_iota(jnp.int32, (CHUNK, PAIR), 1) & (HEAD - 1)
    strict = (col_h > row_h) if reverse else (col_h < row_h)
    incl = (col_h >= row_h) if reverse else (col_h <= row_h)
    eye_h = jnp.where(col_h == row_h, 1.0, 0.0)
    eye_p = lane_p == row_p

    if reverse:
        g = _dot(jax.nn.sigmoid(dg), g2_ref[...])
        f_in = f_in_ref[0]

    for hp in range(RW // PAIR):
        sl = slice(PAIR * hp, PAIR * (hp + 1))
        s0 = s_ref[0, hp]
        vp = v[:, sl]
        lhs_off = jnp.concatenate([at_off[:, sl], rt_off[:, sl]], axis=0)
        rhs_off = jnp.concatenate([_blockdiag(bh_off[:, sl]), _blockdiag(kh_off[:, sl])], axis=0)
        gm = _dot3(lhs_off, rhs_off, _NT)
        a_m = jnp.where(strict, gm[0:CHUNK, 0:PAIR], 0.0)
        b_m = jnp.where(strict, gm[0:CHUNK, PAIR:2 * PAIR], 0.0)
        p_m = jnp.where(incl, gm[CHUNK:2 * CHUNK, 0:PAIR], 0.0)
        q_m = jnp.where(incl, gm[CHUNK:2 * CHUNK, PAIR:2 * PAIR], 0.0)
        z = _dot3(jnp.concatenate([at_abs[:, sl], rt_abs[:, sl]], axis=0), s0)
        w_m = z[0:CHUNK] + _dot3(b_m, _blockdiag(vp))
        t_m = eye_h + a_m
        pw = a_m
        for _ in range(5):
            pw = _dot3(pw, _blockdiag(pw))
            t_m = t_m + _dot3(t_m, _blockdiag(pw))
        u = _dot3(t_m, _blockdiag(w_m))
        y = z[CHUNK:2 * CHUNK] + _dot3(
            jnp.concatenate([p_m, q_m], axis=1),
            jnp.concatenate([_blockdiag(u), _blockdiag(vp)], axis=0))
        dgam = jnp.where(eye_p, jnp.broadcast_to(gam[:, sl], (PAIR, PAIR)), 0.0)
        s_new = _dot3(jnp.concatenate([bg[:, sl], kg[:, sl], dgam], axis=0),
                      jnp.concatenate([u, vp, s0], axis=0), _TN)
        s_ref[0, hp] = jnp.where(same_head, s_new, 0.0)

        if not reverse:
            out_ref[0, :, sl] = y
        else:
            ysum = y + f_in[:, sl]
            mean = _dot_exact_rhs(ysum, head_ones) * (1.0 / HEAD)
            cen = ysum - mean
            var = _dot_exact_rhs(cen * cen, head_ones) * (1.0 / HEAD)
            gn = cen * lax.rsqrt(var + GN_EPS) * vec(V_LNG, sl) + vec(V_LNB, sl)
            yb = (gn + (f_in[:, RW + PAIR * hp:RW + PAIR * (hp + 1)] + bonus[:, sl])) * g[:, sl]
            out_ref[0, :, CONV_W + PAIR * hp:CONV_W + PAIR * (hp + 1)] = yb

    if not reverse:
        out_ref[0, :, RW:2 * RW] = bonus

    xr = p_ref[0, :, OFF_C:OFF_C + RW]
    if not reverse:
        xh = pp_ref[0, :, OFF_C:OFF_C + RW] * keep_prev
        u_l = (vec(V_CW + 3) * xr + vec(V_CW + 2) * _shift_dn(xr, xh, 1)
               + vec(V_CW + 1) * _shift_dn(xr, xh, 2) + vec(V_CW) * _shift_dn(xr, xh, 3))
    else:
        xh = pn_ref[0, :, OFF_C:OFF_C + RW] * keep_next
        u_l = (vec(V_CW) * xr + vec(V_CW + 1) * _shift_up(xr, xh, 1)
               + vec(V_CW + 2) * _shift_up(xr, xh, 2) + vec(V_CW + 3) * _shift_up(xr, xh, 3))
    u_l = u_l + vec(V_CB)
    rg = jax.nn.sigmoid(_dot(u_l, wr_ref[...]) + vec(V_BR))
    ig = jax.nn.sigmoid(_dot(u_l, wi_ref[...]) + vec(V_BI))
    log_a = -LRU_C * (_softplus(-vec(V_LAM)) * rg)
    a_l = jnp.exp(log_a)
    th = jnp.tanh(log_a)
    b_l = jnp.sqrt(-2.0 * th / (1.0 - th)) * (ig * u_l)
    step = 1
    while step < CHUNK:
        b_l = b_l + a_l * _scan_shift(b_l, step, 0.0, reverse)
        a_l = a_l * _scan_shift(a_l, step, 1.0, reverse)
        step *= 2
    h = b_l + a_l * h_ref[0]
    h_ref[0] = h[0:1, :] if reverse else h[CHUNK - 1:CHUNK, :]

    if not reverse:
        out_ref[0, :, 2 * RW:3 * RW] = h
    else:
        gr = p_ref[0, :, OFF_C + RW:OFF_C + 2 * RW]
        out_ref[0, :, CONV_W + RW:CONV_W + 2 * RW] = jax.nn.gelu(gr) * (f_in[:, 2 * RW:3 * RW] + h)
        bgate = p_ref[0, :, 0:CONV_W]
        uc = p_ref[0, :, CONV_W:2 * CONV_W] * p_ref[0, :, 2 * CONV_W:3 * CONV_W]
        if line_is_chunk:
            zero = jnp.zeros((HALO, CONV_W), F32)
            up, un = zero, zero
        else:
            up = pp_ref[0, :, CONV_W:2 * CONV_W] * pp_ref[0, :, 2 * CONV_W:3 * CONV_W] * keep_prev
            un = pn_ref[0, :, CONV_W:2 * CONV_W] * pn_ref[0, :, 2 * CONV_W:3 * CONV_W] * keep_next
        yc = (conva_ref[0:1, :] * _shift_dn(uc, up, 1) + conva_ref[1:2, :] * uc
              + conva_ref[2:3, :] * _shift_up(uc, un, 1))
        out_ref[0, :, 0:CONV_W] = bgate * yc


def _mixer_pass(p, f_in, s0, h0, prm, d, col_major, line_is_chunk):
    bsz, t_len, _ = p.shape
    n = t_len // CHUNK
    reverse = d == 1
    out_cols = 4 * CONV_W if reverse else F_COLS

    def ci(i):
        return (n - 1 - i) if reverse else i

    if col_major:
        assert n == GRID_W and t_len == GRID_W * CHUNK
        p_view = p.reshape(bsz, CHUNK, n * P_COLS)
        cur_spec = pl.BlockSpec((1, CHUNK, P_COLS), lambda b, i: (b, 0, ci(i)))
        prev_spec = pl.BlockSpec((1, HALO, P_COLS),
                                 lambda b, i: (b, CHUNK // HALO - 1, jnp.maximum(ci(i) - 1, 0)))
        next_spec = pl.BlockSpec((1, HALO, P_COLS), lambda b, i: (b, 0, jnp.minimum(ci(i) + 1, n - 1)))
    else:
        p_view = p
        per = CHUNK // HALO
        cur_spec = pl.BlockSpec((1, CHUNK, P_COLS), lambda b, i: (b, ci(i), 0))
        prev_spec = pl.BlockSpec((1, HALO, P_COLS), lambda b, i: (b, jnp.maximum(per * ci(i) - 1, 0), 0))
        next_spec = pl.BlockSpec((1, HALO, P_COLS),
                                 lambda b, i: (b, jnp.minimum(per * ci(i) + per, per * n - 1), 0))
    full = lambda a: pl.BlockSpec(a.shape, lambda b, i: (0,) * a.ndim)
    in_specs = [cur_spec, prev_spec, next_spec]
    args = [p_view, p_view, p_view]
    if reverse:
        in_specs.append(pl.BlockSpec((1, CHUNK, F_COLS), lambda b, i: (b, ci(i), 0)))
        args.append(f_in)
    state_specs = [pl.BlockSpec((1,) + s0.shape[1:], lambda b, i: (b, 0, 0, 0)),
                   pl.BlockSpec((1, 1, RW), lambda b, i: (b, 0, 0))]
    in_specs += state_specs
    args += [s0, h0]
    params = [prm["vec"][d], prm["mu"], prm["conv_a"], prm["w2"][d], prm["a2"][d], prm["g2"],
              prm["wr"][d], prm["wi"][d]]
    in_specs += [full(a) for a in params]
    args += params

    if reverse and col_major:
        out_shape = jax.ShapeDtypeStruct((bsz, CHUNK, n * out_cols), F32)
        out_spec = pl.BlockSpec((1, CHUNK, out_cols), lambda b, i: (b, 0, ci(i)))
    else:
        out_shape = jax.ShapeDtypeStruct((bsz, t_len, out_cols), F32)
        out_spec = pl.BlockSpec((1, CHUNK, out_cols), lambda b, i: (b, ci(i), 0))

    out, s_fin, h_fin = pl.pallas_call(
        functools.partial(_mixer_kernel, d=d, n_chunks=n, line_is_chunk=line_is_chunk),
        grid=(bsz, n),
        in_specs=in_specs,
        out_specs=[out_spec] + state_specs,
        out_shape=[out_shape, jax.ShapeDtypeStruct(s0.shape, F32), jax.ShapeDtypeStruct(h0.shape, F32)],
        compiler_params=pltpu.CompilerParams(
            dimension_semantics=("parallel", "arbitrary"), vmem_limit_bytes=VMEM_LIMIT),
        name="mixer_bwd" if reverse else "mixer_fwd",
    )(*args)
    return out.reshape(bsz, t_len, out_cols), s_fin, h_fin


def _mix_stream(p, prm, col_major, line_is_chunk, init):
    (s0f, h0f), (s0b, h0b) = init
    f, sf, hf = _mixer_pass(p, None, s0f, h0f, prm, 0, col_major, line_is_chunk)
    y, sb, hb = _mixer_pass(p, f, s0b, h0b, prm, 1, col_major, line_is_chunk)
    return y, ((sf, hf), (sb, hb))


def _mixer_params(l, conv_a, rwkv_mu, rwkv_w0, rwkv_w2, rwkv_a0, rwkv_a2, rwkv_g2, rwkv_kk, rwkv_ka,
                  rwkv_rk, rwkv_lnx_g, rwkv_lnx_b, lru_conv_w, lru_conv_b, lru_w_r, lru_b_r, lru_w_i,
                  lru_b_i, lru_lam):
    def both(a):
        return jnp.broadcast_to(a[None], (2,) + a.shape)

    rows = [rwkv_w0[l], rwkv_a0[l], both(rwkv_kk[l]), both(rwkv_ka[l]), both(rwkv_rk[l]),
            both(rwkv_lnx_g[l]), both(rwkv_lnx_b[l]),
            lru_conv_w[l][:, 0], lru_conv_w[l][:, 1], lru_conv_w[l][:, 2], lru_conv_w[l][:, 3],
            lru_conv_b[l], lru_b_r[l], lru_b_i[l], lru_lam[l], jnp.zeros((2, RW), F32)]
    assert len(rows) == VEC_ROWS
    vec = jnp.stack(rows, axis=1)
    nb = lru_w_r.shape[2]
    eye = jnp.eye(nb, dtype=F32)

    def bd(w):
        return jnp.einsum("dnij,nm->dnimj", w, eye).reshape(2, nb * HEAD, nb * HEAD).astype(BF16)

    return {
        "vec": vec, "mu": rwkv_mu[l], "conv_a": conv_a[l],
        "w2": rwkv_w2[l].astype(BF16), "a2": rwkv_a2[l].astype(BF16), "g2": rwkv_g2[l].astype(BF16),
        "wr": bd(lru_w_r[l]), "wi": bd(lru_w_i[l]),
    }


def kernel(x, c, ctx, c_ctx, w_mod, b_mod, g_ffn1, w_gu1, w_down1, g_mix, w_in, conv_a, rwkv_mu, rwkv_w0, rwkv_w2, rwkv_a0, rwkv_a2, rwkv_g2, rwkv_kk, rwkv_ka, rwkv_rk, rwkv_lnx_g, rwkv_lnx_b, lru_conv_w, lru_conv_b, lru_w_r, lru_b_r, lru_w_i, lru_b_i, lru_lam, w_out, g_ffn2, w_gu2, w_down2, g_final):
    bsz, t_len, d = x.shape
    depth = w_mod.shape[0]
    n_ctx = ctx.shape[1]
    assert w_in.shape[2] == P_COLS and t_len % CHUNK == 0 and n_ctx % CHUNK == 0

    mod_rows = -(-(bsz + 1) // 8) * 8
    cc = jnp.concatenate([c, c_ctx[None, :], jnp.zeros((mod_rows - bsz - 1, d), F32)], axis=0)
    mods = _modulation(cc, w_mod, b_mod)

    zero_state = ((jnp.zeros((bsz, RW // PAIR, PAIR, PAIR), F32), jnp.zeros((bsz, 1, RW), F32)),) * 2
    s_lat = x
    s_ctx = ctx.reshape(1, bsz * n_ctx, d)
    for l in range(depth):
        last = l == depth - 1
        m_lat = mods[l, :bsz].reshape(bsz, 9, d)
        m_ctx = mods[l, bsz:bsz + 1].reshape(1, 9, d)
        wgu1, wd1 = w_gu1[l].astype(BF16), w_down1[l].astype(BF16)
        wgu2, wd2 = w_gu2[l].astype(BF16), w_down2[l].astype(BF16)
        win, wout = w_in[l].astype(BF16), w_out[l].astype(BF16)
        prm = _mixer_params(l, conv_a, rwkv_mu, rwkv_w0, rwkv_w2, rwkv_a0, rwkv_a2, rwkv_g2, rwkv_kk,
                            rwkv_ka, rwkv_rk, rwkv_lnx_g, rwkv_lnx_b, lru_conv_w, lru_conv_b, lru_w_r,
                            lru_b_r, lru_w_i, lru_b_i, lru_lam)

        s_lat = _ffn(s_lat, m_lat, 0, g_ffn1[l], wgu1, wd1)
        s_ctx = _ffn(s_ctx, m_ctx, 0, g_ffn1[l], wgu1, wd1)
        p_lat = _inproj(s_lat, m_lat, 3, g_mix[l], win)
        p_ctx = _inproj(s_ctx, m_ctx, 3, g_mix[l], win)

        y_ctx, ctx_fin = _mix_stream(p_ctx.reshape(bsz, n_ctx, P_COLS), prm, False, False, zero_state)
        y_lat, _ = _mix_stream(p_lat, prm, l % 2 == 1, True, ctx_fin)

        s_lat = _ffn(s_lat, m_lat, 6, g_ffn2[l], wgu2, wd2, y=y_lat, wout=wout,
                     gfin=g_final if last else None)
        if not last:
            s_ctx = _ffn(s_ctx, m_ctx, 6, g_ffn2[l], wgu2, wd2,
                         y=y_ctx.reshape(1, bsz * n_ctx, 4 * CONV_W), wout=wout)
    return s_lat
```

```python
import functools

import jax
import jax.numpy as jnp
from jax import lax
from jax.experimental import pallas as pl
from jax.experimental.pallas import tpu as pltpu

F32 = jnp.float32
BF16 = jnp.bfloat16

HEAD = 64
PAIR = 2 * HEAD
CHUNK = 64
HALO = 8
GRID_W = 64
NORM_EPS = 1e-6
GN_EPS = 64e-5
LRU_C = 8.0
VMEM_LIMIT = 56 * 1024 * 1024


def _dot(a, b):
    return jnp.dot(a.astype(BF16), b.astype(BF16), preferred_element_type=F32)


def _split2(a):
    hi = a.astype(BF16)
    lo = (a - hi.astype(F32)).astype(BF16)
    return hi, lo


_NN = (((1,), (0,)), ((), ()))
_NT = (((1,), (1,)), ((), ()))
_TN = (((0,), (0,)), ((), ()))


def _dot3(a, b, dims=_NN):
    ah, al = _split2(a)
    bh, bl = _split2(b)
    d = functools.partial(lax.dot_general, dimension_numbers=dims, preferred_element_type=F32)
    return d(ah, bh) + (d(ah, bl) + d(al, bh))


def _mm3(a, b, dims=_NN):
    ah, al = a
    bh, bl = b
    d = functools.partial(lax.dot_general, dimension_numbers=dims, preferred_element_type=F32)
    return d(ah, bh) + (d(ah, bl) + d(al, bh))


def _cat(parts, axis):
    return tuple(jnp.concatenate([p[i] for p in parts], axis=axis) for i in range(2))


def _lanes(sp, sl):
    return tuple(x[:, sl] for x in sp)


def _bd(sp):
    return tuple(_blockdiag(x) for x in sp)


def _dot_exact_rhs(a, ones_rhs):
    a1 = a.astype(BF16)
    r1 = a - a1.astype(F32)
    a2 = r1.astype(BF16)
    a3 = (r1 - a2.astype(F32)).astype(BF16)
    d = functools.partial(jnp.dot, preferred_element_type=F32)
    return d(a1, ones_rhs) + (d(a2, ones_rhs) + d(a3, ones_rhs))


def _dot_exact_lhs(ones_lhs, b):
    b1 = b.astype(BF16)
    r1 = b - b1.astype(F32)
    b2 = r1.astype(BF16)
    b3 = (r1 - b2.astype(F32)).astype(BF16)
    d = functools.partial(jnp.dot, preferred_element_type=F32)
    return d(ones_lhs, b1) + (d(ones_lhs, b2) + d(ones_lhs, b3))


def _softplus(x):
    return jnp.maximum(x, 0.0) + jnp.log1p(jnp.exp(-jnp.abs(x)))


def _rms_modulate(s, g, shift, scale):
    y = s * lax.rsqrt(jnp.mean(s * s, axis=-1, keepdims=True) + NORM_EPS)
    return (y * g) * (1.0 + scale) + shift


def _shift_dn(x, prev, k):
    ch = x.shape[1]
    rolled = pltpu.roll(x, k, axis=0)
    pr = pltpu.roll(prev, k, axis=0)
    r8 = lax.broadcasted_iota(jnp.int32, (HALO, ch), 0)
    head = jnp.where(r8 < k, pr, rolled[:HALO])
    return jnp.concatenate([head, rolled[HALO:]], axis=0)


def _shift_up(x, nxt, k):
    n, ch = x.shape
    rolled = pltpu.roll(x, n - k, axis=0)
    nr = pltpu.roll(nxt, HALO - k, axis=0)
    r8 = lax.broadcasted_iota(jnp.int32, (HALO, ch), 0)
    tail = jnp.where(r8 >= HALO - k, nr, rolled[n - HALO:])
    return jnp.concatenate([rolled[:n - HALO], tail], axis=0)


def _scan_shift(x, k, fill, reverse):
    n, ch = x.shape
    rows = lax.broadcasted_iota(jnp.int32, (n, ch), 0)
    if not reverse:
        return jnp.where(rows >= k, pltpu.roll(x, k, axis=0), fill)
    return jnp.where(rows < n - k, pltpu.roll(x, n - k, axis=0), fill)


def _blockdiag(xp):
    n = xp.shape[0]
    lane = lax.broadcasted_iota(jnp.int32, (n, PAIR), 1)
    zero = jnp.zeros_like(xp)
    top = jnp.where(lane < HEAD, xp, zero)
    bot = jnp.where(lane >= HEAD, xp, zero)
    return jnp.concatenate([top, bot], axis=0)


def _mod_kernel(c_ref, w_ref, b_ref, o_ref):
    cc = c_ref[...]
    act = cc * jax.nn.sigmoid(cc)
    o_ref[0] = _dot(act, w_ref[0]) + b_ref[0]


def _modulation(cc, w_mod, b_mod):
    depth, d, n = w_mod.shape
    rows = cc.shape[0]
    tn = n // 8
    return pl.pallas_call(
        _mod_kernel,
        grid=(depth, n // tn),
        in_specs=[
            pl.BlockSpec((rows, d), lambda l, j: (0, 0)),
            pl.BlockSpec((1, d, tn), lambda l, j: (l, 0, j)),
            pl.BlockSpec((1, 1, tn), lambda l, j: (l, 0, j)),
        ],
        out_specs=pl.BlockSpec((1, rows, tn), lambda l, j: (l, 0, j)),
        out_shape=jax.ShapeDtypeStruct((depth, rows, n), F32),
        compiler_params=pltpu.CompilerParams(
            dimension_semantics=("parallel", "parallel"), vmem_limit_bytes=VMEM_LIMIT),
        name="modulation",
    )(cc, w_mod, b_mod.reshape(depth, 1, n))


def _ffn_kernel(*refs, mod_base, has_pre, final_norm, d_ff, tf):
    it = iter(refs)
    s_ref = next(it)
    if has_pre:
        y_ref = next(it)
        wout_ref = next(it)
    mod_ref = next(it)
    g_ref = next(it)
    wgu_ref = next(it)
    wdown_ref = next(it)
    if final_norm:
        gfin_ref = next(it)
    o_ref = next(it)
    acc_ref = next(it)

    s = s_ref[0]
    if has_pre:
        gate_mix = mod_ref[0, mod_base - 1:mod_base, :]
        s = s + gate_mix * _dot(y_ref[0], wout_ref[...])
    shift = mod_ref[0, mod_base:mod_base + 1, :]
    scale = mod_ref[0, mod_base + 1:mod_base + 2, :]
    gate = mod_ref[0, mod_base + 2:mod_base + 3, :]
    hb = _rms_modulate(s, g_ref[...], shift, scale).astype(BF16)
    for j in range(d_ff // tf):
        gt = jnp.dot(hb, wgu_ref[:, j * tf:(j + 1) * tf], preferred_element_type=F32)
        up = jnp.dot(hb, wgu_ref[:, d_ff + j * tf:d_ff + (j + 1) * tf], preferred_element_type=F32)
        act = ((gt * jax.nn.sigmoid(gt)) * up).astype(BF16)
        part = jnp.dot(act, wdown_ref[j * tf:(j + 1) * tf, :], preferred_element_type=F32)
        if j == 0:
            acc_ref[...] = part
        else:
            acc_ref[...] += part
    out = s + (0.5 * gate) * acc_ref[...]
    if final_norm:
        out = out * lax.rsqrt(jnp.mean(out * out, axis=-1, keepdims=True) + NORM_EPS) * gfin_ref[...]
    o_ref[0] = out


def _ffn(s, mod, mod_base, g, wgu, wdown, y=None, wout=None, gfin=None, tm=512):
    bv, tv, d = s.shape
    d_ff = wdown.shape[0]
    tf = 256
    has_pre = y is not None
    final_norm = gfin is not None
    const = lambda b, i: (0, 0)
    tile = lambda b, i: (b, i, 0)
    in_specs = [pl.BlockSpec((1, tm, d), tile)]
    args = [s]
    if has_pre:
        in_specs += [pl.BlockSpec((1, tm, y.shape[2]), tile), pl.BlockSpec(wout.shape, const)]
        args += [y, wout]
    in_specs += [
        pl.BlockSpec((1, mod.shape[1], d), lambda b, i: (b, 0, 0)),
        pl.BlockSpec((1, d), const),
        pl.BlockSpec(wgu.shape, const),
        pl.BlockSpec(wdown.shape, const),
    ]
    args += [mod, g.reshape(1, d), wgu, wdown]
    if final_norm:
        in_specs.append(pl.BlockSpec((1, d), const))
        args.append(gfin.reshape(1, d))
    body = functools.partial(_ffn_kernel, mod_base=mod_base, has_pre=has_pre,
                             final_norm=final_norm, d_ff=d_ff, tf=tf)
    return pl.pallas_call(
        body,
        grid=(bv, tv // tm),
        in_specs=in_specs,
        out_specs=pl.BlockSpec((1, tm, d), tile),
        out_shape=jax.ShapeDtypeStruct(s.shape, F32),
        scratch_shapes=[pltpu.VMEM((tm, d), F32)],
        compiler_params=pltpu.CompilerParams(
            dimension_semantics=("parallel", "parallel"), vmem_limit_bytes=VMEM_LIMIT),
        name="ffn",
    )(*args)


def _inproj_kernel(s_ref, mod_ref, g_ref, w_ref, o_ref, *, mod_base):
    shift = mod_ref[0, mod_base:mod_base + 1, :]
    scale = mod_ref[0, mod_base + 1:mod_base + 2, :]
    hb = _rms_modulate(s_ref[0], g_ref[...], shift, scale).astype(BF16)
    o_ref[0] = jnp.dot(hb, w_ref[...], preferred_element_type=F32)


def _inproj(s, mod, mod_base, g, w, tm=512):
    bv, tv, d = s.shape
    n = w.shape[1]
    const = lambda b, i: (0, 0)
    return pl.pallas_call(
        functools.partial(_inproj_kernel, mod_base=mod_base),
        grid=(bv, tv // tm),
        in_specs=[
            pl.BlockSpec((1, tm, d), lambda b, i: (b, i, 0)),
            pl.BlockSpec((1, mod.shape[1], d), lambda b, i: (b, 0, 0)),
            pl.BlockSpec((1, d), const),
            pl.BlockSpec(w.shape, const),
        ],
        out_specs=pl.BlockSpec((1, tm, n), lambda b, i: (b, i, 0)),
        out_shape=jax.ShapeDtypeStruct((bv, tv, n), F32),
        compiler_params=pltpu.CompilerParams(
            dimension_semantics=("parallel", "parallel"), vmem_limit_bytes=VMEM_LIMIT),
        name="inproj",
    )(s, mod, g.reshape(1, d), w)


CONV_W = 256
RW = 384
OFF_B = 3 * CONV_W
RWKV_COLS = 3 * RW + 64 + 64 + 128
OFF_C = OFF_B + RWKV_COLS
P_COLS = OFF_C + 2 * RW
F_COLS = 3 * RW
V_W0, V_A0, V_KK, V_KA, V_RK, V_LNG, V_LNB, V_CW, V_CB, V_BR, V_BI, V_LAM = 0, 1, 2, 3, 4, 5, 6, 7, 11, 12, 13, 14
VEC_ROWS = 16


def _mixer_kernel(*refs, d, n_chunks, line_is_chunk):
    it = iter(refs)
    p_ref, pp_ref, pn_ref = next(it), next(it), next(it)
    if d == 1:
        f_in_ref = next(it)
    s0_ref, h0_ref = next(it), next(it)
    vec_ref, mu_ref, conva_ref = next(it), next(it), next(it)
    w2_ref, a2_ref, g2_ref, wr_ref, wi_ref = next(it), next(it), next(it), next(it), next(it)
    out_ref, s_ref, h_ref = next(it), next(it), next(it)

    reverse = d == 1
    i = pl.program_id(1)
    ci = (n_chunks - 1 - i) if reverse else i
    keep_prev = jnp.where(ci == 0, 0.0, 1.0)
    keep_next = jnp.where(ci == n_chunks - 1, 0.0, 1.0)

    @pl.when(i == 0)
    def _():
        s_ref[...] = s0_ref[...]
        h_ref[...] = h0_ref[...]

    def vec(r, sl=slice(None)):
        return vec_ref[r:r + 1, sl]

    cur = p_ref[0, :, OFF_B:OFF_C]
    prev = pp_ref[0, :, OFF_B:OFF_C] * keep_prev
    nxt = pn_ref[0, :, OFF_B:OFF_C] * keep_next
    pb = (cur + mu_ref[0:1, :] * (_shift_dn(cur, prev, 1) - cur)
          + mu_ref[1:2, :] * (_shift_up(cur, nxt, 1) - cur))
    r = pb[:, 0:RW]
    k = pb[:, RW:2 * RW]
    v = pb[:, 2 * RW:3 * RW]
    dw = pb[:, 3 * RW:3 * RW + 64]
    da = pb[:, 3 * RW + 64:3 * RW + 128]
    dg = pb[:, 3 * RW + 128:3 * RW + 256]

    lane_p = lax.broadcasted_iota(jnp.int32, (PAIR, PAIR), 1)
    row_p = lax.broadcasted_iota(jnp.int32, (PAIR, PAIR), 0)
    same_head = (lane_p >= HEAD) == (row_p >= HEAD)
    head_ones = jnp.where(same_head, 1.0, 0.0).astype(BF16)

    def head_sum(x):
        return jnp.concatenate(
            [_dot_exact_rhs(x[:, PAIR * hp:PAIR * (hp + 1)], head_ones) for hp in range(RW // PAIR)],
            axis=1)

    kkr = k * vec(V_KK)
    kk = kkr * lax.rsqrt(head_sum(kkr * kkr) + 1e-12)
    w_log = -_softplus(-(vec(V_W0) + _dot(jnp.tanh(dw), w2_ref[...]))) - 0.5
    lw = -jnp.exp(w_log)
    a = jax.nn.sigmoid(vec(V_A0) + _dot(da, a2_ref[...]))
    kd = k * (1.0 + (a - 1.0) * vec(V_KA))
    beta = kk * a
    bonus = head_sum(r * kd * vec(V_RK)) * v

    row_c = lax.broadcasted_iota(jnp.int32, (CHUNK, CHUNK), 0)
    col_c = lax.broadcasted_iota(jnp.int32, (CHUNK, CHUNK), 1)
    incl_c = (col_c >= row_c) if reverse else (col_c <= row_c)
    c_inc = _dot_exact_lhs(jnp.where(incl_c, 1.0, 0.0).astype(BF16), lw)
    c_exc = c_inc - lw
    c_tot = c_inc[0:1, :] if reverse else c_inc[CHUNK - 1:CHUNK, :]
    c_mid = c_inc[CHUNK // 2:CHUNK // 2 + 1, :]
    e_exc = jnp.exp(c_exc)
    e_inc = jnp.exp(c_inc)
    e_rem = jnp.exp(c_tot - c_inc)
    e_neg_mid = jnp.exp(-c_mid)
    e_mid_tot = jnp.exp(c_mid - c_tot)
    gam = jnp.exp(c_tot)
    at_abs = -(kk * e_exc)
    rt_abs = r * e_inc
    bg = beta * e_rem
    kg = kd * e_rem
    at_off = at_abs * e_neg_mid
    rt_off = rt_abs * e_neg_mid
    bh_off = bg * e_mid_tot
    kh_off = kg * e_mid_tot

    row_h = lax.broadcasted_iota(jnp.int32, (CHUNK, PAIR), 0)
    col_h = lax.broadcasted_iota(jnp.int32, (CHUNK, PAIR), 1) & (HEAD - 1)
    strict = (col_h > row_h) if reverse else (col_h < row_h)
    incl = (col_h >= row_h) if reverse else (col_h <= row_h)
    eye_h = jnp.where(col_h == row_h, 1.0, 0.0)
    eye_p = lane_p == row_p

    if reverse:
        g = _dot(jax.nn.sigmoid(dg), g2_ref[...])
        f_in = f_in_ref[0]

    n_pair = RW // PAIR
    sls = [slice(PAIR * hp, PAIR * (hp + 1)) for hp in range(n_pair)]
    s_sp = [_split2(s_ref[0, hp]) for hp in range(n_pair)]
    at_off_sp, rt_off_sp = _split2(at_off), _split2(rt_off)
    bh_off_sp, kh_off_sp = _split2(bh_off), _split2(kh_off)
    at_abs_sp, rt_abs_sp = _split2(at_abs), _split2(rt_abs)
    bg_sp, kg_sp, v_sp = _split2(bg), _split2(kg), _split2(v)
    gm = [_mm3(_cat([_lanes(at_off_sp, sl), _lanes(rt_off_sp, sl)], 0),
               _cat([_bd(_lanes(bh_off_sp, sl)), _bd(_lanes(kh_off_sp, sl))], 0), _NT) for sl in sls]
    a_m = [jnp.where(strict, m[0:CHUNK, 0:PAIR], 0.0) for m in gm]
    b_m = [jnp.where(strict, m[0:CHUNK, PAIR:2 * PAIR], 0.0) for m in gm]
    p_m = [jnp.where(incl, m[CHUNK:2 * CHUNK, 0:PAIR], 0.0) for m in gm]
    q_m = [jnp.where(incl, m[CHUNK:2 * CHUNK, PAIR:2 * PAIR], 0.0) for m in gm]
    t_m = [eye_h + m for m in a_m]
    pw_sp = [_split2(m) for m in a_m]
    for _ in range(5):
        pw_sp = [_split2(_mm3(ps, _bd(ps))) for ps in pw_sp]
        t_m = [t + _mm3(_split2(t), _bd(ps)) for t, ps in zip(t_m, pw_sp)]
    z = [_mm3(_cat([_lanes(at_abs_sp, sl), _lanes(rt_abs_sp, sl)], 0), s_sp[hp]) for hp, sl in enumerate(sls)]
    w_m = [z[hp][0:CHUNK] + _mm3(_split2(b_m[hp]), _bd(_lanes(v_sp, sl))) for hp, sl in enumerate(sls)]
    u_sp = [_split2(_mm3(_split2(t), _bd(_split2(w)))) for t, w in zip(t_m, w_m)]
    y_rw = [z[hp][CHUNK:2 * CHUNK] + _mm3(
        _cat([_split2(p_m[hp]), _split2(q_m[hp])], 1),
        _cat([_bd(u_sp[hp]), _bd(_lanes(v_sp, sl))], 0)) for hp, sl in enumerate(sls)]
    s_upd = []
    for hp, sl in enumerate(sls):
        dgam = jnp.where(eye_p, jnp.broadcast_to(gam[:, sl], (PAIR, PAIR)), 0.0)
        s_new = _mm3(_cat([_lanes(bg_sp, sl), _lanes(kg_sp, sl), _split2(dgam)], 0),
                     _cat([u_sp[hp], _lanes(v_sp, sl), s_sp[hp]], 0), _TN)
        s_upd.append(jnp.where(same_head, s_new, 0.0))

    if not reverse:
        y_out = y_rw
    else:
        y_out = []
        for hp, sl in enumerate(sls):
            ysum = y_rw[hp] + f_in[:, sl]
            mean = _dot_exact_rhs(ysum, head_ones) * (1.0 / HEAD)
            cen = ysum - mean
            var = _dot_exact_rhs(cen * cen, head_ones) * (1.0 / HEAD)
            gn = cen * lax.rsqrt(var + GN_EPS) * vec(V_LNG, sl) + vec(V_LNB, sl)
            y_out.append((gn + (f_in[:, RW + PAIR * hp:RW + PAIR * (hp + 1)] + bonus[:, sl])) * g[:, sl])

    y_base = CONV_W if reverse else 0
    for hp in range(RW // PAIR):
        s_ref[0, hp] = s_upd[hp]
        out_ref[0, :, y_base + PAIR * hp:y_base + PAIR * (hp + 1)] = y_out[hp]
    if not reverse:
        out_ref[0, :, RW:2 * RW] = bonus

    xr = p_ref[0, :, OFF_C:OFF_C + RW]
    if not reverse:
        xh = pp_ref[0, :, OFF_C:OFF_C + RW] * keep_prev
        u_l = (vec(V_CW + 3) * xr + vec(V_CW + 2) * _shift_dn(xr, xh, 1)
               + vec(V_CW + 1) * _shift_dn(xr, xh, 2) + vec(V_CW) * _shift_dn(xr, xh, 3))
    else:
        xh = pn_ref[0, :, OFF_C:OFF_C + RW] * keep_next
        u_l = (vec(V_CW) * xr + vec(V_CW + 1) * _shift_up(xr, xh, 1)
               + vec(V_CW + 2) * _shift_up(xr, xh, 2) + vec(V_CW + 3) * _shift_up(xr, xh, 3))
    u_l = u_l + vec(V_CB)
    rg = jax.nn.sigmoid(_dot(u_l, wr_ref[...]) + vec(V_BR))
    ig = jax.nn.sigmoid(_dot(u_l, wi_ref[...]) + vec(V_BI))
    log_a = -LRU_C * (_softplus(-vec(V_LAM)) * rg)
    a_l = jnp.exp(log_a)
    th = jnp.tanh(log_a)
    b_l = jnp.sqrt(-2.0 * th / (1.0 - th)) * (ig * u_l)
    step = 1
    while step < CHUNK:
        b_l = b_l + a_l * _scan_shift(b_l, step, 0.0, reverse)
        a_l = a_l * _scan_shift(a_l, step, 1.0, reverse)
        step *= 2
    h = b_l + a_l * h_ref[0]
    h_ref[0] = h[0:1, :] if reverse else h[CHUNK - 1:CHUNK, :]

    if not reverse:
        out_ref[0, :, 2 * RW:3 * RW] = h
    else:
        gr = p_ref[0, :, OFF_C + RW:OFF_C + 2 * RW]
        out_ref[0, :, CONV_W + RW:CONV_W + 2 * RW] = jax.nn.gelu(gr) * (f_in[:, 2 * RW:3 * RW] + h)
        bgate = p_ref[0, :, 0:CONV_W]
        uc = p_ref[0, :, CONV_W:2 * CONV_W] * p_ref[0, :, 2 * CONV_W:3 * CONV_W]
        if line_is_chunk:
            zero = jnp.zeros((HALO, CONV_W), F32)
            up, un = zero, zero
        else:
            up = pp_ref[0, :, CONV_W:2 * CONV_W] * pp_ref[0, :, 2 * CONV_W:3 * CONV_W] * keep_prev
            un = pn_ref[0, :, CONV_W:2 * CONV_W] * pn_ref[0, :, 2 * CONV_W:3 * CONV_W] * keep_next
        yc = (conva_ref[0:1, :] * _shift_dn(uc, up, 1) + conva_ref[1:2, :] * uc
              + conva_ref[2:3, :] * _shift_up(uc, un, 1))
        out_ref[0, :, 0:CONV_W] = bgate * yc


def _mixer_pass(p, f_in, s0, h0, prm, d, col_major, line_is_chunk):
    bsz, t_len, _ = p.shape
    n = t_len // CHUNK
    reverse = d == 1
    out_cols = 4 * CONV_W if reverse else F_COLS

    def ci(i):
        return (n - 1 - i) if reverse else i

    if col_major:
        assert n == GRID_W and t_len == GRID_W * CHUNK
        p_view = p.reshape(bsz, CHUNK, n * P_COLS)
        cur_spec = pl.BlockSpec((1, CHUNK, P_COLS), lambda b, i: (b, 0, ci(i)))
        prev_spec = pl.BlockSpec((1, HALO, P_COLS),
                                 lambda b, i: (b, CHUNK // HALO - 1, jnp.maximum(ci(i) - 1, 0)))
        next_spec = pl.BlockSpec((1, HALO, P_COLS), lambda b, i: (b, 0, jnp.minimum(ci(i) + 1, n - 1)))
    else:
        p_view = p
        per = CHUNK // HALO
        cur_spec = pl.BlockSpec((1, CHUNK, P_COLS), lambda b, i: (b, ci(i), 0))
        prev_spec = pl.BlockSpec((1, HALO, P_COLS), lambda b, i: (b, jnp.maximum(per * ci(i) - 1, 0), 0))
        next_spec = pl.BlockSpec((1, HALO, P_COLS),
                                 lambda b, i: (b, jnp.minimum(per * ci(i) + per, per * n - 1), 0))
    full = lambda a: pl.BlockSpec(a.shape, lambda b, i: (0,) * a.ndim)
    in_specs = [cur_spec, prev_spec, next_spec]
    args = [p_view, p_view, p_view]
    if reverse:
        in_specs.append(pl.BlockSpec((1, CHUNK, F_COLS), lambda b, i: (b, ci(i), 0)))
        args.append(f_in)
    state_specs = [pl.BlockSpec((1,) + s0.shape[1:], lambda b, i: (b, 0, 0, 0)),
                   pl.BlockSpec((1, 1, RW), lambda b, i: (b, 0, 0))]
    in_specs += state_specs
    args += [s0, h0]
    params = [prm["vec"][d], prm["mu"], prm["conv_a"], prm["w2"][d], prm["a2"][d], prm["g2"],
              prm["wr"][d], prm["wi"][d]]
    in_specs += [full(a) for a in params]
    args += params

    if reverse and col_major:
        out_shape = jax.ShapeDtypeStruct((bsz, CHUNK, n * out_cols), F32)
        out_spec = pl.BlockSpec((1, CHUNK, out_cols), lambda b, i: (b, 0, ci(i)))
    else:
        out_shape = jax.ShapeDtypeStruct((bsz, t_len, out_cols), F32)
        out_spec = pl.BlockSpec((1, CHUNK, out_cols), lambda b, i: (b, ci(i), 0))

    out, s_fin, h_fin = pl.pallas_call(
        functools.partial(_mixer_kernel, d=d, n_chunks=n, line_is_chunk=line_is_chunk),
        grid=(bsz, n),
        in_specs=in_specs,
        out_specs=[out_spec] + state_specs,
        out_shape=[out_shape, jax.ShapeDtypeStruct(s0.shape, F32), jax.ShapeDtypeStruct(h0.shape, F32)],
        compiler_params=pltpu.CompilerParams(
            dimension_semantics=("parallel", "arbitrary"), vmem_limit_bytes=VMEM_LIMIT),
        name="mixer_bwd" if reverse else "mixer_fwd",
    )(*args)
    return out.reshape(bsz, t_len, out_cols), s_fin, h_fin


def _mix_stream(p, prm, col_major, line_is_chunk, init):
    (s0f, h0f), (s0b, h0b) = init
    f, sf, hf = _mixer_pass(p, None, s0f, h0f, prm, 0, col_major, line_is_chunk)
    y, sb, hb = _mixer_pass(p, f, s0b, h0b, prm, 1, col_major, line_is_chunk)
    return y, ((sf, hf), (sb, hb))


def _mixer_params(l, conv_a, rwkv_mu, rwkv_w0, rwkv_w2, rwkv_a0, rwkv_a2, rwkv_g2, rwkv_kk, rwkv_ka,
                  rwkv_rk, rwkv_lnx_g, rwkv_lnx_b, lru_conv_w, lru_conv_b, lru_w_r, lru_b_r, lru_w_i,
                  lru_b_i, lru_lam):
    def both(a):
        return jnp.broadcast_to(a[None], (2,) + a.shape)

    rows = [rwkv_w0[l], rwkv_a0[l], both(rwkv_kk[l]), both(rwkv_ka[l]), both(rwkv_rk[l]),
            both(rwkv_lnx_g[l]), both(rwkv_lnx_b[l]),
            lru_conv_w[l][:, 0], lru_conv_w[l][:, 1], lru_conv_w[l][:, 2], lru_conv_w[l][:, 3],
            lru_conv_b[l], lru_b_r[l], lru_b_i[l], lru_lam[l], jnp.zeros((2, RW), F32)]
    assert len(rows) == VEC_ROWS
    vec = jnp.stack(rows, axis=1)
    nb = lru_w_r.shape[2]
    eye = jnp.eye(nb, dtype=F32)

    def bd(w):
        return jnp.einsum("dnij,nm->dnimj", w, eye).reshape(2, nb * HEAD, nb * HEAD).astype(BF16)

    return {
        "vec": vec, "mu": rwkv_mu[l], "conv_a": conv_a[l],
        "w2": rwkv_w2[l].astype(BF16), "a2": rwkv_a2[l].astype(BF16), "g2": rwkv_g2[l].astype(BF16),
        "wr": bd(lru_w_r[l]), "wi": bd(lru_w_i[l]),
    }


def kernel(x, c, ctx, c_ctx, w_mod, b_mod, g_ffn1, w_gu1, w_down1, g_mix, w_in, conv_a, rwkv_mu, rwkv_w0, rwkv_w2, rwkv_a0, rwkv_a2, rwkv_g2, rwkv_kk, rwkv_ka, rwkv_rk, rwkv_lnx_g, rwkv_lnx_b, lru_conv_w, lru_conv_b, lru_w_r, lru_b_r, lru_w_i, lru_b_i, lru_lam, w_out, g_ffn2, w_gu2, w_down2, g_final):
    bsz, t_len, d = x.shape
    depth = w_mod.shape[0]
    n_ctx = ctx.shape[1]
    assert w_in.shape[2] == P_COLS and t_len % CHUNK == 0 and n_ctx % CHUNK == 0

    mod_rows = -(-(bsz + 1) // 8) * 8
    cc = jnp.concatenate([c, c_ctx[None, :], jnp.zeros((mod_rows - bsz - 1, d), F32)], axis=0)
    mods = _modulation(cc, w_mod, b_mod)

    zero_state = ((jnp.zeros((bsz, RW // PAIR, PAIR, PAIR), F32), jnp.zeros((bsz, 1, RW), F32)),) * 2
    s_lat = x
    s_ctx = ctx.reshape(1, bsz * n_ctx, d)
    for l in range(depth):
        last = l == depth - 1
        m_lat = mods[l, :bsz].reshape(bsz, 9, d)
        m_ctx = mods[l, bsz:bsz + 1].reshape(1, 9, d)
        wgu1, wd1 = w_gu1[l].astype(BF16), w_down1[l].astype(BF16)
        wgu2, wd2 = w_gu2[l].astype(BF16), w_down2[l].astype(BF16)
        win, wout = w_in[l].astype(BF16), w_out[l].astype(BF16)
        prm = _mixer_params(l, conv_a, rwkv_mu, rwkv_w0, rwkv_w2, rwkv_a0, rwkv_a2, rwkv_g2, rwkv_kk,
                            rwkv_ka, rwkv_rk, rwkv_lnx_g, rwkv_lnx_b, lru_conv_w, lru_conv_b, lru_w_r,
                            lru_b_r, lru_w_i, lru_b_i, lru_lam)

        s_lat = _ffn(s_lat, m_lat, 0, g_ffn1[l], wgu1, wd1)
        s_ctx = _ffn(s_ctx, m_ctx, 0, g_ffn1[l], wgu1, wd1)
        p_lat = _inproj(s_lat, m_lat, 3, g_mix[l], win)
        p_ctx = _inproj(s_ctx, m_ctx, 3, g_mix[l], win)

        y_ctx, ctx_fin = _mix_stream(p_ctx.reshape(bsz, n_ctx, P_COLS), prm, False, False, zero_state)
        y_lat, _ = _mix_stream(p_lat, prm, l % 2 == 1, True, ctx_fin)

        s_lat = _ffn(s_lat, m_lat, 6, g_ffn2[l], wgu2, wd2, y=y_lat, wout=wout,
                     gfin=g_final if last else None)
        if not last:
            s_ctx = _ffn(s_ctx, m_ctx, 6, g_ffn2[l], wgu2, wd2,
                         y=y_ctx.reshape(1, bsz * n_ctx, 4 * CONV_W), wout=wout)
    return s_lat
```

```python
import functools

import jax
import jax.numpy as jnp
from jax import lax
from jax.experimental import pallas as pl
from jax.experimental.pallas import tpu as pltpu

F32 = jnp.float32
BF16 = jnp.bfloat16

HEAD = 64
PAIR = 2 * HEAD
CHUNK = 64
HALO = 8
MIX_NB = 2
GRID_W = 64
NORM_EPS = 1e-6
GN_EPS = 64e-5
LRU_C = 8.0
DECAY_SCALE = 0.6065306597126334
VMEM_LIMIT = 56 * 1024 * 1024


def _dot(a, b):
    return jnp.dot(a.astype(BF16), b.astype(BF16), preferred_element_type=F32)


def _split2(a):
    hi = a.astype(BF16)
    lo = (a - hi.astype(F32)).astype(BF16)
    return hi, lo


_NN = (((1,), (0,)), ((), ()))
_NT = (((1,), (1,)), ((), ()))
_TN = (((0,), (0,)), ((), ()))


def _dot3(a, b, dims=_NN):
    ah, al = _split2(a)
    bh, bl = _split2(b)
    d = functools.partial(lax.dot_general, dimension_numbers=dims, preferred_element_type=F32)
    return d(ah, bh) + (d(ah, bl) + d(al, bh))


def _mm3(a, b, dims=_NN):
    ah, al = a
    bh, bl = b
    d = functools.partial(lax.dot_general, dimension_numbers=dims, preferred_element_type=F32)
    return d(ah, bh) + (d(ah, bl) + d(al, bh))


def _cat(parts, axis):
    return tuple(jnp.concatenate([p[i] for p in parts], axis=axis) for i in range(2))


def _lanes(sp, sl):
    return tuple(x[:, sl] for x in sp)


def _bd(sp):
    return tuple(_blockdiag(x) for x in sp)


def _dot_ones_rhs(a, ones_rhs):
    ah, al = _split2(a)
    d = functools.partial(jnp.dot, preferred_element_type=F32)
    return d(ah, ones_rhs) + d(al, ones_rhs)


def _dot_exact_lhs(ones_lhs, b):
    b1 = b.astype(BF16)
    r1 = b - b1.astype(F32)
    b2 = r1.astype(BF16)
    b3 = (r1 - b2.astype(F32)).astype(BF16)
    d = functools.partial(jnp.dot, preferred_element_type=F32)
    return d(ones_lhs, b1) + (d(ones_lhs, b2) + d(ones_lhs, b3))


def _softplus(x):
    return jnp.maximum(x, 0.0) + jnp.log1p(jnp.exp(-jnp.abs(x)))


def _rms_modulate(s, g, shift, scale):
    y = s * lax.rsqrt(jnp.mean(s * s, axis=-1, keepdims=True) + NORM_EPS)
    return (y * g) * (1.0 + scale) + shift


def _per_group(fn, x, halo, *args):
    nb = x.shape[0] // CHUNK
    if nb == 1:
        return fn(x, halo, *args)
    return jnp.concatenate(
        [fn(x[b * CHUNK:(b + 1) * CHUNK], None if halo is None else halo[b * HALO:(b + 1) * HALO], *args)
         for b in range(nb)], axis=0)


def _shift_dn(x, prev, k):
    return _per_group(_shift_dn1, x, prev, k)


def _shift_up(x, nxt, k):
    return _per_group(_shift_up1, x, nxt, k)


def _scan_shift(x, k, fill, reverse):
    return _per_group(_scan_shift1, x, None, k, fill, reverse)


def _rows_to_groups(rows):
    return jnp.concatenate([jnp.broadcast_to(r, (CHUNK, r.shape[1])) for r in rows], axis=0)


def _shift_dn1(x, prev, k):
    ch = x.shape[1]
    rolled = pltpu.roll(x, k, axis=0)
    pr = pltpu.roll(prev, k, axis=0)
    r8 = lax.broadcasted_iota(jnp.int32, (HALO, ch), 0)
    head = jnp.where(r8 < k, pr, rolled[:HALO])
    return jnp.concatenate([head, rolled[HALO:]], axis=0)


def _shift_up1(x, nxt, k):
    n, ch = x.shape
    rolled = pltpu.roll(x, n - k, axis=0)
    nr = pltpu.roll(nxt, HALO - k, axis=0)
    r8 = lax.broadcasted_iota(jnp.int32, (HALO, ch), 0)
    tail = jnp.where(r8 >= HALO - k, nr, rolled[n - HALO:])
    return jnp.concatenate([rolled[:n - HALO], tail], axis=0)


def _scan_shift1(x, _, k, fill, reverse):
    n, ch = x.shape
    rows = lax.broadcasted_iota(jnp.int32, (n, ch), 0)
    if not reverse:
        return jnp.where(rows >= k, pltpu.roll(x, k, axis=0), fill)
    return jnp.where(rows < n - k, pltpu.roll(x, n - k, axis=0), fill)


def _blockdiag(xp):
    n = xp.shape[0]
    lane = lax.broadcasted_iota(jnp.int32, (n, PAIR), 1)
    zero = jnp.zeros_like(xp)
    top = jnp.where(lane < HEAD, xp, zero)
    bot = jnp.where(lane >= HEAD, xp, zero)
    return jnp.concatenate([top, bot], axis=0)


def _mod_kernel(c_ref, w_ref, b_ref, o_ref):
    cc = c_ref[...]
    act = cc * jax.nn.sigmoid(cc)
    o_ref[0] = _dot(act, w_ref[0]) + b_ref[0]


def _modulation(cc, w_mod, b_mod):
    depth, d, n = w_mod.shape
    rows = cc.shape[0]
    tn = n // 8
    return pl.pallas_call(
        _mod_kernel,
        grid=(depth, n // tn),
        in_specs=[
            pl.BlockSpec((rows, d), lambda l, j: (0, 0)),
            pl.BlockSpec((1, d, tn), lambda l, j: (l, 0, j)),
            pl.BlockSpec((1, 1, tn), lambda l, j: (l, 0, j)),
        ],
        out_specs=pl.BlockSpec((1, rows, tn), lambda l, j: (l, 0, j)),
        out_shape=jax.ShapeDtypeStruct((depth, rows, n), F32),
        compiler_params=pltpu.CompilerParams(
            dimension_semantics=("parallel", "parallel"), vmem_limit_bytes=VMEM_LIMIT),
        name="modulation",
    )(cc, w_mod, b_mod.reshape(depth, 1, n))


def _ffn_kernel(*refs, mod_base, has_pre, final_norm, d_ff, tf):
    it = iter(refs)
    s_ref = next(it)
    if has_pre:
        y_ref = next(it)
        wout_ref = next(it)
    mod_ref = next(it)
    g_ref = next(it)
    wgu_ref = next(it)
    wdown_ref = next(it)
    if final_norm:
        gfin_ref = next(it)
    o_ref = next(it)
    acc_ref = next(it)

    s = s_ref[0]
    if has_pre:
        gate_mix = mod_ref[0, mod_base - 1:mod_base, :]
        s = s + gate_mix * _dot(y_ref[0], wout_ref[...])
    shift = mod_ref[0, mod_base:mod_base + 1, :]
    scale = mod_ref[0, mod_base + 1:mod_base + 2, :]
    gate = mod_ref[0, mod_base + 2:mod_base + 3, :]
    hb = _rms_modulate(s, g_ref[...], shift, scale).astype(BF16)
    for j in range(d_ff // tf):
        gt = jnp.dot(hb, wgu_ref[:, j * tf:(j + 1) * tf], preferred_element_type=F32)
        up = jnp.dot(hb, wgu_ref[:, d_ff + j * tf:d_ff + (j + 1) * tf], preferred_element_type=F32)
        act = ((gt * jax.nn.sigmoid(gt)) * up).astype(BF16)
        part = jnp.dot(act, wdown_ref[j * tf:(j + 1) * tf, :], preferred_element_type=F32)
        if j == 0:
            acc_ref[...] = part
        else:
            acc_ref[...] += part
    out = s + (0.5 * gate) * acc_ref[...]
    if final_norm:
        out = out * lax.rsqrt(jnp.mean(out * out, axis=-1, keepdims=True) + NORM_EPS) * gfin_ref[...]
    o_ref[0] = out


def _ffn(s, mod, mod_base, g, wgu, wdown, y=None, wout=None, gfin=None, tm=512):
    bv, tv, d = s.shape
    d_ff = wdown.shape[0]
    tf = 256
    has_pre = y is not None
    final_norm = gfin is not None
    const = lambda b, i: (0, 0)
    tile = lambda b, i: (b, i, 0)
    in_specs = [pl.BlockSpec((1, tm, d), tile)]
    args = [s]
    if has_pre:
        in_specs += [pl.BlockSpec((1, tm, y.shape[2]), tile), pl.BlockSpec(wout.shape, const)]
        args += [y, wout]
    in_specs += [
        pl.BlockSpec((1, mod.shape[1], d), lambda b, i: (b, 0, 0)),
        pl.BlockSpec((1, d), const),
        pl.BlockSpec(wgu.shape, const),
        pl.BlockSpec(wdown.shape, const),
    ]
    args += [mod, g.reshape(1, d), wgu, wdown]
    if final_norm:
        in_specs.append(pl.BlockSpec((1, d), const))
        args.append(gfin.reshape(1, d))
    body = functools.partial(_ffn_kernel, mod_base=mod_base, has_pre=has_pre,
                             final_norm=final_norm, d_ff=d_ff, tf=tf)
    return pl.pallas_call(
        body,
        grid=(bv, tv // tm),
        in_specs=in_specs,
        out_specs=pl.BlockSpec((1, tm, d), tile),
        out_shape=jax.ShapeDtypeStruct(s.shape, F32),
        scratch_shapes=[pltpu.VMEM((tm, d), F32)],
        compiler_params=pltpu.CompilerParams(
            dimension_semantics=("parallel", "parallel"), vmem_limit_bytes=VMEM_LIMIT),
        name="ffn",
    )(*args)


def _inproj_kernel(s_ref, mod_ref, g_ref, w_ref, o_ref, *, mod_base):
    shift = mod_ref[0, mod_base:mod_base + 1, :]
    scale = mod_ref[0, mod_base + 1:mod_base + 2, :]
    hb = _rms_modulate(s_ref[0], g_ref[...], shift, scale).astype(BF16)
    o_ref[0] = jnp.dot(hb, w_ref[...], preferred_element_type=F32)


def _inproj(s, mod, mod_base, g, w, tm=512):
    bv, tv, d = s.shape
    n = w.shape[1]
    const = lambda b, i: (0, 0)
    return pl.pallas_call(
        functools.partial(_inproj_kernel, mod_base=mod_base),
        grid=(bv, tv // tm),
        in_specs=[
            pl.BlockSpec((1, tm, d), lambda b, i: (b, i, 0)),
            pl.BlockSpec((1, mod.shape[1], d), lambda b, i: (b, 0, 0)),
            pl.BlockSpec((1, d), const),
            pl.BlockSpec(w.shape, const),
        ],
        out_specs=pl.BlockSpec((1, tm, n), lambda b, i: (b, i, 0)),
        out_shape=jax.ShapeDtypeStruct((bv, tv, n), F32),
        compiler_params=pltpu.CompilerParams(
            dimension_semantics=("parallel", "parallel"), vmem_limit_bytes=VMEM_LIMIT),
        name="inproj",
    )(s, mod, g.reshape(1, d), w)


CONV_W = 256
RW = 384
OFF_B = 3 * CONV_W
RWKV_COLS = 3 * RW + 64 + 64 + 128
OFF_C = OFF_B + RWKV_COLS
P_COLS = OFF_C + 2 * RW
F_COLS = 3 * RW
V_W0, V_A0, V_KK, V_KA, V_RK, V_LNG, V_LNB, V_CW, V_CB, V_BR, V_BI, V_LAM = 0, 1, 2, 3, 4, 5, 6, 7, 11, 12, 13, 14
VEC_ROWS = 16


def _mixer_kernel(*refs, d, n_chunks, line_is_chunk):
    it = iter(refs)
    p_ref, pp_ref, pn_ref = next(it), next(it), next(it)
    if d == 1:
        f_in_ref = next(it)
    s0_ref, h0_ref = next(it), next(it)
    vec_ref, mu_ref, conva_ref = next(it), next(it), next(it)
    w2_ref, a2_ref, g2_ref, wr_ref, wi_ref = next(it), next(it), next(it), next(it), next(it)
    out_ref, s_ref, h_ref = next(it), next(it), next(it)

    reverse = d == 1
    i = pl.program_id(1)
    ci = (n_chunks - 1 - i) if reverse else i
    keep_prev = jnp.where(ci == 0, 0.0, 1.0)
    keep_next = jnp.where(ci == n_chunks - 1, 0.0, 1.0)

    @pl.when(i == 0)
    def _():
        s_ref[...] = s0_ref[...]
        h_ref[...] = h0_ref[...]

    def vec(r, sl=slice(None)):
        return vec_ref[r:r + 1, sl]

    nb = p_ref.shape[0]
    grp = [slice(b * CHUNK, (b + 1) * CHUNK) for b in range(nb)]

    def stacked(ref, c0, c1):
        return jnp.concatenate([ref[b, :, c0:c1] for b in range(nb)], axis=0)

    cur = stacked(p_ref, OFF_B, OFF_C)
    prev = stacked(pp_ref, OFF_B, OFF_C) * keep_prev
    nxt = stacked(pn_ref, OFF_B, OFF_C) * keep_next
    pb = (cur + mu_ref[0:1, :] * (_shift_dn(cur, prev, 1) - cur)
          + mu_ref[1:2, :] * (_shift_up(cur, nxt, 1) - cur))
    r = pb[:, 0:RW]
    k = pb[:, RW:2 * RW]
    v = pb[:, 2 * RW:3 * RW]
    dw = pb[:, 3 * RW:3 * RW + 64]
    da = pb[:, 3 * RW + 64:3 * RW + 128]
    dg = pb[:, 3 * RW + 128:3 * RW + 256]

    lane_p = lax.broadcasted_iota(jnp.int32, (PAIR, PAIR), 1)
    row_p = lax.broadcasted_iota(jnp.int32, (PAIR, PAIR), 0)
    same_head = (lane_p >= HEAD) == (row_p >= HEAD)
    head_ones = jnp.where(same_head, 1.0, 0.0).astype(BF16)

    def head_sum(x):
        return jnp.concatenate(
            [_dot_ones_rhs(x[:, PAIR * hp:PAIR * (hp + 1)], head_ones) for hp in range(RW // PAIR)],
            axis=1)

    kkr = k * vec(V_KK)
    kk = kkr * lax.rsqrt(head_sum(kkr * kkr) + 1e-12)
    lw = -DECAY_SCALE * jax.nn.sigmoid(vec(V_W0) + _dot(jnp.tanh(dw), w2_ref[...]))
    a = jax.nn.sigmoid(vec(V_A0) + _dot(da, a2_ref[...]))
    kd = k * (1.0 + (a - 1.0) * vec(V_KA))
    beta = kk * a
    bonus = head_sum(r * kd * vec(V_RK)) * v

    row_c = lax.broadcasted_iota(jnp.int32, (nb * CHUNK, nb * CHUNK), 0)
    col_c = lax.broadcasted_iota(jnp.int32, (nb * CHUNK, nb * CHUNK), 1)
    incl_c = (col_c >= row_c) if reverse else (col_c <= row_c)
    shift = CHUNK.bit_length() - 1
    incl_c = incl_c & ((row_c >> shift) == (col_c >> shift))
    c_inc = _dot_exact_lhs(jnp.where(incl_c, 1.0, 0.0).astype(BF16), lw)
    c_exc = c_inc - lw
    c_tot = [c_inc[g.start:g.start + 1, :] if reverse else c_inc[g.stop - 1:g.stop, :] for g in grp]
    c_mid = [c_inc[g.start + CHUNK // 2:g.start + CHUNK // 2 + 1, :] for g in grp]
    e_exc = jnp.exp(c_exc)
    e_inc = jnp.exp(c_inc)
    e_rem = jnp.exp(_rows_to_groups(c_tot) - c_inc)
    e_neg_mid = _rows_to_groups([jnp.exp(-m) for m in c_mid])
    e_mid_tot = _rows_to_groups([jnp.exp(m - t) for m, t in zip(c_mid, c_tot)])
    gam = [jnp.exp(t) for t in c_tot]
    at_abs = -(kk * e_exc)
    rt_abs = r * e_inc
    bg = beta * e_rem
    kg = kd * e_rem
    at_off = at_abs * e_neg_mid
    rt_off = rt_abs * e_neg_mid
    bh_off = bg * e_mid_tot
    kh_off = kg * e_mid_tot

    row_h = lax.broadcasted_iota(jnp.int32, (CHUNK, PAIR), 0)
    col_h = lax.broadcasted_iota(jnp.int32, (CHUNK, PAIR), 1) & (HEAD - 1)
    strict = (col_h > row_h) if reverse else (col_h < row_h)
    incl = (col_h >= row_h) if reverse else (col_h <= row_h)
    eye_h = jnp.where(col_h == row_h, 1.0, 0.0)
    eye_p = lane_p == row_p

    if reverse:
        g = _dot(jax.nn.sigmoid(dg), g2_ref[...])
        f_in = stacked(f_in_ref, 0, F_COLS)

    n_pair = RW // PAIR
    units = [(b, hp) for b in range(nb) for hp in range(n_pair)]
    pos = [(grp[b], slice(PAIR * hp, PAIR * (hp + 1))) for b, hp in units]

    def blk(sp, gs):
        return tuple(x[gs[0], gs[1]] for x in sp)

    s_sp = [_split2(s_ref[b, hp]) for b, hp in units]
    at_off_sp, rt_off_sp = _split2(at_off), _split2(rt_off)
    bh_off_sp, kh_off_sp = _split2(bh_off), _split2(kh_off)
    at_abs_sp, rt_abs_sp = _split2(at_abs), _split2(rt_abs)
    bg_sp, kg_sp, v_sp = _split2(bg), _split2(kg), _split2(v)
    gm = [_mm3(_cat([blk(at_off_sp, gs), blk(rt_off_sp, gs)], 0),
               _cat([_bd(blk(bh_off_sp, gs)), _bd(blk(kh_off_sp, gs))], 0), _NT) for gs in pos]
    a_m = [jnp.where(strict, m[0:CHUNK, 0:PAIR], 0.0) for m in gm]
    b_m = [jnp.where(strict, m[0:CHUNK, PAIR:2 * PAIR], 0.0) for m in gm]
    p_m = [jnp.where(incl, m[CHUNK:2 * CHUNK, 0:PAIR], 0.0) for m in gm]
    q_m = [jnp.where(incl, m[CHUNK:2 * CHUNK, PAIR:2 * PAIR], 0.0) for m in gm]
    z = [_mm3(_cat([blk(at_abs_sp, gs), blk(rt_abs_sp, gs)], 0), s_sp[u]) for u, gs in enumerate(pos)]
    w_m = [z[u][0:CHUNK] + _mm3(_split2(b_m[u]), _bd(blk(v_sp, gs))) for u, gs in enumerate(pos)]

    xr = stacked(p_ref, OFF_C, OFF_C + RW)
    if not reverse:
        xh = stacked(pp_ref, OFF_C, OFF_C + RW) * keep_prev
        u_l = (vec(V_CW + 3) * xr + vec(V_CW + 2) * _shift_dn(xr, xh, 1)
               + vec(V_CW + 1) * _shift_dn(xr, xh, 2) + vec(V_CW) * _shift_dn(xr, xh, 3))
    else:
        xh = stacked(pn_ref, OFF_C, OFF_C + RW) * keep_next
        u_l = (vec(V_CW) * xr + vec(V_CW + 1) * _shift_up(xr, xh, 1)
               + vec(V_CW + 2) * _shift_up(xr, xh, 2) + vec(V_CW + 3) * _shift_up(xr, xh, 3))
    u_l = u_l + vec(V_CB)
    rg = jax.nn.sigmoid(_dot(u_l, wr_ref[...]) + vec(V_BR))
    ig = jax.nn.sigmoid(_dot(u_l, wi_ref[...]) + vec(V_BI))
    log_a = -LRU_C * (_softplus(-vec(V_LAM)) * rg)
    a_l = jnp.exp(log_a)
    th = jnp.tanh(log_a)
    b_l = jnp.sqrt(-2.0 * th / (1.0 - th)) * (ig * u_l)

    t_m = [eye_h + m for m in a_m]
    pw_sp = [_split2(m) for m in a_m]
    step = 1
    for _ in range(5):
        pw_sp = [_split2(_mm3(ps, _bd(ps))) for ps in pw_sp]
        t_m = [t + _mm3(_split2(t), _bd(ps)) for t, ps in zip(t_m, pw_sp)]
        b_l = b_l + a_l * _scan_shift(b_l, step, 0.0, reverse)
        a_l = a_l * _scan_shift(a_l, step, 1.0, reverse)
        step *= 2
    u_sp = [_split2(_mm3(_split2(t), _bd(_split2(w)))) for t, w in zip(t_m, w_m)]
    y_rw = [z[u][CHUNK:2 * CHUNK] + _mm3(
        _cat([_split2(p_m[u]), _split2(q_m[u])], 1),
        _cat([_bd(u_sp[u]), _bd(blk(v_sp, gs))], 0)) for u, gs in enumerate(pos)]
    s_upd = []
    for u, (gs, (b, hp)) in enumerate(zip(pos, units)):
        dgam = jnp.where(eye_p, jnp.broadcast_to(gam[b][:, gs[1]], (PAIR, PAIR)), 0.0)
        s_new = _mm3(_cat([blk(bg_sp, gs), blk(kg_sp, gs), _split2(dgam)], 0),
                     _cat([u_sp[u], blk(v_sp, gs), s_sp[u]], 0), _TN)
        s_upd.append(jnp.where(same_head, s_new, 0.0))

    if not reverse:
        y_out = y_rw
    else:
        y_out = []
        for u, ((gr_, sl), (b, hp)) in enumerate(zip(pos, units)):
            ysum = y_rw[u] + f_in[gr_, sl]
            mean = _dot_ones_rhs(ysum, head_ones) * (1.0 / HEAD)
            cen = ysum - mean
            var = _dot_ones_rhs(cen * cen, head_ones) * (1.0 / HEAD)
            gn = cen * lax.rsqrt(var + GN_EPS) * vec(V_LNG, sl) + vec(V_LNB, sl)
            y_out.append((gn + (f_in[gr_, RW + PAIR * hp:RW + PAIR * (hp + 1)] + bonus[gr_, sl])) * g[gr_, sl])

    y_base = CONV_W if reverse else 0
    for u, (b, hp) in enumerate(units):
        s_ref[b, hp] = s_upd[u]
        out_ref[b, :, y_base + PAIR * hp:y_base + PAIR * (hp + 1)] = y_out[u]
    if not reverse:
        for b in range(nb):
            out_ref[b, :, RW:2 * RW] = bonus[grp[b]]

    while step < CHUNK:
        b_l = b_l + a_l * _scan_shift(b_l, step, 0.0, reverse)
        a_l = a_l * _scan_shift(a_l, step, 1.0, reverse)
        step *= 2
    h = b_l + a_l * _rows_to_groups([h_ref[b] for b in range(nb)])
    for b in range(nb):
        hb = h[grp[b]]
        h_ref[b] = hb[0:1, :] if reverse else hb[CHUNK - 1:CHUNK, :]

    if not reverse:
        for b in range(nb):
            out_ref[b, :, 2 * RW:3 * RW] = h[grp[b]]
    else:
        gr = stacked(p_ref, OFF_C + RW, OFF_C + 2 * RW)
        y_lru = jax.nn.gelu(gr) * (f_in[:, 2 * RW:3 * RW] + h)
        bgate = stacked(p_ref, 0, CONV_W)
        uc = stacked(p_ref, CONV_W, 2 * CONV_W) * stacked(p_ref, 2 * CONV_W, 3 * CONV_W)
        if line_is_chunk:
            zero = jnp.zeros((nb * HALO, CONV_W), F32)
            up, un = zero, zero
        else:
            up = stacked(pp_ref, CONV_W, 2 * CONV_W) * stacked(pp_ref, 2 * CONV_W, 3 * CONV_W) * keep_prev
            un = stacked(pn_ref, CONV_W, 2 * CONV_W) * stacked(pn_ref, 2 * CONV_W, 3 * CONV_W) * keep_next
        yc = bgate * (conva_ref[0:1, :] * _shift_dn(uc, up, 1) + conva_ref[1:2, :] * uc
                      + conva_ref[2:3, :] * _shift_up(uc, un, 1))
        for b in range(nb):
            out_ref[b, :, CONV_W + RW:CONV_W + 2 * RW] = y_lru[grp[b]]
            out_ref[b, :, 0:CONV_W] = yc[grp[b]]


def _mixer_pass(p, f_in, s0, h0, prm, d, col_major, line_is_chunk):
    bsz, t_len, _ = p.shape
    n = t_len // CHUNK
    reverse = d == 1
    out_cols = 4 * CONV_W if reverse else F_COLS

    def ci(i):
        return (n - 1 - i) if reverse else i

    if col_major:
        assert n == GRID_W and t_len == GRID_W * CHUNK
        p_view = p.reshape(bsz, CHUNK, n * P_COLS)
        cur_spec = pl.BlockSpec((MIX_NB, CHUNK,P_COLS), lambda b, i: (b, 0, ci(i)))
        prev_spec = pl.BlockSpec((MIX_NB, HALO,P_COLS),
                                 lambda b, i: (b, CHUNK // HALO - 1, jnp.maximum(ci(i) - 1, 0)))
        next_spec = pl.BlockSpec((MIX_NB, HALO,P_COLS), lambda b, i: (b, 0, jnp.minimum(ci(i) + 1, n - 1)))
    else:
        p_view = p
        per = CHUNK // HALO
        cur_spec = pl.BlockSpec((MIX_NB, CHUNK,P_COLS), lambda b, i: (b, ci(i), 0))
        prev_spec = pl.BlockSpec((MIX_NB, HALO,P_COLS), lambda b, i: (b, jnp.maximum(per * ci(i) - 1, 0), 0))
        next_spec = pl.BlockSpec((MIX_NB, HALO,P_COLS),
                                 lambda b, i: (b, jnp.minimum(per * ci(i) + per, per * n - 1), 0))
    full = lambda a: pl.BlockSpec(a.shape, lambda b, i: (0,) * a.ndim)
    in_specs = [cur_spec, prev_spec, next_spec]
    args = [p_view, p_view, p_view]
    if reverse:
        in_specs.append(pl.BlockSpec((MIX_NB, CHUNK,F_COLS), lambda b, i: (b, ci(i), 0)))
        args.append(f_in)
    state_specs = [pl.BlockSpec((MIX_NB,) + s0.shape[1:], lambda b, i: (b, 0, 0, 0)),
                   pl.BlockSpec((MIX_NB, 1, RW), lambda b, i: (b, 0, 0))]
    in_specs += state_specs
    args += [s0, h0]
    params = [prm["vec"][d], prm["mu"], prm["conv_a"], prm["w2"][d], prm["a2"][d], prm["g2"],
              prm["wr"][d], prm["wi"][d]]
    in_specs += [full(a) for a in params]
    args += params

    if reverse and col_major:
        out_shape = jax.ShapeDtypeStruct((bsz, CHUNK, n * out_cols), F32)
        out_spec = pl.BlockSpec((MIX_NB, CHUNK,out_cols), lambda b, i: (b, 0, ci(i)))
    else:
        out_shape = jax.ShapeDtypeStruct((bsz, t_len, out_cols), F32)
        out_spec = pl.BlockSpec((MIX_NB, CHUNK,out_cols), lambda b, i: (b, ci(i), 0))

    out, s_fin, h_fin = pl.pallas_call(
        functools.partial(_mixer_kernel, d=d, n_chunks=n, line_is_chunk=line_is_chunk),
        grid=(bsz // MIX_NB, n),
        in_specs=in_specs,
        out_specs=[out_spec] + state_specs,
        out_shape=[out_shape, jax.ShapeDtypeStruct(s0.shape, F32), jax.ShapeDtypeStruct(h0.shape, F32)],
        compiler_params=pltpu.CompilerParams(
            dimension_semantics=("parallel", "arbitrary"), vmem_limit_bytes=VMEM_LIMIT),
        name="mixer_bwd" if reverse else "mixer_fwd",
    )(*args)
    return out.reshape(bsz, t_len, out_cols), s_fin, h_fin


def _mix_stream(p, prm, col_major, line_is_chunk, init):
    (s0f, h0f), (s0b, h0b) = init
    f, sf, hf = _mixer_pass(p, None, s0f, h0f, prm, 0, col_major, line_is_chunk)
    y, sb, hb = _mixer_pass(p, f, s0b, h0b, prm, 1, col_major, line_is_chunk)
    return y, ((sf, hf), (sb, hb))


def _mixer_params(l, conv_a, rwkv_mu, rwkv_w0, rwkv_w2, rwkv_a0, rwkv_a2, rwkv_g2, rwkv_kk, rwkv_ka,
                  rwkv_rk, rwkv_lnx_g, rwkv_lnx_b, lru_conv_w, lru_conv_b, lru_w_r, lru_b_r, lru_w_i,
                  lru_b_i, lru_lam):
    def both(a):
        return jnp.broadcast_to(a[None], (2,) + a.shape)

    rows = [rwkv_w0[l], rwkv_a0[l], both(rwkv_kk[l]), both(rwkv_ka[l]), both(rwkv_rk[l]),
            both(rwkv_lnx_g[l]), both(rwkv_lnx_b[l]),
            lru_conv_w[l][:, 0], lru_conv_w[l][:, 1], lru_conv_w[l][:, 2], lru_conv_w[l][:, 3],
            lru_conv_b[l], lru_b_r[l], lru_b_i[l], lru_lam[l], jnp.zeros((2, RW), F32)]
    assert len(rows) == VEC_ROWS
    vec = jnp.stack(rows, axis=1)
    nb = lru_w_r.shape[2]
    eye = jnp.eye(nb, dtype=F32)

    def bd(w):
        return jnp.einsum("dnij,nm->dnimj", w, eye).reshape(2, nb * HEAD, nb * HEAD).astype(BF16)

    return {
        "vec": vec, "mu": rwkv_mu[l], "conv_a": conv_a[l],
        "w2": rwkv_w2[l].astype(BF16), "a2": rwkv_a2[l].astype(BF16), "g2": rwkv_g2[l].astype(BF16),
        "wr": bd(lru_w_r[l]), "wi": bd(lru_w_i[l]),
    }


def kernel(x, c, ctx, c_ctx, w_mod, b_mod, g_ffn1, w_gu1, w_down1, g_mix, w_in, conv_a, rwkv_mu, rwkv_w0, rwkv_w2, rwkv_a0, rwkv_a2, rwkv_g2, rwkv_kk, rwkv_ka, rwkv_rk, rwkv_lnx_g, rwkv_lnx_b, lru_conv_w, lru_conv_b, lru_w_r, lru_b_r, lru_w_i, lru_b_i, lru_lam, w_out, g_ffn2, w_gu2, w_down2, g_final):
    bsz, t_len, d = x.shape
    depth = w_mod.shape[0]
    n_ctx = ctx.shape[1]
    assert w_in.shape[2] == P_COLS and t_len % CHUNK == 0 and n_ctx % CHUNK == 0

    mod_rows = -(-(bsz + 1) // 8) * 8
    cc = jnp.concatenate([c, c_ctx[None, :], jnp.zeros((mod_rows - bsz - 1, d), F32)], axis=0)
    mods = _modulation(cc, w_mod, b_mod)

    zero_state = ((jnp.zeros((bsz, RW // PAIR, PAIR, PAIR), F32), jnp.zeros((bsz, 1, RW), F32)),) * 2
    s_lat = x
    s_ctx = ctx.reshape(1, bsz * n_ctx, d)
    for l in range(depth):
        last = l == depth - 1
        m_lat = mods[l, :bsz].reshape(bsz, 9, d)
        m_ctx = mods[l, bsz:bsz + 1].reshape(1, 9, d)
        wgu1, wd1 = w_gu1[l].astype(BF16), w_down1[l].astype(BF16)
        wgu2, wd2 = w_gu2[l].astype(BF16), w_down2[l].astype(BF16)
        win, wout = w_in[l].astype(BF16), w_out[l].astype(BF16)
        prm = _mixer_params(l, conv_a, rwkv_mu, rwkv_w0, rwkv_w2, rwkv_a0, rwkv_a2, rwkv_g2, rwkv_kk,
                            rwkv_ka, rwkv_rk, rwkv_lnx_g, rwkv_lnx_b, lru_conv_w, lru_conv_b, lru_w_r,
                            lru_b_r, lru_w_i, lru_b_i, lru_lam)

        s_lat = _ffn(s_lat, m_lat, 0, g_ffn1[l], wgu1, wd1)
        s_ctx = _ffn(s_ctx, m_ctx, 0, g_ffn1[l], wgu1, wd1)
        p_lat = _inproj(s_lat, m_lat, 3, g_mix[l], win)
        p_ctx = _inproj(s_ctx, m_ctx, 3, g_mix[l], win)

        y_ctx, ctx_fin = _mix_stream(p_ctx.reshape(bsz, n_ctx, P_COLS), prm, False, False, zero_state)
        y_lat, _ = _mix_stream(p_lat, prm, l % 2 == 1, True, ctx_fin)

        s_lat = _ffn(s_lat, m_lat, 6, g_ffn2[l], wgu2, wd2, y=y_lat, wout=wout,
                     gfin=g_final if last else None)
        if not last:
            s_ctx = _ffn(s_ctx, m_ctx, 6, g_ffn2[l], wgu2, wd2,
                         y=y_ctx.reshape(1, bsz * n_ctx, 4 * CONV_W), wout=wout)
    return s_lat
```

```python
import functools

import jax
import jax.numpy as jnp
from jax import lax
from jax.experimental import pallas as pl
from jax.experimental.pallas import tpu as pltpu

F32 = jnp.float32
BF16 = jnp.bfloat16

HEAD = 64
PAIR = 2 * HEAD
CHUNK = 64
HALO = 8
MIX_NB = 4
GRID_W = 64
NORM_EPS = 1e-6
GN_EPS = 64e-5
LRU_C = 8.0
DECAY_SCALE = 0.6065306597126334
VMEM_LIMIT = 56 * 1024 * 1024


def _dot(a, b):
    return jnp.dot(a.astype(BF16), b.astype(BF16), preferred_element_type=F32)


def _split2(a):
    hi = a.astype(BF16)
    lo = (a - hi.astype(F32)).astype(BF16)
    return hi, lo


_NN = (((1,), (0,)), ((), ()))
_NT = (((1,), (1,)), ((), ()))
_TN = (((0,), (0,)), ((), ()))


def _dot3(a, b, dims=_NN):
    ah, al = _split2(a)
    bh, bl = _split2(b)
    d = functools.partial(lax.dot_general, dimension_numbers=dims, preferred_element_type=F32)
    return d(ah, bh) + (d(ah, bl) + d(al, bh))


def _mm3(a, b, dims=_NN):
    ah, al = a
    bh, bl = b
    d = functools.partial(lax.dot_general, dimension_numbers=dims, preferred_element_type=F32)
    free = 1 if dims == _TN else 0
    m = ah.shape[free]
    both = d(jnp.concatenate([ah, al], axis=free), bh)
    return both[:m] + (d(ah, bl) + both[m:])


def _cat(parts, axis):
    return tuple(jnp.concatenate([p[i] for p in parts], axis=axis) for i in range(2))


def _lanes(sp, sl):
    return tuple(x[:, sl] for x in sp)


def _bd(sp):
    return tuple(_blockdiag(x) for x in sp)


def _dot_ones_rhs(a, ones_rhs):
    ah, al = _split2(a)
    d = functools.partial(jnp.dot, preferred_element_type=F32)
    return d(ah, ones_rhs) + d(al, ones_rhs)


def _dot_exact_lhs(ones_lhs, b):
    b1 = b.astype(BF16)
    r1 = b - b1.astype(F32)
    b2 = r1.astype(BF16)
    b3 = (r1 - b2.astype(F32)).astype(BF16)
    d = functools.partial(jnp.dot, preferred_element_type=F32)
    return d(ones_lhs, b1) + (d(ones_lhs, b2) + d(ones_lhs, b3))


def _softplus(x):
    return jnp.maximum(x, 0.0) + jnp.log1p(jnp.exp(-jnp.abs(x)))


def _rms_modulate(s, g, shift, scale):
    y = s * lax.rsqrt(jnp.mean(s * s, axis=-1, keepdims=True) + NORM_EPS)
    return (y * g) * (1.0 + scale) + shift


def _per_group(fn, x, halo, *args):
    nb = x.shape[0] // CHUNK
    if nb == 1:
        return fn(x, halo, *args)
    return jnp.concatenate(
        [fn(x[b * CHUNK:(b + 1) * CHUNK], None if halo is None else halo[b * HALO:(b + 1) * HALO], *args)
         for b in range(nb)], axis=0)


def _shift_dn(x, prev, k):
    return _per_group(_shift_dn1, x, prev, k)


def _shift_up(x, nxt, k):
    return _per_group(_shift_up1, x, nxt, k)


def _scan_shift(x, k, fill, reverse):
    return _per_group(_scan_shift1, x, None, k, fill, reverse)


def _rows_to_groups(rows):
    return jnp.concatenate([jnp.broadcast_to(r, (CHUNK, r.shape[1])) for r in rows], axis=0)


def _shift_dn1(x, prev, k):
    ch = x.shape[1]
    rolled = pltpu.roll(x, k, axis=0)
    pr = pltpu.roll(prev, k, axis=0)
    r8 = lax.broadcasted_iota(jnp.int32, (HALO, ch), 0)
    head = jnp.where(r8 < k, pr, rolled[:HALO])
    return jnp.concatenate([head, rolled[HALO:]], axis=0)


def _shift_up1(x, nxt, k):
    n, ch = x.shape
    rolled = pltpu.roll(x, n - k, axis=0)
    nr = pltpu.roll(nxt, HALO - k, axis=0)
    r8 = lax.broadcasted_iota(jnp.int32, (HALO, ch), 0)
    tail = jnp.where(r8 >= HALO - k, nr, rolled[n - HALO:])
    return jnp.concatenate([rolled[:n - HALO], tail], axis=0)


def _scan_shift1(x, _, k, fill, reverse):
    n, ch = x.shape
    if k % HALO == 0:
        pad = jnp.full((k, ch), fill, x.dtype)
        return jnp.concatenate([x[k:], pad] if reverse else [pad, x[:n - k]], axis=0)
    rows = lax.broadcasted_iota(jnp.int32, (n, ch), 0)
    if not reverse:
        return jnp.where(rows >= k, pltpu.roll(x, k, axis=0), fill)
    return jnp.where(rows < n - k, pltpu.roll(x, n - k, axis=0), fill)


def _blockdiag(xp):
    n = xp.shape[0]
    lane = lax.broadcasted_iota(jnp.int32, (n, PAIR), 1)
    zero = jnp.zeros_like(xp)
    top = jnp.where(lane < HEAD, xp, zero)
    bot = jnp.where(lane >= HEAD, xp, zero)
    return jnp.concatenate([top, bot], axis=0)


def _mod_kernel(c_ref, w_ref, b_ref, o_ref):
    cc = c_ref[...]
    act = cc * jax.nn.sigmoid(cc)
    o_ref[0] = _dot(act, w_ref[0]) + b_ref[0]


def _modulation(cc, w_mod, b_mod):
    depth, d, n = w_mod.shape
    rows = cc.shape[0]
    tn = n // 8
    return pl.pallas_call(
        _mod_kernel,
        grid=(depth, n // tn),
        in_specs=[
            pl.BlockSpec((rows, d), lambda l, j: (0, 0)),
            pl.BlockSpec((1, d, tn), lambda l, j: (l, 0, j)),
            pl.BlockSpec((1, 1, tn), lambda l, j: (l, 0, j)),
        ],
        out_specs=pl.BlockSpec((1, rows, tn), lambda l, j: (l, 0, j)),
        out_shape=jax.ShapeDtypeStruct((depth, rows, n), F32),
        compiler_params=pltpu.CompilerParams(
            dimension_semantics=("parallel", "parallel"), vmem_limit_bytes=VMEM_LIMIT),
        name="modulation",
    )(cc, w_mod, b_mod.reshape(depth, 1, n))


def _ffn_kernel(*refs, mod_base, has_pre, final_norm, d_ff, tf):
    it = iter(refs)
    s_ref = next(it)
    if has_pre:
        y_ref = next(it)
        wout_ref = next(it)
    mod_ref = next(it)
    g_ref = next(it)
    wgu_ref = next(it)
    wdown_ref = next(it)
    if final_norm:
        gfin_ref = next(it)
    o_ref = next(it)
    acc_ref = next(it)

    s = s_ref[0]
    if has_pre:
        gate_mix = mod_ref[0, mod_base - 1:mod_base, :]
        s = s + gate_mix * _dot(y_ref[0], wout_ref[...])
    shift = mod_ref[0, mod_base:mod_base + 1, :]
    scale = mod_ref[0, mod_base + 1:mod_base + 2, :]
    gate = mod_ref[0, mod_base + 2:mod_base + 3, :]
    hb = _rms_modulate(s, g_ref[...], shift, scale).astype(BF16)
    for j in range(d_ff // tf):
        gt = jnp.dot(hb, wgu_ref[:, j * tf:(j + 1) * tf], preferred_element_type=F32)
        up = jnp.dot(hb, wgu_ref[:, d_ff + j * tf:d_ff + (j + 1) * tf], preferred_element_type=F32)
        act = ((gt * jax.nn.sigmoid(gt)) * up).astype(BF16)
        part = jnp.dot(act, wdown_ref[j * tf:(j + 1) * tf, :], preferred_element_type=F32)
        if j == 0:
            acc_ref[...] = part
        else:
            acc_ref[...] += part
    out = s + (0.5 * gate) * acc_ref[...]
    if final_norm:
        out = out * lax.rsqrt(jnp.mean(out * out, axis=-1, keepdims=True) + NORM_EPS) * gfin_ref[...]
    o_ref[0] = out


def _ffn(s, mod, mod_base, g, wgu, wdown, y=None, wout=None, gfin=None, tm=512):
    bv, tv, d = s.shape
    d_ff = wdown.shape[0]
    tf = 256
    has_pre = y is not None
    final_norm = gfin is not None
    const = lambda b, i: (0, 0)
    tile = lambda b, i: (b, i, 0)
    in_specs = [pl.BlockSpec((1, tm, d), tile)]
    args = [s]
    if has_pre:
        in_specs += [pl.BlockSpec((1, tm, y.shape[2]), tile), pl.BlockSpec(wout.shape, const)]
        args += [y, wout]
    in_specs += [
        pl.BlockSpec((1, mod.shape[1], d), lambda b, i: (b, 0, 0)),
        pl.BlockSpec((1, d), const),
        pl.BlockSpec(wgu.shape, const),
        pl.BlockSpec(wdown.shape, const),
    ]
    args += [mod, g.reshape(1, d), wgu, wdown]
    if final_norm:
        in_specs.append(pl.BlockSpec((1, d), const))
        args.append(gfin.reshape(1, d))
    body = functools.partial(_ffn_kernel, mod_base=mod_base, has_pre=has_pre,
                             final_norm=final_norm, d_ff=d_ff, tf=tf)
    return pl.pallas_call(
        body,
        grid=(bv, tv // tm),
        in_specs=in_specs,
        out_specs=pl.BlockSpec((1, tm, d), tile),
        out_shape=jax.ShapeDtypeStruct(s.shape, F32),
        scratch_shapes=[pltpu.VMEM((tm, d), F32)],
        compiler_params=pltpu.CompilerParams(
            dimension_semantics=("parallel", "parallel"), vmem_limit_bytes=VMEM_LIMIT),
        name="ffn",
    )(*args)


def _inproj_kernel(s_ref, mod_ref, g_ref, w_ref, o_ref, *, mod_base):
    shift = mod_ref[0, mod_base:mod_base + 1, :]
    scale = mod_ref[0, mod_base + 1:mod_base + 2, :]
    hb = _rms_modulate(s_ref[0], g_ref[...], shift, scale).astype(BF16)
    o_ref[0] = jnp.dot(hb, w_ref[...], preferred_element_type=F32)


def _inproj(s, mod, mod_base, g, w, tm=512):
    bv, tv, d = s.shape
    n = w.shape[1]
    const = lambda b, i: (0, 0)
    return pl.pallas_call(
        functools.partial(_inproj_kernel, mod_base=mod_base),
        grid=(bv, tv // tm),
        in_specs=[
            pl.BlockSpec((1, tm, d), lambda b, i: (b, i, 0)),
            pl.BlockSpec((1, mod.shape[1], d), lambda b, i: (b, 0, 0)),
            pl.BlockSpec((1, d), const),
            pl.BlockSpec(w.shape, const),
        ],
        out_specs=pl.BlockSpec((1, tm, n), lambda b, i: (b, i, 0)),
        out_shape=jax.ShapeDtypeStruct((bv, tv, n), F32),
        compiler_params=pltpu.CompilerParams(
            dimension_semantics=("parallel", "parallel"), vmem_limit_bytes=VMEM_LIMIT),
        name="inproj",
    )(s, mod, g.reshape(1, d), w)


CONV_W = 256
RW = 384
OFF_B = 3 * CONV_W
RWKV_COLS = 3 * RW + 64 + 64 + 128
OFF_C = OFF_B + RWKV_COLS
P_COLS = OFF_C + 2 * RW
F_COLS = 3 * RW
V_W0, V_A0, V_KK, V_KA, V_RK, V_LNG, V_LNB, V_CW, V_CB, V_BR, V_BI, V_LAM = 0, 1, 2, 3, 4, 5, 6, 7, 11, 12, 13, 14
VEC_ROWS = 16


def _mixer_kernel(*refs, d, n_chunks, line_is_chunk):
    it = iter(refs)
    p_ref, pp_ref, pn_ref = next(it), next(it), next(it)
    if d == 1:
        f_in_ref = next(it)
    s0_ref, h0_ref = next(it), next(it)
    vec_ref, mu_ref, conva_ref = next(it), next(it), next(it)
    w2_ref, a2_ref, g2_ref, wr_ref, wi_ref = next(it), next(it), next(it), next(it), next(it)
    out_ref, s_ref, h_ref = next(it), next(it), next(it)

    reverse = d == 1
    i = pl.program_id(1)
    ci = (n_chunks - 1 - i) if reverse else i
    keep_prev = jnp.where(ci == 0, 0.0, 1.0)
    keep_next = jnp.where(ci == n_chunks - 1, 0.0, 1.0)

    @pl.when(i == 0)
    def _():
        s_ref[...] = s0_ref[...]
        h_ref[...] = h0_ref[...]

    def vec(r, sl=slice(None)):
        return vec_ref[r:r + 1, sl]

    nb = p_ref.shape[0]
    grp = [slice(b * CHUNK, (b + 1) * CHUNK) for b in range(nb)]

    def stacked(ref, c0, c1):
        return jnp.concatenate([ref[b, :, c0:c1] for b in range(nb)], axis=0)

    cur = stacked(p_ref, OFF_B, OFF_C)
    prev = stacked(pp_ref, OFF_B, OFF_C) * keep_prev
    nxt = stacked(pn_ref, OFF_B, OFF_C) * keep_next
    pb = (cur + mu_ref[0:1, :] * (_shift_dn(cur, prev, 1) - cur)
          + mu_ref[1:2, :] * (_shift_up(cur, nxt, 1) - cur))
    r = pb[:, 0:RW]
    k = pb[:, RW:2 * RW]
    v = pb[:, 2 * RW:3 * RW]
    dw = pb[:, 3 * RW:3 * RW + 64]
    da = pb[:, 3 * RW + 64:3 * RW + 128]
    dg = pb[:, 3 * RW + 128:3 * RW + 256]

    lane_p = lax.broadcasted_iota(jnp.int32, (PAIR, PAIR), 1)
    row_p = lax.broadcasted_iota(jnp.int32, (PAIR, PAIR), 0)
    same_head = (lane_p >= HEAD) == (row_p >= HEAD)
    head_ones = jnp.where(same_head, 1.0, 0.0).astype(BF16)

    def head_sum(x):
        return jnp.concatenate(
            [_dot_ones_rhs(x[:, PAIR * hp:PAIR * (hp + 1)], head_ones) for hp in range(RW // PAIR)],
            axis=1)

    kkr = k * vec(V_KK)
    kk = kkr * lax.rsqrt(head_sum(kkr * kkr) + 1e-12)
    lw = -DECAY_SCALE * jax.nn.sigmoid(vec(V_W0) + _dot(jnp.tanh(dw), w2_ref[...]))
    a = jax.nn.sigmoid(vec(V_A0) + _dot(da, a2_ref[...]))
    kd = k * (1.0 + (a - 1.0) * vec(V_KA))
    beta = kk * a
    bonus = head_sum(r * kd * vec(V_RK)) * v

    row_c = lax.broadcasted_iota(jnp.int32, (nb * CHUNK, nb * CHUNK), 0)
    col_c = lax.broadcasted_iota(jnp.int32, (nb * CHUNK, nb * CHUNK), 1)
    incl_c = (col_c >= row_c) if reverse else (col_c <= row_c)
    shift = CHUNK.bit_length() - 1
    incl_c = incl_c & ((row_c >> shift) == (col_c >> shift))
    c_inc = _dot_exact_lhs(jnp.where(incl_c, 1.0, 0.0).astype(BF16), lw)
    c_exc = c_inc - lw
    c_tot = [c_inc[g.start:g.start + 1, :] if reverse else c_inc[g.stop - 1:g.stop, :] for g in grp]
    c_mid = [c_inc[g.start + CHUNK // 2:g.start + CHUNK // 2 + 1, :] for g in grp]
    e_exc = jnp.exp(c_exc)
    e_inc = jnp.exp(c_inc)
    e_rem = jnp.exp(_rows_to_groups(c_tot) - c_inc)
    e_neg_mid = _rows_to_groups([jnp.exp(-m) for m in c_mid])
    e_mid_tot = _rows_to_groups([jnp.exp(m - t) for m, t in zip(c_mid, c_tot)])
    gam = [jnp.exp(t) for t in c_tot]
    at_abs = -(kk * e_exc)
    rt_abs = r * e_inc
    bg = beta * e_rem
    kg = kd * e_rem
    at_off = at_abs * e_neg_mid
    rt_off = rt_abs * e_neg_mid
    bh_off = bg * e_mid_tot
    kh_off = kg * e_mid_tot

    row_h = lax.broadcasted_iota(jnp.int32, (CHUNK, PAIR), 0)
    col_h = lax.broadcasted_iota(jnp.int32, (CHUNK, PAIR), 1) & (HEAD - 1)
    strict = (col_h > row_h) if reverse else (col_h < row_h)
    incl = (col_h >= row_h) if reverse else (col_h <= row_h)
    eye_h = jnp.where(col_h == row_h, 1.0, 0.0)
    eye_p = lane_p == row_p

    if reverse:
        g = _dot(jax.nn.sigmoid(dg), g2_ref[...])
        f_in = stacked(f_in_ref, 0, F_COLS)

    n_pair = RW // PAIR
    units = [(b, hp) for b in range(nb) for hp in range(n_pair)]
    pos = [(grp[b], slice(PAIR * hp, PAIR * (hp + 1))) for b, hp in units]

    def blk(sp, gs):
        return tuple(x[gs[0], gs[1]] for x in sp)

    s_sp = [_split2(s_ref[b, hp]) for b, hp in units]
    at_off_sp, rt_off_sp = _split2(at_off), _split2(rt_off)
    bh_off_sp, kh_off_sp = _split2(bh_off), _split2(kh_off)
    at_abs_sp, rt_abs_sp = _split2(at_abs), _split2(rt_abs)
    bg_sp, kg_sp, v_sp = _split2(bg), _split2(kg), _split2(v)
    gm = [_mm3(_cat([blk(at_off_sp, gs), blk(rt_off_sp, gs)], 0),
               _cat([_bd(blk(bh_off_sp, gs)), _bd(blk(kh_off_sp, gs))], 0), _NT) for gs in pos]
    a_m = [jnp.where(strict, m[0:CHUNK, 0:PAIR], 0.0) for m in gm]
    b_m = [jnp.where(strict, m[0:CHUNK, PAIR:2 * PAIR], 0.0) for m in gm]
    p_m = [jnp.where(incl, m[CHUNK:2 * CHUNK, 0:PAIR], 0.0) for m in gm]
    q_m = [jnp.where(incl, m[CHUNK:2 * CHUNK, PAIR:2 * PAIR], 0.0) for m in gm]
    z = [_mm3(_cat([blk(at_abs_sp, gs), blk(rt_abs_sp, gs)], 0), s_sp[u]) for u, gs in enumerate(pos)]
    w_m = [z[u][0:CHUNK] + _mm3(_split2(b_m[u]), _bd(blk(v_sp, gs))) for u, gs in enumerate(pos)]

    xr = stacked(p_ref, OFF_C, OFF_C + RW)
    if not reverse:
        xh = stacked(pp_ref, OFF_C, OFF_C + RW) * keep_prev
        u_l = (vec(V_CW + 3) * xr + vec(V_CW + 2) * _shift_dn(xr, xh, 1)
               + vec(V_CW + 1) * _shift_dn(xr, xh, 2) + vec(V_CW) * _shift_dn(xr, xh, 3))
    else:
        xh = stacked(pn_ref, OFF_C, OFF_C + RW) * keep_next
        u_l = (vec(V_CW) * xr + vec(V_CW + 1) * _shift_up(xr, xh, 1)
               + vec(V_CW + 2) * _shift_up(xr, xh, 2) + vec(V_CW + 3) * _shift_up(xr, xh, 3))
    u_l = u_l + vec(V_CB)
    rg = jax.nn.sigmoid(_dot(u_l, wr_ref[...]) + vec(V_BR))
    ig = jax.nn.sigmoid(_dot(u_l, wi_ref[...]) + vec(V_BI))
    log_a = -LRU_C * (_softplus(-vec(V_LAM)) * rg)
    a_l = jnp.exp(log_a)
    th = jnp.tanh(log_a)
    b_l = jnp.sqrt(-2.0 * th / (1.0 - th)) * (ig * u_l)

    t_m = [eye_h + m for m in a_m]
    pw_sp = [_split2(_mm3(ps, _bd(ps))) for ps in [_split2(m) for m in a_m]]
    step = 1
    for lvl in range(5):
        rhs = [_bd(ps) for ps in pw_sp]
        if lvl < 4:
            prod = [_mm3(_cat([ps, _split2(t)], 0), r) for ps, t, r in zip(pw_sp, t_m, rhs)]
            pw_sp = [_split2(p[0:CHUNK]) for p in prod]
            t_m = [t + p[CHUNK:2 * CHUNK] for t, p in zip(t_m, prod)]
        else:
            t_m = [t + _mm3(_split2(t), r) for t, r in zip(t_m, rhs)]
        b_l = b_l + a_l * _scan_shift(b_l, step, 0.0, reverse)
        a_l = a_l * _scan_shift(a_l, step, 1.0, reverse)
        step *= 2
    u_sp = [_split2(_mm3(_split2(t), _bd(_split2(w)))) for t, w in zip(t_m, w_m)]
    y_rw = [z[u][CHUNK:2 * CHUNK] + _mm3(
        _cat([_split2(p_m[u]), _split2(q_m[u])], 1),
        _cat([_bd(u_sp[u]), _bd(blk(v_sp, gs))], 0)) for u, gs in enumerate(pos)]
    s_upd = []
    for u, (gs, (b, hp)) in enumerate(zip(pos, units)):
        dgam = jnp.where(eye_p, jnp.broadcast_to(gam[b][:, gs[1]], (PAIR, PAIR)), 0.0)
        s_new = _mm3(_cat([blk(bg_sp, gs), blk(kg_sp, gs), _split2(dgam)], 0),
                     _cat([u_sp[u], blk(v_sp, gs), s_sp[u]], 0), _TN)
        s_upd.append(jnp.where(same_head, s_new, 0.0))

    if not reverse:
        y_out = y_rw
    else:
        y_out = []
        for u, ((gr_, sl), (b, hp)) in enumerate(zip(pos, units)):
            ysum = y_rw[u] + f_in[gr_, sl]
            mean = _dot_ones_rhs(ysum, head_ones) * (1.0 / HEAD)
            cen = ysum - mean
            var = _dot_ones_rhs(cen * cen, head_ones) * (1.0 / HEAD)
            gn = cen * lax.rsqrt(var + GN_EPS) * vec(V_LNG, sl) + vec(V_LNB, sl)
            y_out.append((gn + (f_in[gr_, RW + PAIR * hp:RW + PAIR * (hp + 1)] + bonus[gr_, sl])) * g[gr_, sl])

    y_base = CONV_W if reverse else 0
    for u, (b, hp) in enumerate(units):
        s_ref[b, hp] = s_upd[u]
        out_ref[b, :, y_base + PAIR * hp:y_base + PAIR * (hp + 1)] = y_out[u]
    if not reverse:
        for b in range(nb):
            out_ref[b, :, RW:2 * RW] = bonus[grp[b]]

    while step < CHUNK:
        b_l = b_l + a_l * _scan_shift(b_l, step, 0.0, reverse)
        a_l = a_l * _scan_shift(a_l, step, 1.0, reverse)
        step *= 2
    h = b_l + a_l * _rows_to_groups([h_ref[b] for b in range(nb)])
    for b in range(nb):
        hb = h[grp[b]]
        h_ref[b] = hb[0:1, :] if reverse else hb[CHUNK - 1:CHUNK, :]

    if not reverse:
        for b in range(nb):
            out_ref[b, :, 2 * RW:3 * RW] = h[grp[b]]
    else:
        gr = stacked(p_ref, OFF_C + RW, OFF_C + 2 * RW)
        y_lru = jax.nn.gelu(gr) * (f_in[:, 2 * RW:3 * RW] + h)
        bgate = stacked(p_ref, 0, CONV_W)
        uc = stacked(p_ref, CONV_W, 2 * CONV_W) * stacked(p_ref, 2 * CONV_W, 3 * CONV_W)
        if line_is_chunk:
            zero = jnp.zeros((nb * HALO, CONV_W), F32)
            up, un = zero, zero
        else:
            up = stacked(pp_ref, CONV_W, 2 * CONV_W) * stacked(pp_ref, 2 * CONV_W, 3 * CONV_W) * keep_prev
            un = stacked(pn_ref, CONV_W, 2 * CONV_W) * stacked(pn_ref, 2 * CONV_W, 3 * CONV_W) * keep_next
        yc = bgate * (conva_ref[0:1, :] * _shift_dn(uc, up, 1) + conva_ref[1:2, :] * uc
                      + conva_ref[2:3, :] * _shift_up(uc, un, 1))
        for b in range(nb):
            out_ref[b, :, CONV_W + RW:CONV_W + 2 * RW] = y_lru[grp[b]]
            out_ref[b, :, 0:CONV_W] = yc[grp[b]]


def _mixer_pass(p, f_in, s0, h0, prm, d, col_major, line_is_chunk):
    bsz, t_len, _ = p.shape
    n = t_len // CHUNK
    reverse = d == 1
    out_cols = 4 * CONV_W if reverse else F_COLS

    def ci(i):
        return (n - 1 - i) if reverse else i

    if col_major:
        assert n == GRID_W and t_len == GRID_W * CHUNK
        p_view = p.reshape(bsz, CHUNK, n * P_COLS)
        cur_spec = pl.BlockSpec((MIX_NB, CHUNK,P_COLS), lambda b, i: (b, 0, ci(i)))
        prev_spec = pl.BlockSpec((MIX_NB, HALO,P_COLS),
                                 lambda b, i: (b, CHUNK // HALO - 1, jnp.maximum(ci(i) - 1, 0)))
        next_spec = pl.BlockSpec((MIX_NB, HALO,P_COLS), lambda b, i: (b, 0, jnp.minimum(ci(i) + 1, n - 1)))
    else:
        p_view = p
        per = CHUNK // HALO
        cur_spec = pl.BlockSpec((MIX_NB, CHUNK,P_COLS), lambda b, i: (b, ci(i), 0))
        prev_spec = pl.BlockSpec((MIX_NB, HALO,P_COLS), lambda b, i: (b, jnp.maximum(per * ci(i) - 1, 0), 0))
        next_spec = pl.BlockSpec((MIX_NB, HALO,P_COLS),
                                 lambda b, i: (b, jnp.minimum(per * ci(i) + per, per * n - 1), 0))
    full = lambda a: pl.BlockSpec(a.shape, lambda b, i: (0,) * a.ndim)
    in_specs = [cur_spec, prev_spec, next_spec]
    args = [p_view, p_view, p_view]
    if reverse:
        in_specs.append(pl.BlockSpec((MIX_NB, CHUNK,F_COLS), lambda b, i: (b, ci(i), 0)))
        args.append(f_in)
    state_specs = [pl.BlockSpec((MIX_NB,) + s0.shape[1:], lambda b, i: (b, 0, 0, 0)),
                   pl.BlockSpec((MIX_NB, 1, RW), lambda b, i: (b, 0, 0))]
    in_specs += state_specs
    args += [s0, h0]
    params = [prm["vec"][d], prm["mu"], prm["conv_a"], prm["w2"][d], prm["a2"][d], prm["g2"],
              prm["wr"][d], prm["wi"][d]]
    in_specs += [full(a) for a in params]
    args += params

    if reverse and col_major:
        out_shape = jax.ShapeDtypeStruct((bsz, CHUNK, n * out_cols), F32)
        out_spec = pl.BlockSpec((MIX_NB, CHUNK,out_cols), lambda b, i: (b, 0, ci(i)))
    else:
        out_shape = jax.ShapeDtypeStruct((bsz, t_len, out_cols), F32)
        out_spec = pl.BlockSpec((MIX_NB, CHUNK,out_cols), lambda b, i: (b, ci(i), 0))

    out, s_fin, h_fin = pl.pallas_call(
        functools.partial(_mixer_kernel, d=d, n_chunks=n, line_is_chunk=line_is_chunk),
        grid=(bsz // MIX_NB, n),
        in_specs=in_specs,
        out_specs=[out_spec] + state_specs,
        out_shape=[out_shape, jax.ShapeDtypeStruct(s0.shape, F32), jax.ShapeDtypeStruct(h0.shape, F32)],
        compiler_params=pltpu.CompilerParams(
            dimension_semantics=("parallel", "arbitrary"), vmem_limit_bytes=VMEM_LIMIT),
        name="mixer_bwd" if reverse else "mixer_fwd",
    )(*args)
    return out.reshape(bsz, t_len, out_cols), s_fin, h_fin


def _mix_stream(p, prm, col_major, line_is_chunk, init):
    (s0f, h0f), (s0b, h0b) = init
    f, sf, hf = _mixer_pass(p, None, s0f, h0f, prm, 0, col_major, line_is_chunk)
    y, sb, hb = _mixer_pass(p, f, s0b, h0b, prm, 1, col_major, line_is_chunk)
    return y, ((sf, hf), (sb, hb))


def _mixer_params(l, conv_a, rwkv_mu, rwkv_w0, rwkv_w2, rwkv_a0, rwkv_a2, rwkv_g2, rwkv_kk, rwkv_ka,
                  rwkv_rk, rwkv_lnx_g, rwkv_lnx_b, lru_conv_w, lru_conv_b, lru_w_r, lru_b_r, lru_w_i,
                  lru_b_i, lru_lam):
    def both(a):
        return jnp.broadcast_to(a[None], (2,) + a.shape)

    rows = [rwkv_w0[l], rwkv_a0[l], both(rwkv_kk[l]), both(rwkv_ka[l]), both(rwkv_rk[l]),
            both(rwkv_lnx_g[l]), both(rwkv_lnx_b[l]),
            lru_conv_w[l][:, 0], lru_conv_w[l][:, 1], lru_conv_w[l][:, 2], lru_conv_w[l][:, 3],
            lru_conv_b[l], lru_b_r[l], lru_b_i[l], lru_lam[l], jnp.zeros((2, RW), F32)]
    assert len(rows) == VEC_ROWS
    vec = jnp.stack(rows, axis=1)
    nb = lru_w_r.shape[2]
    eye = jnp.eye(nb, dtype=F32)

    def bd(w):
        return jnp.einsum("dnij,nm->dnimj", w, eye).reshape(2, nb * HEAD, nb * HEAD).astype(BF16)

    return {
        "vec": vec, "mu": rwkv_mu[l], "conv_a": conv_a[l],
        "w2": rwkv_w2[l].astype(BF16), "a2": rwkv_a2[l].astype(BF16), "g2": rwkv_g2[l].astype(BF16),
        "wr": bd(lru_w_r[l]), "wi": bd(lru_w_i[l]),
    }


def kernel(x, c, ctx, c_ctx, w_mod, b_mod, g_ffn1, w_gu1, w_down1, g_mix, w_in, conv_a, rwkv_mu, rwkv_w0, rwkv_w2, rwkv_a0, rwkv_a2, rwkv_g2, rwkv_kk, rwkv_ka, rwkv_rk, rwkv_lnx_g, rwkv_lnx_b, lru_conv_w, lru_conv_b, lru_w_r, lru_b_r, lru_w_i, lru_b_i, lru_lam, w_out, g_ffn2, w_gu2, w_down2, g_final):
    bsz, t_len, d = x.shape
    depth = w_mod.shape[0]
    n_ctx = ctx.shape[1]
    assert w_in.shape[2] == P_COLS and t_len % CHUNK == 0 and n_ctx % CHUNK == 0

    mod_rows = -(-(bsz + 1) // 8) * 8
    cc = jnp.concatenate([c, c_ctx[None, :], jnp.zeros((mod_rows - bsz - 1, d), F32)], axis=0)
    mods = _modulation(cc, w_mod, b_mod)

    zero_state = ((jnp.zeros((bsz, RW // PAIR, PAIR, PAIR), F32), jnp.zeros((bsz, 1, RW), F32)),) * 2
    s_lat = x
    s_ctx = ctx.reshape(1, bsz * n_ctx, d)
    for l in range(depth):
        last = l == depth - 1
        m_lat = mods[l, :bsz].reshape(bsz, 9, d)
        m_ctx = mods[l, bsz:bsz + 1].reshape(1, 9, d)
        wgu1, wd1 = w_gu1[l].astype(BF16), w_down1[l].astype(BF16)
        wgu2, wd2 = w_gu2[l].astype(BF16), w_down2[l].astype(BF16)
        win, wout = w_in[l].astype(BF16), w_out[l].astype(BF16)
        prm = _mixer_params(l, conv_a, rwkv_mu, rwkv_w0, rwkv_w2, rwkv_a0, rwkv_a2, rwkv_g2, rwkv_kk,
                            rwkv_ka, rwkv_rk, rwkv_lnx_g, rwkv_lnx_b, lru_conv_w, lru_conv_b, lru_w_r,
                            lru_b_r, lru_w_i, lru_b_i, lru_lam)

        s_lat = _ffn(s_lat, m_lat, 0, g_ffn1[l], wgu1, wd1)
        s_ctx = _ffn(s_ctx, m_ctx, 0, g_ffn1[l], wgu1, wd1)
        p_lat = _inproj(s_lat, m_lat, 3, g_mix[l], win)
        p_ctx = _inproj(s_ctx, m_ctx, 3, g_mix[l], win)

        y_ctx, ctx_fin = _mix_stream(p_ctx.reshape(bsz, n_ctx, P_COLS), prm, False, False, zero_state)
        y_lat, _ = _mix_stream(p_lat, prm, l % 2 == 1, True, ctx_fin)

        s_lat = _ffn(s_lat, m_lat, 6, g_ffn2[l], wgu2, wd2, y=y_lat, wout=wout,
                     gfin=g_final if last else None)
        if not last:
            s_ctx = _ffn(s_ctx, m_ctx, 6, g_ffn2[l], wgu2, wd2,
                         y=y_ctx.reshape(1, bsz * n_ctx, 4 * CONV_W), wout=wout)
    return s_lat
```

```python
import functools

import jax
import jax.numpy as jnp
from jax import lax
from jax.experimental import pallas as pl
from jax.experimental.pallas import tpu as pltpu

F32 = jnp.float32
BF16 = jnp.bfloat16

HEAD = 64
LANES = 128
PAIR = 2 * HEAD
CHUNK = 64
HALO = 8
MIX_NB = 4
GRID_W = 64
NORM_EPS = 1e-6
GN_EPS = 64e-5
LRU_C = 8.0
DECAY_SCALE = 0.6065306597126334
VMEM_LIMIT = 56 * 1024 * 1024


def _dot(a, b):
    return jnp.dot(a.astype(BF16), b.astype(BF16), preferred_element_type=F32)


def _split2(a):
    hi = a.astype(BF16)
    lo = (a - hi.astype(F32)).astype(BF16)
    return hi, lo


_NN = (((1,), (0,)), ((), ()))
_NT = (((1,), (1,)), ((), ()))
_TN = (((0,), (0,)), ((), ()))


def _dot3(a, b, dims=_NN):
    ah, al = _split2(a)
    bh, bl = _split2(b)
    d = functools.partial(lax.dot_general, dimension_numbers=dims, preferred_element_type=F32)
    return d(ah, bh) + (d(ah, bl) + d(al, bh))


def _mm3(a, b, dims=_NN):
    ah, al = a
    bh, bl = b
    d = functools.partial(lax.dot_general, dimension_numbers=dims, preferred_element_type=F32)
    free = 1 if dims == _TN else 0
    m = ah.shape[free]
    both = d(jnp.concatenate([ah, al], axis=free), bh)
    return both[:m] + (d(ah, bl) + both[m:])


def _cat(parts, axis):
    return tuple(jnp.concatenate([p[i] for p in parts], axis=axis) for i in range(2))


def _lanes(sp, sl):
    return tuple(x[:, sl] for x in sp)


def _bd(sp):
    return tuple(_blockdiag(x) for x in sp)


def _dot_ones_rhs(a, ones_rhs):
    ah, al = _split2(a)
    d = functools.partial(jnp.dot, preferred_element_type=F32)
    return d(ah, ones_rhs) + d(al, ones_rhs)


def _dot_exact_lhs(ones_lhs, b):
    b1 = b.astype(BF16)
    r1 = b - b1.astype(F32)
    b2 = r1.astype(BF16)
    b3 = (r1 - b2.astype(F32)).astype(BF16)
    d = functools.partial(jnp.dot, preferred_element_type=F32)
    return d(ones_lhs, b1) + (d(ones_lhs, b2) + d(ones_lhs, b3))


def _softplus(x):
    return jnp.maximum(x, 0.0) + jnp.log1p(jnp.exp(-jnp.abs(x)))


def _rms_modulate(s, g, shift, scale):
    y = s * lax.rsqrt(jnp.mean(s * s, axis=-1, keepdims=True) + NORM_EPS)
    return (y * g) * (1.0 + scale) + shift


def _per_group(fn, x, halo, *args):
    nb = x.shape[0] // CHUNK
    if nb == 1:
        return fn(x, halo, *args)
    return jnp.concatenate(
        [fn(x[b * CHUNK:(b + 1) * CHUNK], None if halo is None else halo[b * HALO:(b + 1) * HALO], *args)
         for b in range(nb)], axis=0)


def _shift_dn(x, prev, k):
    return _per_group(_shift_dn1, x, prev, k)


def _shift_up(x, nxt, k):
    return _per_group(_shift_up1, x, nxt, k)


def _scan_shift(x, k, fill, reverse):
    return _per_group(_scan_shift1, x, None, k, fill, reverse)


def _rows_to_groups(rows):
    return jnp.concatenate([jnp.broadcast_to(r, (CHUNK, r.shape[1])) for r in rows], axis=0)


def _shift_dn1(x, prev, k):
    ch = x.shape[1]
    rolled = pltpu.roll(x, k, axis=0)
    pr = pltpu.roll(prev, k, axis=0)
    r8 = lax.broadcasted_iota(jnp.int32, (HALO, ch), 0)
    head = jnp.where(r8 < k, pr, rolled[:HALO])
    return jnp.concatenate([head, rolled[HALO:]], axis=0)


def _shift_up1(x, nxt, k):
    n, ch = x.shape
    rolled = pltpu.roll(x, n - k, axis=0)
    nr = pltpu.roll(nxt, HALO - k, axis=0)
    r8 = lax.broadcasted_iota(jnp.int32, (HALO, ch), 0)
    tail = jnp.where(r8 >= HALO - k, nr, rolled[n - HALO:])
    return jnp.concatenate([rolled[:n - HALO], tail], axis=0)


def _scan_shift1(x, _, k, fill, reverse):
    n, ch = x.shape
    if k % HALO == 0:
        pad = jnp.full((k, ch), fill, x.dtype)
        return jnp.concatenate([x[k:], pad] if reverse else [pad, x[:n - k]], axis=0)
    rows = lax.broadcasted_iota(jnp.int32, (n, ch), 0)
    if not reverse:
        return jnp.where(rows >= k, pltpu.roll(x, k, axis=0), fill)
    return jnp.where(rows < n - k, pltpu.roll(x, n - k, axis=0), fill)


def _blockdiag(xp):
    n = xp.shape[0]
    lane = lax.broadcasted_iota(jnp.int32, (n, PAIR), 1)
    zero = jnp.zeros_like(xp)
    top = jnp.where(lane < HEAD, xp, zero)
    bot = jnp.where(lane >= HEAD, xp, zero)
    return jnp.concatenate([top, bot], axis=0)


def _mod_kernel(c_ref, w_ref, b_ref, o_ref):
    cc = c_ref[...]
    act = cc * jax.nn.sigmoid(cc)
    o_ref[0] = _dot(act, w_ref[0]) + b_ref[0]


def _modulation(cc, w_mod, b_mod):
    depth, d, n = w_mod.shape
    rows = cc.shape[0]
    tn = n // 8
    return pl.pallas_call(
        _mod_kernel,
        grid=(depth, n // tn),
        in_specs=[
            pl.BlockSpec((rows, d), lambda l, j: (0, 0)),
            pl.BlockSpec((1, d, tn), lambda l, j: (l, 0, j)),
            pl.BlockSpec((1, 1, tn), lambda l, j: (l, 0, j)),
        ],
        out_specs=pl.BlockSpec((1, rows, tn), lambda l, j: (l, 0, j)),
        out_shape=jax.ShapeDtypeStruct((depth, rows, n), F32),
        compiler_params=pltpu.CompilerParams(
            dimension_semantics=("parallel", "parallel"), vmem_limit_bytes=VMEM_LIMIT),
        name="modulation",
    )(cc, w_mod, b_mod.reshape(depth, 1, n))


def _ffn_kernel(*refs, mod_base, has_pre, y_col_major, final_norm, d_ff, tf):
    it = iter(refs)
    s_ref = next(it)
    if has_pre:
        y_ref = next(it)
        wout_ref = next(it)
        if y_col_major:
            perm_ref = next(it)
    mod_ref = next(it)
    g_ref = next(it)
    wgu_ref = next(it)
    wdown_ref = next(it)
    if final_norm:
        gfin_ref = next(it)
    o_ref = next(it)
    acc_ref = next(it)

    s = s_ref[0]
    if has_pre:
        gate_mix = mod_ref[0, mod_base - 1:mod_base, :]
        if y_col_major:
            mix = wout_ref.shape[0]
            yc = jnp.concatenate([y_ref[0, :, c * mix:(c + 1) * mix] for c in range(GRID_W)], axis=0)
            yb = jnp.dot(perm_ref[...], yc.astype(BF16), preferred_element_type=F32).astype(BF16)
        else:
            yb = y_ref[0].astype(BF16)
        s = s + gate_mix * jnp.dot(yb, wout_ref[...], preferred_element_type=F32)
    shift = mod_ref[0, mod_base:mod_base + 1, :]
    scale = mod_ref[0, mod_base + 1:mod_base + 2, :]
    gate = mod_ref[0, mod_base + 2:mod_base + 3, :]
    hb = _rms_modulate(s, g_ref[...], shift, scale).astype(BF16)
    for j in range(d_ff // tf):
        gt = jnp.dot(hb, wgu_ref[:, j * tf:(j + 1) * tf], preferred_element_type=F32)
        up = jnp.dot(hb, wgu_ref[:, d_ff + j * tf:d_ff + (j + 1) * tf], preferred_element_type=F32)
        act = ((gt * jax.nn.sigmoid(gt)) * up).astype(BF16)
        part = jnp.dot(act, wdown_ref[j * tf:(j + 1) * tf, :], preferred_element_type=F32)
        if j == 0:
            acc_ref[...] = part
        else:
            acc_ref[...] += part
    out = s + (0.5 * gate) * acc_ref[...]
    if final_norm:
        out = out * lax.rsqrt(jnp.mean(out * out, axis=-1, keepdims=True) + NORM_EPS) * gfin_ref[...]
    o_ref[0] = out


def _ffn(s, mod, mod_base, g, wgu, wdown, y=None, wout=None, y_col_major=False, gfin=None, tm=512):
    bv, tv, d = s.shape
    d_ff = wdown.shape[0]
    tf = 256
    has_pre = y is not None
    final_norm = gfin is not None
    const = lambda b, i: (0, 0)
    tile = lambda b, i: (b, i, 0)
    in_specs = [pl.BlockSpec((1, tm, d), tile)]
    args = [s]
    if has_pre:
        y_rows = tm // GRID_W if y_col_major else tm
        in_specs += [pl.BlockSpec((1, y_rows, y.shape[2]), tile), pl.BlockSpec(wout.shape, const)]
        args += [y, wout]
        if y_col_major:
            src = (jnp.arange(tm) % GRID_W) * y_rows + jnp.arange(tm) // GRID_W
            in_specs.append(pl.BlockSpec((tm, tm), const))
            args.append(jax.nn.one_hot(src, tm, dtype=BF16))
    in_specs += [
        pl.BlockSpec((1, mod.shape[1], d), lambda b, i: (b, 0, 0)),
        pl.BlockSpec((1, d), const),
        pl.BlockSpec(wgu.shape, const),
        pl.BlockSpec(wdown.shape, const),
    ]
    args += [mod, g.reshape(1, d), wgu, wdown]
    if final_norm:
        in_specs.append(pl.BlockSpec((1, d), const))
        args.append(gfin.reshape(1, d))
    body = functools.partial(_ffn_kernel, mod_base=mod_base, has_pre=has_pre, y_col_major=y_col_major,
                             final_norm=final_norm, d_ff=d_ff, tf=tf)
    return pl.pallas_call(
        body,
        grid=(bv, tv // tm),
        in_specs=in_specs,
        out_specs=pl.BlockSpec((1, tm, d), tile),
        out_shape=jax.ShapeDtypeStruct(s.shape, F32),
        scratch_shapes=[pltpu.VMEM((tm, d), F32)],
        compiler_params=pltpu.CompilerParams(
            dimension_semantics=("parallel", "parallel"), vmem_limit_bytes=VMEM_LIMIT),
        name="ffn",
    )(*args)


def _inproj_kernel(s_ref, mod_ref, g_ref, w_ref, *rest, mod_base, col_major):
    shift = mod_ref[0, mod_base:mod_base + 1, :]
    scale = mod_ref[0, mod_base + 1:mod_base + 2, :]
    hb = _rms_modulate(s_ref[0], g_ref[...], shift, scale).astype(BF16)
    if not col_major:
        o_ref, = rest
        o_ref[0] = jnp.dot(hb, w_ref[...], preferred_element_type=F32)
        return
    perm_ref, o_ref = rest
    hp = jnp.dot(perm_ref[...], hb, preferred_element_type=F32).astype(BF16)
    res = jnp.dot(hp, w_ref[...], preferred_element_type=F32)
    rows = res.shape[0] // GRID_W
    n = res.shape[1]
    for c in range(GRID_W):
        o_ref[0, :, c * n:(c + 1) * n] = res[c * rows:(c + 1) * rows, :]


def _inproj(s, mod, mod_base, g, w, col_major=False, tm=512):
    bv, tv, d = s.shape
    n = w.shape[1]
    const = lambda b, i: (0, 0)
    in_specs = [
        pl.BlockSpec((1, tm, d), lambda b, i: (b, i, 0)),
        pl.BlockSpec((1, mod.shape[1], d), lambda b, i: (b, 0, 0)),
        pl.BlockSpec((1, d), const),
        pl.BlockSpec(w.shape, const),
    ]
    args = [s, mod, g.reshape(1, d), w]
    if col_major:
        rows = tm // GRID_W
        out_spec = pl.BlockSpec((1, rows, GRID_W * n), lambda b, i: (b, i, 0))
        out_shape = jax.ShapeDtypeStruct((bv, tv // GRID_W, GRID_W * n), F32)
        src = (jnp.arange(tm) % rows) * GRID_W + jnp.arange(tm) // rows
        in_specs.append(pl.BlockSpec((tm, tm), const))
        args.append(jax.nn.one_hot(src, tm, dtype=BF16))
    else:
        out_spec = pl.BlockSpec((1, tm, n), lambda b, i: (b, i, 0))
        out_shape = jax.ShapeDtypeStruct((bv, tv, n), F32)
    return pl.pallas_call(
        functools.partial(_inproj_kernel, mod_base=mod_base, col_major=col_major),
        grid=(bv, tv // tm),
        in_specs=in_specs,
        out_specs=out_spec,
        out_shape=out_shape,
        compiler_params=pltpu.CompilerParams(
            dimension_semantics=("parallel", "parallel"), vmem_limit_bytes=VMEM_LIMIT),
        name="inproj",
    )(*args)


CONV_W = 256
RW = 384
OFF_B = 3 * CONV_W
RWKV_COLS = 3 * RW + 64 + 64 + 128
OFF_C = OFF_B + RWKV_COLS
P_COLS = OFF_C + 2 * RW
F_COLS = 3 * RW
V_W0, V_A0, V_KK, V_KA, V_RK, V_LNG, V_LNB, V_CW, V_CB, V_BR, V_BI, V_LAM = 0, 1, 2, 3, 4, 5, 6, 7, 11, 12, 13, 14
VEC_ROWS = 16


def _mixer_kernel(*refs, d, n_chunks, line_is_chunk):
    it = iter(refs)
    p_ref, pp_ref, pn_ref = next(it), next(it), next(it)
    if d == 1:
        f_in_ref = next(it)
    s0_ref, h0_ref = next(it), next(it)
    vec_ref, mu_ref, conva_ref = next(it), next(it), next(it)
    w2_ref, a2_ref, g2_ref, wr_ref, wi_ref = next(it), next(it), next(it), next(it), next(it)
    out_ref, s_ref, h_ref = next(it), next(it), next(it)

    reverse = d == 1
    i = pl.program_id(1)
    ci = (n_chunks - 1 - i) if reverse else i
    keep_prev = jnp.where(ci == 0, 0.0, 1.0)
    keep_next = jnp.where(ci == n_chunks - 1, 0.0, 1.0)

    @pl.when(i == 0)
    def _():
        s_ref[...] = s0_ref[...]
        h_ref[...] = h0_ref[...]

    def vec(r, sl=slice(None)):
        return vec_ref[r:r + 1, sl]

    nb = p_ref.shape[0]
    grp = [slice(b * CHUNK, (b + 1) * CHUNK) for b in range(nb)]

    def stacked(ref, c0, c1):
        return jnp.concatenate([ref[b, :, c0:c1] for b in range(nb)], axis=0)

    cur = stacked(p_ref, OFF_B, OFF_C)
    prev = stacked(pp_ref, OFF_B, OFF_C) * keep_prev
    nxt = stacked(pn_ref, OFF_B, OFF_C) * keep_next
    pb = (cur + mu_ref[0:1, :] * (_shift_dn(cur, prev, 1) - cur)
          + mu_ref[1:2, :] * (_shift_up(cur, nxt, 1) - cur))
    r = pb[:, 0:RW]
    k = pb[:, RW:2 * RW]
    v = pb[:, 2 * RW:3 * RW]
    dw = pb[:, 3 * RW:3 * RW + 64]
    da = pb[:, 3 * RW + 64:3 * RW + 128]
    dg = pb[:, 3 * RW + 128:3 * RW + 256]

    lane_p = lax.broadcasted_iota(jnp.int32, (PAIR, PAIR), 1)
    row_p = lax.broadcasted_iota(jnp.int32, (PAIR, PAIR), 0)
    same_head = (lane_p >= HEAD) == (row_p >= HEAD)
    head_ones = jnp.where(same_head, 1.0, 0.0).astype(BF16)

    def head_sum(x):
        return jnp.concatenate(
            [_dot_ones_rhs(x[:, PAIR * hp:PAIR * (hp + 1)], head_ones) for hp in range(RW // PAIR)],
            axis=1)

    kkr = k * vec(V_KK)
    kk = kkr * lax.rsqrt(head_sum(kkr * kkr) + 1e-12)
    lw = -DECAY_SCALE * jax.nn.sigmoid(vec(V_W0) + _dot(jnp.tanh(dw), w2_ref[...]))
    a = jax.nn.sigmoid(vec(V_A0) + _dot(da, a2_ref[...]))
    kd = k * (1.0 + (a - 1.0) * vec(V_KA))
    beta = kk * a
    bonus = head_sum(r * kd * vec(V_RK)) * v

    row_c = lax.broadcasted_iota(jnp.int32, (nb * CHUNK, nb * CHUNK), 0)
    col_c = lax.broadcasted_iota(jnp.int32, (nb * CHUNK, nb * CHUNK), 1)
    incl_c = (col_c >= row_c) if reverse else (col_c <= row_c)
    shift = CHUNK.bit_length() - 1
    incl_c = incl_c & ((row_c >> shift) == (col_c >> shift))
    c_inc = _dot_exact_lhs(jnp.where(incl_c, 1.0, 0.0).astype(BF16), lw)
    c_exc = c_inc - lw
    c_tot = [c_inc[g.start:g.start + 1, :] if reverse else c_inc[g.stop - 1:g.stop, :] for g in grp]
    c_mid = [c_inc[g.start + CHUNK // 2:g.start + CHUNK // 2 + 1, :] for g in grp]
    e_exc = jnp.exp(c_exc)
    e_inc = jnp.exp(c_inc)
    e_rem = jnp.exp(_rows_to_groups(c_tot) - c_inc)
    e_neg_mid = _rows_to_groups([jnp.exp(-m) for m in c_mid])
    e_mid_tot = _rows_to_groups([jnp.exp(m - t) for m, t in zip(c_mid, c_tot)])
    gam = [jnp.exp(t) for t in c_tot]
    at_abs = -(kk * e_exc)
    rt_abs = r * e_inc
    bg = beta * e_rem
    kg = kd * e_rem
    at_off = at_abs * e_neg_mid
    rt_off = rt_abs * e_neg_mid
    bh_off = bg * e_mid_tot
    kh_off = kg * e_mid_tot

    row_h = lax.broadcasted_iota(jnp.int32, (CHUNK, PAIR), 0)
    col_h = lax.broadcasted_iota(jnp.int32, (CHUNK, PAIR), 1) & (HEAD - 1)
    strict = (col_h > row_h) if reverse else (col_h < row_h)
    incl = (col_h >= row_h) if reverse else (col_h <= row_h)
    eye_h = jnp.where(col_h == row_h, 1.0, 0.0)
    eye_p = lane_p == row_p

    if reverse:
        g = _dot(jax.nn.sigmoid(dg), g2_ref[...])
        f_in = stacked(f_in_ref, 0, F_COLS)

    n_pair = RW // PAIR
    units = [(b, hp) for b in range(nb) for hp in range(n_pair)]
    pos = [(grp[b], slice(PAIR * hp, PAIR * (hp + 1))) for b, hp in units]

    def blk(sp, gs):
        return tuple(x[gs[0], gs[1]] for x in sp)

    s_sp = [_split2(s_ref[b, hp]) for b, hp in units]
    at_off_sp, rt_off_sp = _split2(at_off), _split2(rt_off)
    bh_off_sp, kh_off_sp = _split2(bh_off), _split2(kh_off)
    at_abs_sp, rt_abs_sp = _split2(at_abs), _split2(rt_abs)
    bg_sp, kg_sp, v_sp = _split2(bg), _split2(kg), _split2(v)
    gm = [_mm3(_cat([blk(at_off_sp, gs), blk(rt_off_sp, gs)], 0),
               _cat([_bd(blk(bh_off_sp, gs)), _bd(blk(kh_off_sp, gs))], 0), _NT) for gs in pos]
    a_m = [jnp.where(strict, m[0:CHUNK, 0:PAIR], 0.0) for m in gm]
    b_m = [jnp.where(strict, m[0:CHUNK, PAIR:2 * PAIR], 0.0) for m in gm]
    p_m = [jnp.where(incl, m[CHUNK:2 * CHUNK, 0:PAIR], 0.0) for m in gm]
    q_m = [jnp.where(incl, m[CHUNK:2 * CHUNK, PAIR:2 * PAIR], 0.0) for m in gm]
    z = [_mm3(_cat([blk(at_abs_sp, gs), blk(rt_abs_sp, gs)], 0), s_sp[u]) for u, gs in enumerate(pos)]
    w_m = [z[u][0:CHUNK] + _mm3(_split2(b_m[u]), _bd(blk(v_sp, gs))) for u, gs in enumerate(pos)]

    xr = stacked(p_ref, OFF_C, OFF_C + RW)
    if not reverse:
        xh = stacked(pp_ref, OFF_C, OFF_C + RW) * keep_prev
        u_l = (vec(V_CW + 3) * xr + vec(V_CW + 2) * _shift_dn(xr, xh, 1)
               + vec(V_CW + 1) * _shift_dn(xr, xh, 2) + vec(V_CW) * _shift_dn(xr, xh, 3))
    else:
        xh = stacked(pn_ref, OFF_C, OFF_C + RW) * keep_next
        u_l = (vec(V_CW) * xr + vec(V_CW + 1) * _shift_up(xr, xh, 1)
               + vec(V_CW + 2) * _shift_up(xr, xh, 2) + vec(V_CW + 3) * _shift_up(xr, xh, 3))
    u_l = u_l + vec(V_CB)
    rg = jax.nn.sigmoid(_dot(u_l, wr_ref[...]) + vec(V_BR))
    ig = jax.nn.sigmoid(_dot(u_l, wi_ref[...]) + vec(V_BI))
    log_a = -LRU_C * (_softplus(-vec(V_LAM)) * rg)
    a_l = jnp.exp(log_a)
    th = jnp.tanh(log_a)
    b_l = jnp.sqrt(-2.0 * th / (1.0 - th)) * (ig * u_l)

    t_m = [eye_h + m for m in a_m]
    pw_sp = [_split2(_mm3(ps, _bd(ps))) for ps in [_split2(m) for m in a_m]]
    step = 1
    for lvl in range(5):
        rhs = [_bd(ps) for ps in pw_sp]
        if lvl < 4:
            prod = [_mm3(_cat([ps, _split2(t)], 0), r) for ps, t, r in zip(pw_sp, t_m, rhs)]
            pw_sp = [_split2(p[0:CHUNK]) for p in prod]
            t_m = [t + p[CHUNK:2 * CHUNK] for t, p in zip(t_m, prod)]
        else:
            t_m = [t + _mm3(_split2(t), r) for t, r in zip(t_m, rhs)]
        b_l = b_l + a_l * _scan_shift(b_l, step, 0.0, reverse)
        a_l = a_l * _scan_shift(a_l, step, 1.0, reverse)
        step *= 2
    u_sp = [_split2(_mm3(_split2(t), _bd(_split2(w)))) for t, w in zip(t_m, w_m)]
    y_rw = [z[u][CHUNK:2 * CHUNK] + _mm3(
        _cat([_split2(p_m[u]), _split2(q_m[u])], 1),
        _cat([_bd(u_sp[u]), _bd(blk(v_sp, gs))], 0)) for u, gs in enumerate(pos)]
    s_upd = []
    for u, (gs, (b, hp)) in enumerate(zip(pos, units)):
        dgam = jnp.where(eye_p, jnp.broadcast_to(gam[b][:, gs[1]], (PAIR, PAIR)), 0.0)
        s_new = _mm3(_cat([blk(bg_sp, gs), blk(kg_sp, gs), _split2(dgam)], 0),
                     _cat([u_sp[u], blk(v_sp, gs), s_sp[u]], 0), _TN)
        s_upd.append(jnp.where(same_head, s_new, 0.0))

    if not reverse:
        y_out = y_rw
    else:
        y_out = []
        for u, ((gr_, sl), (b, hp)) in enumerate(zip(pos, units)):
            ysum = y_rw[u] + f_in[gr_, sl]
            mean = _dot_ones_rhs(ysum, head_ones) * (1.0 / HEAD)
            cen = ysum - mean
            var = _dot_ones_rhs(cen * cen, head_ones) * (1.0 / HEAD)
            gn = cen * lax.rsqrt(var + GN_EPS) * vec(V_LNG, sl) + vec(V_LNB, sl)
            y_out.append((gn + (f_in[gr_, RW + PAIR * hp:RW + PAIR * (hp + 1)] + bonus[gr_, sl])) * g[gr_, sl])

    y_base = CONV_W if reverse else 0
    for u, (b, hp) in enumerate(units):
        s_ref[b, hp] = s_upd[u]
        out_ref[b, :, y_base + PAIR * hp:y_base + PAIR * (hp + 1)] = y_out[u]
    if not reverse:
        for b in range(nb):
            out_ref[b, :, RW:2 * RW] = bonus[grp[b]]

    while step < CHUNK:
        b_l = b_l + a_l * _scan_shift(b_l, step, 0.0, reverse)
        a_l = a_l * _scan_shift(a_l, step, 1.0, reverse)
        step *= 2
    h = b_l + a_l * _rows_to_groups([h_ref[b] for b in range(nb)])
    for b in range(nb):
        hb = h[grp[b]]
        h_ref[b] = hb[0:1, :] if reverse else hb[CHUNK - 1:CHUNK, :]

    if not reverse:
        for b in range(nb):
            out_ref[b, :, 2 * RW:3 * RW] = h[grp[b]]
    else:
        gr = stacked(p_ref, OFF_C + RW, OFF_C + 2 * RW)
        y_lru = jax.nn.gelu(gr) * (f_in[:, 2 * RW:3 * RW] + h)
        bgate = stacked(p_ref, 0, CONV_W)
        uc = stacked(p_ref, CONV_W, 2 * CONV_W) * stacked(p_ref, 2 * CONV_W, 3 * CONV_W)
        if line_is_chunk:
            zero = jnp.zeros((nb * HALO, CONV_W), F32)
            up, un = zero, zero
        else:
            up = stacked(pp_ref, CONV_W, 2 * CONV_W) * stacked(pp_ref, 2 * CONV_W, 3 * CONV_W) * keep_prev
            un = stacked(pn_ref, CONV_W, 2 * CONV_W) * stacked(pn_ref, 2 * CONV_W, 3 * CONV_W) * keep_next
        yc = bgate * (conva_ref[0:1, :] * _shift_dn(uc, up, 1) + conva_ref[1:2, :] * uc
                      + conva_ref[2:3, :] * _shift_up(uc, un, 1))
        for b in range(nb):
            out_ref[b, :, CONV_W + RW:CONV_W + 2 * RW] = y_lru[grp[b]]
            out_ref[b, :, 0:CONV_W] = yc[grp[b]]


def _mixer_pass(p, f_in, s0, h0, prm, d, col_major, line_is_chunk):
    bsz = p.shape[0]
    t_len = p.shape[1] * p.shape[2] // P_COLS
    n = t_len // CHUNK
    reverse = d == 1
    out_cols = 4 * CONV_W if reverse else F_COLS

    def ci(i):
        return (n - 1 - i) if reverse else i

    if col_major:
        assert p.shape == (bsz, CHUNK, GRID_W * P_COLS) and n == GRID_W
        p_view = p
        cur_spec = pl.BlockSpec((MIX_NB, CHUNK,P_COLS), lambda b, i: (b, 0, ci(i)))
        prev_spec = pl.BlockSpec((MIX_NB, HALO,P_COLS),
                                 lambda b, i: (b, CHUNK // HALO - 1, jnp.maximum(ci(i) - 1, 0)))
        next_spec = pl.BlockSpec((MIX_NB, HALO,P_COLS), lambda b, i: (b, 0, jnp.minimum(ci(i) + 1, n - 1)))
    else:
        p_view = p
        per = CHUNK // HALO
        cur_spec = pl.BlockSpec((MIX_NB, CHUNK,P_COLS), lambda b, i: (b, ci(i), 0))
        prev_spec = pl.BlockSpec((MIX_NB, HALO,P_COLS), lambda b, i: (b, jnp.maximum(per * ci(i) - 1, 0), 0))
        next_spec = pl.BlockSpec((MIX_NB, HALO,P_COLS),
                                 lambda b, i: (b, jnp.minimum(per * ci(i) + per, per * n - 1), 0))
    full = lambda a: pl.BlockSpec(a.shape, lambda b, i: (0,) * a.ndim)
    in_specs = [cur_spec, prev_spec, next_spec]
    args = [p_view, p_view, p_view]
    if reverse:
        in_specs.append(pl.BlockSpec((MIX_NB, CHUNK,F_COLS), lambda b, i: (b, ci(i), 0)))
        args.append(f_in)
    state_specs = [pl.BlockSpec((MIX_NB,) + s0.shape[1:], lambda b, i: (b, 0, 0, 0)),
                   pl.BlockSpec((MIX_NB, 1, RW), lambda b, i: (b, 0, 0))]
    in_specs += state_specs
    args += [s0, h0]
    params = [prm["vec"][d], prm["mu"], prm["conv_a"], prm["w2"][d], prm["a2"][d], prm["g2"],
              prm["wr"][d], prm["wi"][d]]
    in_specs += [full(a) for a in params]
    args += params

    if reverse and col_major:
        out_shape = jax.ShapeDtypeStruct((bsz, CHUNK, n * out_cols), F32)
        out_spec = pl.BlockSpec((MIX_NB, CHUNK,out_cols), lambda b, i: (b, 0, ci(i)))
    else:
        out_shape = jax.ShapeDtypeStruct((bsz, t_len, out_cols), F32)
        out_spec = pl.BlockSpec((MIX_NB, CHUNK,out_cols), lambda b, i: (b, ci(i), 0))

    out, s_fin, h_fin = pl.pallas_call(
        functools.partial(_mixer_kernel, d=d, n_chunks=n, line_is_chunk=line_is_chunk),
        grid=(bsz // MIX_NB, n),
        in_specs=in_specs,
        out_specs=[out_spec] + state_specs,
        out_shape=[out_shape, jax.ShapeDtypeStruct(s0.shape, F32), jax.ShapeDtypeStruct(h0.shape, F32)],
        compiler_params=pltpu.CompilerParams(
            dimension_semantics=("parallel", "arbitrary"), vmem_limit_bytes=VMEM_LIMIT),
        name="mixer_bwd" if reverse else "mixer_fwd",
    )(*args)
    return out, s_fin, h_fin


def _mix_stream(p, prm, col_major, line_is_chunk, init):
    (s0f, h0f), (s0b, h0b) = init
    f, sf, hf = _mixer_pass(p, None, s0f, h0f, prm, 0, col_major, line_is_chunk)
    y, sb, hb = _mixer_pass(p, f, s0b, h0b, prm, 1, col_major, line_is_chunk)
    return y, ((sf, hf), (sb, hb))


def _mixer_params(l, conv_a, rwkv_mu, rwkv_w0, rwkv_w2, rwkv_a0, rwkv_a2, rwkv_g2, rwkv_kk, rwkv_ka,
                  rwkv_rk, rwkv_lnx_g, rwkv_lnx_b, lru_conv_w, lru_conv_b, lru_w_r, lru_b_r, lru_w_i,
                  lru_b_i, lru_lam):
    def both(a):
        return jnp.broadcast_to(a[None], (2,) + a.shape)

    rows = [rwkv_w0[l], rwkv_a0[l], both(rwkv_kk[l]), both(rwkv_ka[l]), both(rwkv_rk[l]),
            both(rwkv_lnx_g[l]), both(rwkv_lnx_b[l]),
            lru_conv_w[l][:, 0], lru_conv_w[l][:, 1], lru_conv_w[l][:, 2], lru_conv_w[l][:, 3],
            lru_conv_b[l], lru_b_r[l], lru_b_i[l], lru_lam[l], jnp.zeros((2, RW), F32)]
    assert len(rows) == VEC_ROWS
    vec = jnp.stack(rows, axis=1)
    nb = lru_w_r.shape[2]
    eye = jnp.eye(nb, dtype=F32)

    def bd(w):
        return jnp.einsum("dnij,nm->dnimj", w, eye).reshape(2, nb * HEAD, nb * HEAD).astype(BF16)

    return {
        "vec": vec, "mu": rwkv_mu[l], "conv_a": conv_a[l],
        "w2": rwkv_w2[l].astype(BF16), "a2": rwkv_a2[l].astype(BF16), "g2": rwkv_g2[l].astype(BF16),
        "wr": bd(lru_w_r[l]), "wi": bd(lru_w_i[l]),
    }


def kernel(x, c, ctx, c_ctx, w_mod, b_mod, g_ffn1, w_gu1, w_down1, g_mix, w_in, conv_a, rwkv_mu, rwkv_w0, rwkv_w2, rwkv_a0, rwkv_a2, rwkv_g2, rwkv_kk, rwkv_ka, rwkv_rk, rwkv_lnx_g, rwkv_lnx_b, lru_conv_w, lru_conv_b, lru_w_r, lru_b_r, lru_w_i, lru_b_i, lru_lam, w_out, g_ffn2, w_gu2, w_down2, g_final):
    bsz, t_len, d = x.shape
    depth = w_mod.shape[0]
    n_ctx = ctx.shape[1]
    assert w_in.shape[2] == P_COLS and t_len % CHUNK == 0 and n_ctx % CHUNK == 0

    mod_rows = -(-(bsz + 1) // 8) * 8
    cc = jnp.concatenate([c, c_ctx[None, :], jnp.zeros((mod_rows - bsz - 1, d), F32)], axis=0)
    mods = _modulation(cc, w_mod, b_mod)

    zero_state = ((jnp.zeros((bsz, RW // PAIR, PAIR, PAIR), F32), jnp.zeros((bsz, 1, RW), F32)),) * 2
    s_lat = x
    s_ctx = ctx.reshape(1, bsz * n_ctx, d)
    for l in range(depth):
        last = l == depth - 1
        m_lat = mods[l, :bsz].reshape(bsz, 9, d)
        m_ctx = mods[l, bsz:bsz + 1].reshape(1, 9, d)
        wgu1, wd1 = w_gu1[l].astype(BF16), w_down1[l].astype(BF16)
        wgu2, wd2 = w_gu2[l].astype(BF16), w_down2[l].astype(BF16)
        win, wout = w_in[l].astype(BF16), w_out[l].astype(BF16)
        prm = _mixer_params(l, conv_a, rwkv_mu, rwkv_w0, rwkv_w2, rwkv_a0, rwkv_a2, rwkv_g2, rwkv_kk,
                            rwkv_ka, rwkv_rk, rwkv_lnx_g, rwkv_lnx_b, lru_conv_w, lru_conv_b, lru_w_r,
                            lru_b_r, lru_w_i, lru_b_i, lru_lam)

        s_lat = _ffn(s_lat, m_lat, 0, g_ffn1[l], wgu1, wd1)
        s_ctx = _ffn(s_ctx, m_ctx, 0, g_ffn1[l], wgu1, wd1)
        col_major = l % 2 == 1
        p_lat = _inproj(s_lat, m_lat, 3, g_mix[l], win, col_major=col_major)
        p_ctx = _inproj(s_ctx, m_ctx, 3, g_mix[l], win)

        y_ctx, ctx_fin = _mix_stream(p_ctx.reshape(bsz, n_ctx, P_COLS), prm, False, False, zero_state)
        y_lat, _ = _mix_stream(p_lat, prm, col_major, True, ctx_fin)

        s_lat = _ffn(s_lat, m_lat, 6, g_ffn2[l], wgu2, wd2, y=y_lat, wout=wout, y_col_major=col_major,
                     gfin=g_final if last else None)
        if not last:
            s_ctx = _ffn(s_ctx, m_ctx, 6, g_ffn2[l], wgu2, wd2,
                         y=y_ctx.reshape(1, bsz * n_ctx, 4 * CONV_W), wout=wout)
    return s_lat
```

```python
import functools

import jax
import jax.numpy as jnp
from jax import lax
from jax.experimental import pallas as pl
from jax.experimental.pallas import tpu as pltpu

F32 = jnp.float32
BF16 = jnp.bfloat16

HEAD = 64
LANES = 128
PAIR = 2 * HEAD
CHUNK = 64
HALO = 8
INV_BASE = 8
MIX_NB = 4
GRID_W = 64
NORM_EPS = 1e-6
GN_EPS = 64e-5
LRU_C = 8.0
DECAY_SCALE = 0.6065306597126334
VMEM_LIMIT = 56 * 1024 * 1024


def _dot(a, b):
    return jnp.dot(a.astype(BF16), b.astype(BF16), preferred_element_type=F32)


def _split2(a):
    hi = a.astype(BF16)
    lo = (a - hi.astype(F32)).astype(BF16)
    return hi, lo


_NN = (((1,), (0,)), ((), ()))
_NT = (((1,), (1,)), ((), ()))
_TN = (((0,), (0,)), ((), ()))


def _dot3(a, b, dims=_NN):
    ah, al = _split2(a)
    bh, bl = _split2(b)
    d = functools.partial(lax.dot_general, dimension_numbers=dims, preferred_element_type=F32)
    return d(ah, bh) + (d(ah, bl) + d(al, bh))


def _mm3(a, b, dims=_NN):
    ah, al = a
    bh, bl = b
    d = functools.partial(lax.dot_general, dimension_numbers=dims, preferred_element_type=F32)
    free = 1 if dims == _TN else 0
    m = ah.shape[free]
    both = d(jnp.concatenate([ah, al], axis=free), bh)
    return both[:m] + (d(ah, bl) + both[m:])


def _mm3_and_1(a, c_hi, b, dims):
    ah, al = a
    bh, bl = b
    d = functools.partial(lax.dot_general, dimension_numbers=dims, preferred_element_type=F32)
    m = ah.shape[0]
    both = d(jnp.concatenate([ah, al, c_hi], axis=0), bh)
    return both[:m] + (d(ah, bl) + both[m:2 * m]), both[2 * m:]


def _cat(parts, axis):
    return tuple(jnp.concatenate([p[i] for p in parts], axis=axis) for i in range(2))


def _lanes(sp, sl):
    return tuple(x[:, sl] for x in sp)


def _bd(sp):
    return tuple(_blockdiag(x) for x in sp)


def _dot_ones_rhs(a, ones_rhs):
    ah, al = _split2(a)
    d = functools.partial(jnp.dot, preferred_element_type=F32)
    return d(ah, ones_rhs) + d(al, ones_rhs)


def _dot_exact_lhs(ones_lhs, b):
    b1 = b.astype(BF16)
    r1 = b - b1.astype(F32)
    b2 = r1.astype(BF16)
    b3 = (r1 - b2.astype(F32)).astype(BF16)
    d = functools.partial(jnp.dot, preferred_element_type=F32)
    return d(ones_lhs, b1) + (d(ones_lhs, b2) + d(ones_lhs, b3))


def _softplus(x):
    return jnp.maximum(x, 0.0) + jnp.log1p(jnp.exp(-jnp.abs(x)))


def _rms_modulate(s, g, shift, scale):
    y = s * lax.rsqrt(jnp.mean(s * s, axis=-1, keepdims=True) + NORM_EPS)
    return (y * g) * (1.0 + scale) + shift


def _per_group(fn, x, halo, *args):
    nb = x.shape[0] // CHUNK
    if nb == 1:
        return fn(x, halo, *args)
    return jnp.concatenate(
        [fn(x[b * CHUNK:(b + 1) * CHUNK], None if halo is None else halo[b * HALO:(b + 1) * HALO], *args)
         for b in range(nb)], axis=0)


def _shift_dn(x, prev, k):
    return _per_group(_shift_dn1, x, prev, k)


def _shift_up(x, nxt, k):
    return _per_group(_shift_up1, x, nxt, k)


def _scan_shift(x, k, fill, reverse):
    return _per_group(_scan_shift1, x, None, k, fill, reverse)


def _rows_to_groups(rows):
    return jnp.concatenate([jnp.broadcast_to(r, (CHUNK, r.shape[1])) for r in rows], axis=0)


def _shift_dn1(x, prev, k):
    ch = x.shape[1]
    rolled = pltpu.roll(x, k, axis=0)
    pr = pltpu.roll(prev, k, axis=0)
    r8 = lax.broadcasted_iota(jnp.int32, (HALO, ch), 0)
    head = jnp.where(r8 < k, pr, rolled[:HALO])
    return jnp.concatenate([head, rolled[HALO:]], axis=0)


def _shift_up1(x, nxt, k):
    n, ch = x.shape
    rolled = pltpu.roll(x, n - k, axis=0)
    nr = pltpu.roll(nxt, HALO - k, axis=0)
    r8 = lax.broadcasted_iota(jnp.int32, (HALO, ch), 0)
    tail = jnp.where(r8 >= HALO - k, nr, rolled[n - HALO:])
    return jnp.concatenate([rolled[:n - HALO], tail], axis=0)


def _scan_shift1(x, _, k, fill, reverse):
    n, ch = x.shape
    if k % HALO == 0:
        pad = jnp.full((k, ch), fill, x.dtype)
        return jnp.concatenate([x[k:], pad] if reverse else [pad, x[:n - k]], axis=0)
    rows = lax.broadcasted_iota(jnp.int32, (n, ch), 0)
    if not reverse:
        return jnp.where(rows >= k, pltpu.roll(x, k, axis=0), fill)
    return jnp.where(rows < n - k, pltpu.roll(x, n - k, axis=0), fill)


def _blockdiag(xp):
    n = xp.shape[0]
    lane = lax.broadcasted_iota(jnp.int32, (n, PAIR), 1)
    zero = jnp.zeros_like(xp)
    top = jnp.where(lane < HEAD, xp, zero)
    bot = jnp.where(lane >= HEAD, xp, zero)
    return jnp.concatenate([top, bot], axis=0)


def _mod_kernel(c_ref, w_ref, b_ref, o_ref):
    cc = c_ref[...]
    act = cc * jax.nn.sigmoid(cc)
    o_ref[0] = _dot(act, w_ref[0]) + b_ref[0]


def _modulation(cc, w_mod, b_mod):
    depth, d, n = w_mod.shape
    rows = cc.shape[0]
    tn = n // 8
    return pl.pallas_call(
        _mod_kernel,
        grid=(depth, n // tn),
        in_specs=[
            pl.BlockSpec((rows, d), lambda l, j: (0, 0)),
            pl.BlockSpec((1, d, tn), lambda l, j: (l, 0, j)),
            pl.BlockSpec((1, 1, tn), lambda l, j: (l, 0, j)),
        ],
        out_specs=pl.BlockSpec((1, rows, tn), lambda l, j: (l, 0, j)),
        out_shape=jax.ShapeDtypeStruct((depth, rows, n), F32),
        compiler_params=pltpu.CompilerParams(
            dimension_semantics=("parallel", "parallel"), vmem_limit_bytes=VMEM_LIMIT),
        name="modulation",
    )(cc, w_mod, b_mod.reshape(depth, 1, n))


def _ffn_kernel(*refs, mod_base, has_pre, y_col_major, final_norm, d_ff, tf):
    it = iter(refs)
    s_ref = next(it)
    if has_pre:
        y_ref = next(it)
        wout_ref = next(it)
        if y_col_major:
            perm_ref = next(it)
    mod_ref = next(it)
    g_ref = next(it)
    wgu_ref = next(it)
    wdown_ref = next(it)
    if final_norm:
        gfin_ref = next(it)
    o_ref = next(it)
    acc_ref = next(it)

    s = s_ref[0]
    if has_pre:
        gate_mix = mod_ref[0, mod_base - 1:mod_base, :]
        if y_col_major:
            mix = wout_ref.shape[0]
            yc = jnp.concatenate([y_ref[0, :, c * mix:(c + 1) * mix] for c in range(GRID_W)], axis=0)
            yb = jnp.dot(perm_ref[...], yc.astype(BF16), preferred_element_type=F32).astype(BF16)
        else:
            yb = y_ref[0].astype(BF16)
        s = s + gate_mix * jnp.dot(yb, wout_ref[...], preferred_element_type=F32)
    shift = mod_ref[0, mod_base:mod_base + 1, :]
    scale = mod_ref[0, mod_base + 1:mod_base + 2, :]
    gate = mod_ref[0, mod_base + 2:mod_base + 3, :]
    hb = _rms_modulate(s, g_ref[...], shift, scale).astype(BF16)
    for j in range(d_ff // tf):
        gt = jnp.dot(hb, wgu_ref[:, j * tf:(j + 1) * tf], preferred_element_type=F32)
        up = jnp.dot(hb, wgu_ref[:, d_ff + j * tf:d_ff + (j + 1) * tf], preferred_element_type=F32)
        act = ((gt * jax.nn.sigmoid(gt)) * up).astype(BF16)
        part = jnp.dot(act, wdown_ref[j * tf:(j + 1) * tf, :], preferred_element_type=F32)
        if j == 0:
            acc_ref[...] = part
        else:
            acc_ref[...] += part
    out = s + (0.5 * gate) * acc_ref[...]
    if final_norm:
        out = out * lax.rsqrt(jnp.mean(out * out, axis=-1, keepdims=True) + NORM_EPS) * gfin_ref[...]
    o_ref[0] = out


def _ffn(s, mod, mod_base, g, wgu, wdown, y=None, wout=None, y_col_major=False, gfin=None, tm=512):
    bv, tv, d = s.shape
    d_ff = wdown.shape[0]
    tf = 256
    has_pre = y is not None
    final_norm = gfin is not None
    const = lambda b, i: (0, 0)
    tile = lambda b, i: (b, i, 0)
    in_specs = [pl.BlockSpec((1, tm, d), tile)]
    args = [s]
    if has_pre:
        y_rows = tm // GRID_W if y_col_major else tm
        in_specs += [pl.BlockSpec((1, y_rows, y.shape[2]), tile), pl.BlockSpec(wout.shape, const)]
        args += [y, wout]
        if y_col_major:
            src = (jnp.arange(tm) % GRID_W) * y_rows + jnp.arange(tm) // GRID_W
            in_specs.append(pl.BlockSpec((tm, tm), const))
            args.append(jax.nn.one_hot(src, tm, dtype=BF16))
    in_specs += [
        pl.BlockSpec((1, mod.shape[1], d), lambda b, i: (b, 0, 0)),
        pl.BlockSpec((1, d), const),
        pl.BlockSpec(wgu.shape, const),
        pl.BlockSpec(wdown.shape, const),
    ]
    args += [mod, g.reshape(1, d), wgu, wdown]
    if final_norm:
        in_specs.append(pl.BlockSpec((1, d), const))
        args.append(gfin.reshape(1, d))
    body = functools.partial(_ffn_kernel, mod_base=mod_base, has_pre=has_pre, y_col_major=y_col_major,
                             final_norm=final_norm, d_ff=d_ff, tf=tf)
    return pl.pallas_call(
        body,
        grid=(bv, tv // tm),
        in_specs=in_specs,
        out_specs=pl.BlockSpec((1, tm, d), tile),
        out_shape=jax.ShapeDtypeStruct(s.shape, F32),
        scratch_shapes=[pltpu.VMEM((tm, d), F32)],
        compiler_params=pltpu.CompilerParams(
            dimension_semantics=("parallel", "parallel"), vmem_limit_bytes=VMEM_LIMIT),
        name="ffn",
    )(*args)


def _inproj_kernel(s_ref, mod_ref, g_ref, w_ref, *rest, mod_base, col_major):
    shift = mod_ref[0, mod_base:mod_base + 1, :]
    scale = mod_ref[0, mod_base + 1:mod_base + 2, :]
    hb = _rms_modulate(s_ref[0], g_ref[...], shift, scale).astype(BF16)
    if not col_major:
        o_ref, = rest
        o_ref[0] = jnp.dot(hb, w_ref[...], preferred_element_type=F32)
        return
    perm_ref, o_ref = rest
    hp = jnp.dot(perm_ref[...], hb, preferred_element_type=F32).astype(BF16)
    res = jnp.dot(hp, w_ref[...], preferred_element_type=F32)
    rows = res.shape[0] // GRID_W
    n = res.shape[1]
    for c in range(GRID_W):
        o_ref[0, :, c * n:(c + 1) * n] = res[c * rows:(c + 1) * rows, :]


def _inproj(s, mod, mod_base, g, w, col_major=False, tm=512):
    bv, tv, d = s.shape
    n = w.shape[1]
    const = lambda b, i: (0, 0)
    in_specs = [
        pl.BlockSpec((1, tm, d), lambda b, i: (b, i, 0)),
        pl.BlockSpec((1, mod.shape[1], d), lambda b, i: (b, 0, 0)),
        pl.BlockSpec((1, d), const),
        pl.BlockSpec(w.shape, const),
    ]
    args = [s, mod, g.reshape(1, d), w]
    if col_major:
        rows = tm // GRID_W
        out_spec = pl.BlockSpec((1, rows, GRID_W * n), lambda b, i: (b, i, 0))
        out_shape = jax.ShapeDtypeStruct((bv, tv // GRID_W, GRID_W * n), F32)
        src = (jnp.arange(tm) % rows) * GRID_W + jnp.arange(tm) // rows
        in_specs.append(pl.BlockSpec((tm, tm), const))
        args.append(jax.nn.one_hot(src, tm, dtype=BF16))
    else:
        out_spec = pl.BlockSpec((1, tm, n), lambda b, i: (b, i, 0))
        out_shape = jax.ShapeDtypeStruct((bv, tv, n), F32)
    return pl.pallas_call(
        functools.partial(_inproj_kernel, mod_base=mod_base, col_major=col_major),
        grid=(bv, tv // tm),
        in_specs=in_specs,
        out_specs=out_spec,
        out_shape=out_shape,
        compiler_params=pltpu.CompilerParams(
            dimension_semantics=("parallel", "parallel"), vmem_limit_bytes=VMEM_LIMIT),
        name="inproj",
    )(*args)


CONV_W = 256
RW = 384
OFF_B = 3 * CONV_W
RWKV_COLS = 3 * RW + 64 + 64 + 128
OFF_C = OFF_B + RWKV_COLS
P_COLS = OFF_C + 2 * RW
F_COLS = 3 * RW
V_W0, V_A0, V_KK, V_KA, V_RK, V_LNG, V_LNB, V_CW, V_CB, V_BR, V_BI, V_LAM = 0, 1, 2, 3, 4, 5, 6, 7, 11, 12, 13, 14
VEC_ROWS = 16


def _mixer_kernel(*refs, d, n_chunks, line_is_chunk):
    it = iter(refs)
    p_ref, pp_ref, pn_ref = next(it), next(it), next(it)
    if d == 1:
        f_in_ref = next(it)
    s0_ref, h0_ref = next(it), next(it)
    vec_ref, mu_ref, conva_ref = next(it), next(it), next(it)
    w2_ref, a2_ref, g2_ref, wr_ref, wi_ref = next(it), next(it), next(it), next(it), next(it)
    out_ref, s_ref, h_ref = next(it), next(it), next(it)

    reverse = d == 1
    i = pl.program_id(1)
    ci = (n_chunks - 1 - i) if reverse else i
    keep_prev = jnp.where(ci == 0, 0.0, 1.0)
    keep_next = jnp.where(ci == n_chunks - 1, 0.0, 1.0)

    @pl.when(i == 0)
    def _():
        s_ref[...] = s0_ref[...]
        h_ref[...] = h0_ref[...]

    def vec(r, sl=slice(None)):
        return vec_ref[r:r + 1, sl]

    nb = p_ref.shape[0]
    grp = [slice(b * CHUNK, (b + 1) * CHUNK) for b in range(nb)]

    def stacked(ref, c0, c1):
        return jnp.concatenate([ref[b, :, c0:c1] for b in range(nb)], axis=0)

    cur = stacked(p_ref, OFF_B, OFF_C)
    prev = stacked(pp_ref, OFF_B, OFF_C) * keep_prev
    nxt = stacked(pn_ref, OFF_B, OFF_C) * keep_next
    pb = (cur + mu_ref[0:1, :] * (_shift_dn(cur, prev, 1) - cur)
          + mu_ref[1:2, :] * (_shift_up(cur, nxt, 1) - cur))
    r = pb[:, 0:RW]
    k = pb[:, RW:2 * RW]
    v = pb[:, 2 * RW:3 * RW]
    dw = pb[:, 3 * RW:3 * RW + 64]
    da = pb[:, 3 * RW + 64:3 * RW + 128]
    dg = pb[:, 3 * RW + 128:3 * RW + 256]

    lane_p = lax.broadcasted_iota(jnp.int32, (PAIR, PAIR), 1)
    row_p = lax.broadcasted_iota(jnp.int32, (PAIR, PAIR), 0)
    same_head = (lane_p >= HEAD) == (row_p >= HEAD)
    head_ones = jnp.where(same_head, 1.0, 0.0).astype(BF16)

    def head_sum(x):
        return jnp.concatenate(
            [_dot_ones_rhs(x[:, PAIR * hp:PAIR * (hp + 1)], head_ones) for hp in range(RW // PAIR)],
            axis=1)

    kkr = k * vec(V_KK)
    kk = kkr * lax.rsqrt(head_sum(kkr * kkr) + 1e-12)
    lw = -DECAY_SCALE * jax.nn.sigmoid(vec(V_W0) + _dot(jnp.tanh(dw), w2_ref[...]))
    a = jax.nn.sigmoid(vec(V_A0) + _dot(da, a2_ref[...]))
    kd = k * (1.0 + (a - 1.0) * vec(V_KA))
    beta = kk * a
    bonus = head_sum(r * kd * vec(V_RK)) * v

    row_c = lax.broadcasted_iota(jnp.int32, (nb * CHUNK, nb * CHUNK), 0)
    col_c = lax.broadcasted_iota(jnp.int32, (nb * CHUNK, nb * CHUNK), 1)
    incl_c = (col_c >= row_c) if reverse else (col_c <= row_c)
    shift = CHUNK.bit_length() - 1
    incl_c = incl_c & ((row_c >> shift) == (col_c >> shift))
    c_inc = _dot_exact_lhs(jnp.where(incl_c, 1.0, 0.0).astype(BF16), lw)
    c_exc = c_inc - lw
    c_tot = [c_inc[g.start:g.start + 1, :] if reverse else c_inc[g.stop - 1:g.stop, :] for g in grp]
    e_neg = jnp.exp(-c_inc)
    gam = [jnp.exp(t) for t in c_tot]
    gam_rows = _rows_to_groups(gam)
    at = -(kk * jnp.exp(c_exc))
    rt = r * jnp.exp(c_inc)
    bh = beta * e_neg
    kh = kd * e_neg
    bg = bh * gam_rows
    kg = kh * gam_rows

    row_h = lax.broadcasted_iota(jnp.int32, (CHUNK, PAIR), 0)
    col_h = lax.broadcasted_iota(jnp.int32, (CHUNK, PAIR), 1) & (HEAD - 1)
    strict = (col_h > row_h) if reverse else (col_h < row_h)
    incl = (col_h >= row_h) if reverse else (col_h <= row_h)
    eye_h = jnp.where(col_h == row_h, 1.0, 0.0)
    eye_p = lane_p == row_p

    if reverse:
        g = _dot(jax.nn.sigmoid(dg), g2_ref[...])
        f_in = stacked(f_in_ref, 0, F_COLS)

    n_pair = RW // PAIR
    units = [(b, hp) for b in range(nb) for hp in range(n_pair)]
    pos = [(grp[b], slice(PAIR * hp, PAIR * (hp + 1))) for b, hp in units]

    def blk(sp, gs):
        return tuple(x[gs[0], gs[1]] for x in sp)

    s_sp = [_split2(s_ref[b, hp]) for b, hp in units]
    at_sp, bh_sp, kh_sp = _split2(at), _split2(bh), _split2(kh)
    bg_sp, kg_sp, v_sp = _split2(bg), _split2(kg), _split2(v)
    rt_hi = rt.astype(BF16)
    gm = [_mm3_and_1(blk(at_sp, gs), rt_hi[gs[0], gs[1]],
                     _cat([_bd(blk(bh_sp, gs)), _bd(blk(kh_sp, gs))], 0), _NT) for gs in pos]
    a_m = [jnp.where(strict, m[0][:, 0:PAIR], 0.0) for m in gm]
    b_m = [jnp.where(strict, m[0][:, PAIR:2 * PAIR], 0.0) for m in gm]
    p_hi = [jnp.where(incl, m[1][:, 0:PAIR], 0.0).astype(BF16) for m in gm]
    q_hi = [jnp.where(incl, m[1][:, PAIR:2 * PAIR], 0.0).astype(BF16) for m in gm]
    z = [_mm3_and_1(blk(at_sp, gs), rt_hi[gs[0], gs[1]], s_sp[u], _NN) for u, gs in enumerate(pos)]
    w_m = [z[u][0] + _mm3(_split2(b_m[u]), _bd(blk(v_sp, gs))) for u, gs in enumerate(pos)]

    lru = {}

    def lru_pieces():
        xr = stacked(p_ref, OFF_C, OFF_C + RW)
        if not reverse:
            xh = stacked(pp_ref, OFF_C, OFF_C + RW) * keep_prev
            u_l = (vec(V_CW + 3) * xr + vec(V_CW + 2) * _shift_dn(xr, xh, 1)
                   + vec(V_CW + 1) * _shift_dn(xr, xh, 2) + vec(V_CW) * _shift_dn(xr, xh, 3))
        else:
            xh = stacked(pn_ref, OFF_C, OFF_C + RW) * keep_next
            u_l = (vec(V_CW) * xr + vec(V_CW + 1) * _shift_up(xr, xh, 1)
                   + vec(V_CW + 2) * _shift_up(xr, xh, 2) + vec(V_CW + 3) * _shift_up(xr, xh, 3))
        u_l = u_l + vec(V_CB)
        yield
        rg = jax.nn.sigmoid(_dot(u_l, wr_ref[...]) + vec(V_BR))
        yield
        ig = jax.nn.sigmoid(_dot(u_l, wi_ref[...]) + vec(V_BI))
        yield
        log_a = -LRU_C * (_softplus(-vec(V_LAM)) * rg)
        a_l = jnp.exp(log_a)
        th = jnp.tanh(log_a)
        b_l = jnp.sqrt(-2.0 * th / (1.0 - th)) * (ig * u_l)
        yield
        step = 1
        while step < CHUNK:
            b_l = b_l + a_l * _scan_shift(b_l, step, 0.0, reverse)
            a_l = a_l * _scan_shift(a_l, step, 1.0, reverse)
            step *= 2
            yield
        lru["h"] = b_l + a_l * _rows_to_groups([h_ref[b] for b in range(nb)])

    lru_gen = lru_pieces()

    def lru_advance():
        next(lru_gen, None)

    blk_of = lambda x, w: x >> (w.bit_length() - 1)
    same = {w: blk_of(row_h, w) == blk_of(col_h, w) for w in (INV_BASE, 2 * INV_BASE, 4 * INV_BASE)}
    d_m = [jnp.where(same[INV_BASE], m, 0.0) for m in a_m]
    t_m = [eye_h + m for m in d_m]
    pw_sp = [_split2(_mm3(ps, _bd(ps))) for ps in [_split2(m) for m in d_m]]
    lru_advance()
    prod = [_mm3(_cat([ps, _split2(t)], 0), _bd(ps)) for ps, t in zip(pw_sp, t_m)]
    pw_sp = [_split2(p[0:CHUNK]) for p in prod]
    t_m = [t + p[CHUNK:2 * CHUNK] for t, p in zip(t_m, prod)]
    lru_advance()
    t_m = [t + _mm3(_split2(t), _bd(ps)) for t, ps in zip(t_m, pw_sp)]
    lru_advance()
    width = INV_BASE
    while width < CHUNK:
        if 2 * width < CHUNK:
            off = same[2 * width] & jnp.logical_not(same[width])
        else:
            off = jnp.logical_not(same[width])
        e_sp = [_split2(jnp.where(off, m, 0.0)) for m in a_m]
        t_sp = [_split2(t) for t in t_m]
        x_sp = [_split2(_mm3(e, _bd(ts))) for e, ts in zip(e_sp, t_sp)]
        t_m = [t + _mm3(ts, _bd(x)) for t, ts, x in zip(t_m, t_sp, x_sp)]
        lru_advance()
        width *= 2
    u_sp = [_split2(_mm3(_split2(t), _bd(_split2(w)))) for t, w in zip(t_m, w_m)]
    lru_advance()
    y_rw = [z[u][1] + jnp.dot(
        jnp.concatenate([p_hi[u], q_hi[u]], axis=1),
        jnp.concatenate([_blockdiag(u_sp[u][0]), _blockdiag(v_sp[0][gs[0], gs[1]])], axis=0),
        preferred_element_type=F32) for u, gs in enumerate(pos)]
    lru_advance()
    s_upd = []
    for u, (gs, (b, hp)) in enumerate(zip(pos, units)):
        dgam = jnp.where(eye_p, jnp.broadcast_to(gam[b][:, gs[1]], (PAIR, PAIR)), 0.0)
        s_new = _mm3(_cat([blk(bg_sp, gs), blk(kg_sp, gs), _split2(dgam)], 0),
                     _cat([u_sp[u], blk(v_sp, gs), s_sp[u]], 0), _TN)
        s_upd.append(jnp.where(same_head, s_new, 0.0))

    if not reverse:
        y_out = y_rw
    else:
        y_out = []
        for u, ((gr_, sl), (b, hp)) in enumerate(zip(pos, units)):
            ysum = y_rw[u] + f_in[gr_, sl]
            mean = _dot_ones_rhs(ysum, head_ones) * (1.0 / HEAD)
            cen = ysum - mean
            var = _dot_ones_rhs(cen * cen, head_ones) * (1.0 / HEAD)
            gn = cen * lax.rsqrt(var + GN_EPS) * vec(V_LNG, sl) + vec(V_LNB, sl)
            y_out.append((gn + (f_in[gr_, RW + PAIR * hp:RW + PAIR * (hp + 1)] + bonus[gr_, sl])) * g[gr_, sl])

    y_base = CONV_W if reverse else 0
    for u, (b, hp) in enumerate(units):
        s_ref[b, hp] = s_upd[u]
        out_ref[b, :, y_base + PAIR * hp:y_base + PAIR * (hp + 1)] = y_out[u]
    if not reverse:
        for b in range(nb):
            out_ref[b, :, RW:2 * RW] = bonus[grp[b]]

    for _ in lru_gen:
        pass
    h = lru["h"]
    for b in range(nb):
        hb = h[grp[b]]
        h_ref[b] = hb[0:1, :] if reverse else hb[CHUNK - 1:CHUNK, :]

    if not reverse:
        for b in range(nb):
            out_ref[b, :, 2 * RW:3 * RW] = h[grp[b]]
    else:
        gr = stacked(p_ref, OFF_C + RW, OFF_C + 2 * RW)
        y_lru = jax.nn.gelu(gr) * (f_in[:, 2 * RW:3 * RW] + h)
        bgate = stacked(p_ref, 0, CONV_W)
        uc = stacked(p_ref, CONV_W, 2 * CONV_W) * stacked(p_ref, 2 * CONV_W, 3 * CONV_W)
        if line_is_chunk:
            zero = jnp.zeros((nb * HALO, CONV_W), F32)
            up, un = zero, zero
        else:
            up = stacked(pp_ref, CONV_W, 2 * CONV_W) * stacked(pp_ref, 2 * CONV_W, 3 * CONV_W) * keep_prev
            un = stacked(pn_ref, CONV_W, 2 * CONV_W) * stacked(pn_ref, 2 * CONV_W, 3 * CONV_W) * keep_next
        yc = bgate * (conva_ref[0:1, :] * _shift_dn(uc, up, 1) + conva_ref[1:2, :] * uc
                      + conva_ref[2:3, :] * _shift_up(uc, un, 1))
        for b in range(nb):
            out_ref[b, :, CONV_W + RW:CONV_W + 2 * RW] = y_lru[grp[b]]
            out_ref[b, :, 0:CONV_W] = yc[grp[b]]


def _mixer_pass(p, f_in, s0, h0, prm, d, col_major, line_is_chunk):
    bsz = p.shape[0]
    t_len = p.shape[1] * p.shape[2] // P_COLS
    n = t_len // CHUNK
    reverse = d == 1
    out_cols = 4 * CONV_W if reverse else F_COLS

    def ci(i):
        return (n - 1 - i) if reverse else i

    if col_major:
        assert p.shape == (bsz, CHUNK, GRID_W * P_COLS) and n == GRID_W
        p_view = p
        cur_spec = pl.BlockSpec((MIX_NB, CHUNK,P_COLS), lambda b, i: (b, 0, ci(i)))
        prev_spec = pl.BlockSpec((MIX_NB, HALO,P_COLS),
                                 lambda b, i: (b, CHUNK // HALO - 1, jnp.maximum(ci(i) - 1, 0)))
        next_spec = pl.BlockSpec((MIX_NB, HALO,P_COLS), lambda b, i: (b, 0, jnp.minimum(ci(i) + 1, n - 1)))
    else:
        p_view = p
        per = CHUNK // HALO
        cur_spec = pl.BlockSpec((MIX_NB, CHUNK,P_COLS), lambda b, i: (b, ci(i), 0))
        prev_spec = pl.BlockSpec((MIX_NB, HALO,P_COLS), lambda b, i: (b, jnp.maximum(per * ci(i) - 1, 0), 0))
        next_spec = pl.BlockSpec((MIX_NB, HALO,P_COLS),
                                 lambda b, i: (b, jnp.minimum(per * ci(i) + per, per * n - 1), 0))
    full = lambda a: pl.BlockSpec(a.shape, lambda b, i: (0,) * a.ndim)
    in_specs = [cur_spec, prev_spec, next_spec]
    args = [p_view, p_view, p_view]
    if reverse:
        in_specs.append(pl.BlockSpec((MIX_NB, CHUNK,F_COLS), lambda b, i: (b, ci(i), 0)))
        args.append(f_in)
    state_specs = [pl.BlockSpec((MIX_NB,) + s0.shape[1:], lambda b, i: (b, 0, 0, 0)),
                   pl.BlockSpec((MIX_NB, 1, RW), lambda b, i: (b, 0, 0))]
    in_specs += state_specs
    args += [s0, h0]
    params = [prm["vec"][d], prm["mu"], prm["conv_a"], prm["w2"][d], prm["a2"][d], prm["g2"],
              prm["wr"][d], prm["wi"][d]]
    in_specs += [full(a) for a in params]
    args += params

    if reverse and col_major:
        out_shape = jax.ShapeDtypeStruct((bsz, CHUNK, n * out_cols), F32)
        out_spec = pl.BlockSpec((MIX_NB, CHUNK,out_cols), lambda b, i: (b, 0, ci(i)))
    else:
        out_shape = jax.ShapeDtypeStruct((bsz, t_len, out_cols), F32)
        out_spec = pl.BlockSpec((MIX_NB, CHUNK,out_cols), lambda b, i: (b, ci(i), 0))

    out, s_fin, h_fin = pl.pallas_call(
        functools.partial(_mixer_kernel, d=d, n_chunks=n, line_is_chunk=line_is_chunk),
        grid=(bsz // MIX_NB, n),
        in_specs=in_specs,
        out_specs=[out_spec] + state_specs,
        out_shape=[out_shape, jax.ShapeDtypeStruct(s0.shape, F32), jax.ShapeDtypeStruct(h0.shape, F32)],
        compiler_params=pltpu.CompilerParams(
            dimension_semantics=("parallel", "arbitrary"), vmem_limit_bytes=VMEM_LIMIT),
        name="mixer_bwd" if reverse else "mixer_fwd",
    )(*args)
    return out, s_fin, h_fin


def _mix_stream(p, prm, col_major, line_is_chunk, init):
    (s0f, h0f), (s0b, h0b) = init
    f, sf, hf = _mixer_pass(p, None, s0f, h0f, prm, 0, col_major, line_is_chunk)
    y, sb, hb = _mixer_pass(p, f, s0b, h0b, prm, 1, col_major, line_is_chunk)
    return y, ((sf, hf), (sb, hb))


def _mixer_params(l, conv_a, rwkv_mu, rwkv_w0, rwkv_w2, rwkv_a0, rwkv_a2, rwkv_g2, rwkv_kk, rwkv_ka,
                  rwkv_rk, rwkv_lnx_g, rwkv_lnx_b, lru_conv_w, lru_conv_b, lru_w_r, lru_b_r, lru_w_i,
                  lru_b_i, lru_lam):
    def both(a):
        return jnp.broadcast_to(a[None], (2,) + a.shape)

    rows = [rwkv_w0[l], rwkv_a0[l], both(rwkv_kk[l]), both(rwkv_ka[l]), both(rwkv_rk[l]),
            both(rwkv_lnx_g[l]), both(rwkv_lnx_b[l]),
            lru_conv_w[l][:, 0], lru_conv_w[l][:, 1], lru_conv_w[l][:, 2], lru_conv_w[l][:, 3],
            lru_conv_b[l], lru_b_r[l], lru_b_i[l], lru_lam[l], jnp.zeros((2, RW), F32)]
    assert len(rows) == VEC_ROWS
    vec = jnp.stack(rows, axis=1)
    nb = lru_w_r.shape[2]
    eye = jnp.eye(nb, dtype=F32)

    def bd(w):
        return jnp.einsum("dnij,nm->dnimj", w, eye).reshape(2, nb * HEAD, nb * HEAD).astype(BF16)

    return {
        "vec": vec, "mu": rwkv_mu[l], "conv_a": conv_a[l],
        "w2": rwkv_w2[l].astype(BF16), "a2": rwkv_a2[l].astype(BF16), "g2": rwkv_g2[l].astype(BF16),
        "wr": bd(lru_w_r[l]), "wi": bd(lru_w_i[l]),
    }


def kernel(x, c, ctx, c_ctx, w_mod, b_mod, g_ffn1, w_gu1, w_down1, g_mix, w_in, conv_a, rwkv_mu, rwkv_w0, rwkv_w2, rwkv_a0, rwkv_a2, rwkv_g2, rwkv_kk, rwkv_ka, rwkv_rk, rwkv_lnx_g, rwkv_lnx_b, lru_conv_w, lru_conv_b, lru_w_r, lru_b_r, lru_w_i, lru_b_i, lru_lam, w_out, g_ffn2, w_gu2, w_down2, g_final):
    bsz, t_len, d = x.shape
    depth = w_mod.shape[0]
    n_ctx = ctx.shape[1]
    assert w_in.shape[2] == P_COLS and t_len % CHUNK == 0 and n_ctx % CHUNK == 0

    mod_rows = -(-(bsz + 1) // 8) * 8
    cc = jnp.concatenate([c, c_ctx[None, :], jnp.zeros((mod_rows - bsz - 1, d), F32)], axis=0)
    mods = _modulation(cc, w_mod, b_mod)

    zero_state = ((jnp.zeros((bsz, RW // PAIR, PAIR, PAIR), F32), jnp.zeros((bsz, 1, RW), F32)),) * 2
    s_lat = x
    s_ctx = ctx.reshape(1, bsz * n_ctx, d)
    for l in range(depth):
        last = l == depth - 1
        m_lat = mods[l, :bsz].reshape(bsz, 9, d)
        m_ctx = mods[l, bsz:bsz + 1].reshape(1, 9, d)
        wgu1, wd1 = w_gu1[l].astype(BF16), w_down1[l].astype(BF16)
        wgu2, wd2 = w_gu2[l].astype(BF16), w_down2[l].astype(BF16)
        win, wout = w_in[l].astype(BF16), w_out[l].astype(BF16)
        prm = _mixer_params(l, conv_a, rwkv_mu, rwkv_w0, rwkv_w2, rwkv_a0, rwkv_a2, rwkv_g2, rwkv_kk,
                            rwkv_ka, rwkv_rk, rwkv_lnx_g, rwkv_lnx_b, lru_conv_w, lru_conv_b, lru_w_r,
                            lru_b_r, lru_w_i, lru_b_i, lru_lam)

        s_lat = _ffn(s_lat, m_lat, 0, g_ffn1[l], wgu1, wd1)
        s_ctx = _ffn(s_ctx, m_ctx, 0, g_ffn1[l], wgu1, wd1)
        col_major = l % 2 == 1
        p_lat = _inproj(s_lat, m_lat, 3, g_mix[l], win, col_major=col_major)
        p_ctx = _inproj(s_ctx, m_ctx, 3, g_mix[l], win)

        y_ctx, ctx_fin = _mix_stream(p_ctx.reshape(bsz, n_ctx, P_COLS), prm, False, False, zero_state)
        y_lat, _ = _mix_stream(p_lat, prm, col_major, True, ctx_fin)

        s_lat = _ffn(s_lat, m_lat, 6, g_ffn2[l], wgu2, wd2, y=y_lat, wout=wout, y_col_major=col_major,
                     gfin=g_final if last else None)
        if not last:
            s_ctx = _ffn(s_ctx, m_ctx, 6, g_ffn2[l], wgu2, wd2,
                         y=y_ctx.reshape(1, bsz * n_ctx, 4 * CONV_W), wout=wout)
    return s_lat
```

```python
import functools

import jax
import jax.numpy as jnp
from jax import lax
from jax.experimental import pallas as pl
from jax.experimental.pallas import tpu as pltpu

F32 = jnp.float32
BF16 = jnp.bfloat16

HEAD = 64
LANES = 128
PAIR = 2 * HEAD
CHUNK = 64
HALO = 8
INV_BASE = 8
MIX_NB = 4
GRID_W = 64
NORM_EPS = 1e-6
GN_EPS = 64e-5
LRU_C = 8.0
DECAY_SCALE = 0.6065306597126334
VMEM_LIMIT = 56 * 1024 * 1024


def _dot(a, b):
    return jnp.dot(a.astype(BF16), b.astype(BF16), preferred_element_type=F32)


def _split2(a):
    hi = a.astype(BF16)
    lo = (a - hi.astype(F32)).astype(BF16)
    return hi, lo


_NN = (((1,), (0,)), ((), ()))
_NT = (((1,), (1,)), ((), ()))
_TN = (((0,), (0,)), ((), ()))


def _dot3(a, b, dims=_NN):
    ah, al = _split2(a)
    bh, bl = _split2(b)
    d = functools.partial(lax.dot_general, dimension_numbers=dims, preferred_element_type=F32)
    return d(ah, bh) + (d(ah, bl) + d(al, bh))


def _mm3(a, b, dims=_NN):
    ah, al = a
    bh, bl = b
    d = functools.partial(lax.dot_general, dimension_numbers=dims, preferred_element_type=F32)
    free = 1 if dims == _TN else 0
    m = ah.shape[free]
    both = d(jnp.concatenate([ah, al], axis=free), bh)
    return both[:m] + (d(ah, bl) + both[m:])


def _mm3_and_1(a, c_hi, b, dims):
    ah, al = a
    bh, bl = b
    d = functools.partial(lax.dot_general, dimension_numbers=dims, preferred_element_type=F32)
    m = ah.shape[0]
    both = d(jnp.concatenate([ah, al, c_hi], axis=0), bh)
    return both[:m] + (d(ah, bl) + both[m:2 * m]), both[2 * m:]


def _cat(parts, axis):
    return tuple(jnp.concatenate([p[i] for p in parts], axis=axis) for i in range(2))


def _lanes(sp, sl):
    return tuple(x[:, sl] for x in sp)


def _bd(sp):
    return tuple(_blockdiag(x) for x in sp)


def _dot_ones_rhs(a, ones_rhs):
    ah, al = _split2(a)
    d = functools.partial(jnp.dot, preferred_element_type=F32)
    return d(ah, ones_rhs) + d(al, ones_rhs)


def _dot_exact_lhs(ones_lhs, b):
    b1 = b.astype(BF16)
    r1 = b - b1.astype(F32)
    b2 = r1.astype(BF16)
    b3 = (r1 - b2.astype(F32)).astype(BF16)
    d = functools.partial(jnp.dot, preferred_element_type=F32)
    return d(ones_lhs, b1) + (d(ones_lhs, b2) + d(ones_lhs, b3))


def _softplus(x):
    return jnp.maximum(x, 0.0) + jnp.log1p(jnp.exp(-jnp.abs(x)))


def _rms_modulate(s, g, shift, scale):
    y = s * lax.rsqrt(jnp.mean(s * s, axis=-1, keepdims=True) + NORM_EPS)
    return (y * g) * (1.0 + scale) + shift


def _per_group(fn, x, halo, *args):
    nb = x.shape[0] // CHUNK
    if nb == 1:
        return fn(x, halo, *args)
    return jnp.concatenate(
        [fn(x[b * CHUNK:(b + 1) * CHUNK], None if halo is None else halo[b * HALO:(b + 1) * HALO], *args)
         for b in range(nb)], axis=0)


def _shift_dn(x, prev, k):
    return _per_group(_shift_dn1, x, prev, k)


def _shift_up(x, nxt, k):
    return _per_group(_shift_up1, x, nxt, k)


def _scan_shift(x, k, fill, reverse):
    return _per_group(_scan_shift1, x, None, k, fill, reverse)


def _rows_to_groups(rows):
    return jnp.concatenate([jnp.broadcast_to(r, (CHUNK, r.shape[1])) for r in rows], axis=0)


def _shift_dn1(x, prev, k):
    ch = x.shape[1]
    rolled = pltpu.roll(x, k, axis=0)
    pr = pltpu.roll(prev, k, axis=0)
    r8 = lax.broadcasted_iota(jnp.int32, (HALO, ch), 0)
    head = jnp.where(r8 < k, pr, rolled[:HALO])
    return jnp.concatenate([head, rolled[HALO:]], axis=0)


def _shift_up1(x, nxt, k):
    n, ch = x.shape
    rolled = pltpu.roll(x, n - k, axis=0)
    nr = pltpu.roll(nxt, HALO - k, axis=0)
    r8 = lax.broadcasted_iota(jnp.int32, (HALO, ch), 0)
    tail = jnp.where(r8 >= HALO - k, nr, rolled[n - HALO:])
    return jnp.concatenate([rolled[:n - HALO], tail], axis=0)


def _scan_shift1(x, _, k, fill, reverse):
    n, ch = x.shape
    if k % HALO == 0:
        pad = jnp.full((k, ch), fill, x.dtype)
        return jnp.concatenate([x[k:], pad] if reverse else [pad, x[:n - k]], axis=0)
    rows = lax.broadcasted_iota(jnp.int32, (n, ch), 0)
    if not reverse:
        return jnp.where(rows >= k, pltpu.roll(x, k, axis=0), fill)
    return jnp.where(rows < n - k, pltpu.roll(x, n - k, axis=0), fill)


def _blockdiag(xp):
    n = xp.shape[0]
    lane = lax.broadcasted_iota(jnp.int32, (n, PAIR), 1)
    zero = jnp.zeros_like(xp)
    top = jnp.where(lane < HEAD, xp, zero)
    bot = jnp.where(lane >= HEAD, xp, zero)
    return jnp.concatenate([top, bot], axis=0)


def _mod_kernel(c_ref, w_ref, b_ref, o_ref):
    cc = c_ref[...]
    act = cc * jax.nn.sigmoid(cc)
    o_ref[0] = _dot(act, w_ref[0]) + b_ref[0]


def _modulation(cc, w_mod, b_mod):
    depth, d, n = w_mod.shape
    rows = cc.shape[0]
    tn = n // 8
    return pl.pallas_call(
        _mod_kernel,
        grid=(depth, n // tn),
        in_specs=[
            pl.BlockSpec((rows, d), lambda l, j: (0, 0)),
            pl.BlockSpec((1, d, tn), lambda l, j: (l, 0, j)),
            pl.BlockSpec((1, 1, tn), lambda l, j: (l, 0, j)),
        ],
        out_specs=pl.BlockSpec((1, rows, tn), lambda l, j: (l, 0, j)),
        out_shape=jax.ShapeDtypeStruct((depth, rows, n), F32),
        compiler_params=pltpu.CompilerParams(
            dimension_semantics=("parallel", "parallel"), vmem_limit_bytes=VMEM_LIMIT),
        name="modulation",
    )(cc, w_mod, b_mod.reshape(depth, 1, n))


def _ffn_kernel(*refs, mod_base, has_pre, y_col_major, final_norm, d_ff, tf):
    it = iter(refs)
    s_ref = next(it)
    if has_pre:
        y_ref = next(it)
        wout_ref = next(it)
        if y_col_major:
            perm_ref = next(it)
    mod_ref = next(it)
    g_ref = next(it)
    wgu_ref = next(it)
    wdown_ref = next(it)
    if final_norm:
        gfin_ref = next(it)
    o_ref = next(it)
    acc_ref = next(it)

    s = s_ref[0]
    if has_pre:
        gate_mix = mod_ref[0, mod_base - 1:mod_base, :]
        if y_col_major:
            mix = wout_ref.shape[0]
            yc = jnp.concatenate([y_ref[0, :, c * mix:(c + 1) * mix] for c in range(GRID_W)], axis=0)
            yb = jnp.dot(perm_ref[...], yc.astype(BF16), preferred_element_type=F32).astype(BF16)
        else:
            yb = y_ref[0].astype(BF16)
        s = s + gate_mix * jnp.dot(yb, wout_ref[...], preferred_element_type=F32)
    shift = mod_ref[0, mod_base:mod_base + 1, :]
    scale = mod_ref[0, mod_base + 1:mod_base + 2, :]
    gate = mod_ref[0, mod_base + 2:mod_base + 3, :]
    hb = _rms_modulate(s, g_ref[...], shift, scale).astype(BF16)
    for j in range(d_ff // tf):
        gt = jnp.dot(hb, wgu_ref[:, j * tf:(j + 1) * tf], preferred_element_type=F32)
        up = jnp.dot(hb, wgu_ref[:, d_ff + j * tf:d_ff + (j + 1) * tf], preferred_element_type=F32)
        act = ((gt * jax.nn.sigmoid(gt)) * up).astype(BF16)
        part = jnp.dot(act, wdown_ref[j * tf:(j + 1) * tf, :], preferred_element_type=F32)
        if j == 0:
            acc_ref[...] = part
        else:
            acc_ref[...] += part
    out = s + (0.5 * gate) * acc_ref[...]
    if final_norm:
        out = out * lax.rsqrt(jnp.mean(out * out, axis=-1, keepdims=True) + NORM_EPS) * gfin_ref[...]
    o_ref[0] = out


def _ffn(s, mod, mod_base, g, wgu, wdown, y=None, wout=None, y_col_major=False, gfin=None, tm=512):
    bv, tv, d = s.shape
    d_ff = wdown.shape[0]
    tf = 256
    has_pre = y is not None
    final_norm = gfin is not None
    const = lambda b, i: (0, 0)
    tile = lambda b, i: (b, i, 0)
    in_specs = [pl.BlockSpec((1, tm, d), tile)]
    args = [s]
    if has_pre:
        y_rows = tm // GRID_W if y_col_major else tm
        in_specs += [pl.BlockSpec((1, y_rows, y.shape[2]), tile), pl.BlockSpec(wout.shape, const)]
        args += [y, wout]
        if y_col_major:
            src = (jnp.arange(tm) % GRID_W) * y_rows + jnp.arange(tm) // GRID_W
            in_specs.append(pl.BlockSpec((tm, tm), const))
            args.append(jax.nn.one_hot(src, tm, dtype=BF16))
    in_specs += [
        pl.BlockSpec((1, mod.shape[1], d), lambda b, i: (b, 0, 0)),
        pl.BlockSpec((1, d), const),
        pl.BlockSpec(wgu.shape, const),
        pl.BlockSpec(wdown.shape, const),
    ]
    args += [mod, g.reshape(1, d), wgu, wdown]
    if final_norm:
        in_specs.append(pl.BlockSpec((1, d), const))
        args.append(gfin.reshape(1, d))
    body = functools.partial(_ffn_kernel, mod_base=mod_base, has_pre=has_pre, y_col_major=y_col_major,
                             final_norm=final_norm, d_ff=d_ff, tf=tf)
    return pl.pallas_call(
        body,
        grid=(bv, tv // tm),
        in_specs=in_specs,
        out_specs=pl.BlockSpec((1, tm, d), tile),
        out_shape=jax.ShapeDtypeStruct(s.shape, F32),
        scratch_shapes=[pltpu.VMEM((tm, d), F32)],
        compiler_params=pltpu.CompilerParams(
            dimension_semantics=("parallel", "parallel"), vmem_limit_bytes=VMEM_LIMIT),
        name="ffn",
    )(*args)


def _inproj_kernel(s_ref, mod_ref, g_ref, w_ref, *rest, mod_base, col_major):
    shift = mod_ref[0, mod_base:mod_base + 1, :]
    scale = mod_ref[0, mod_base + 1:mod_base + 2, :]
    hb = _rms_modulate(s_ref[0], g_ref[...], shift, scale).astype(BF16)
    if not col_major:
        o_ref, = rest
        o_ref[0] = jnp.dot(hb, w_ref[...], preferred_element_type=F32)
        return
    perm_ref, o_ref = rest
    hp = jnp.dot(perm_ref[...], hb, preferred_element_type=F32).astype(BF16)
    res = jnp.dot(hp, w_ref[...], preferred_element_type=F32)
    rows = res.shape[0] // GRID_W
    n = res.shape[1]
    for c in range(GRID_W):
        o_ref[0, :, c * n:(c + 1) * n] = res[c * rows:(c + 1) * rows, :]


def _inproj(s, mod, mod_base, g, w, col_major=False, tm=512):
    bv, tv, d = s.shape
    n = w.shape[1]
    const = lambda b, i: (0, 0)
    in_specs = [
        pl.BlockSpec((1, tm, d), lambda b, i: (b, i, 0)),
        pl.BlockSpec((1, mod.shape[1], d), lambda b, i: (b, 0, 0)),
        pl.BlockSpec((1, d), const),
        pl.BlockSpec(w.shape, const),
    ]
    args = [s, mod, g.reshape(1, d), w]
    if col_major:
        rows = tm // GRID_W
        out_spec = pl.BlockSpec((1, rows, GRID_W * n), lambda b, i: (b, i, 0))
        out_shape = jax.ShapeDtypeStruct((bv, tv // GRID_W, GRID_W * n), F32)
        src = (jnp.arange(tm) % rows) * GRID_W + jnp.arange(tm) // rows
        in_specs.append(pl.BlockSpec((tm, tm), const))
        args.append(jax.nn.one_hot(src, tm, dtype=BF16))
    else:
        out_spec = pl.BlockSpec((1, tm, n), lambda b, i: (b, i, 0))
        out_shape = jax.ShapeDtypeStruct((bv, tv, n), F32)
    return pl.pallas_call(
        functools.partial(_inproj_kernel, mod_base=mod_base, col_major=col_major),
        grid=(bv, tv // tm),
        in_specs=in_specs,
        out_specs=out_spec,
        out_shape=out_shape,
        compiler_params=pltpu.CompilerParams(
            dimension_semantics=("parallel", "parallel"), vmem_limit_bytes=VMEM_LIMIT),
        name="inproj",
    )(*args)


CONV_W = 256
RW = 384
OFF_B = 3 * CONV_W
RWKV_COLS = 3 * RW + 64 + 64 + 128
OFF_C = OFF_B + RWKV_COLS
P_COLS = OFF_C + 2 * RW
F_PB = 3 * RW
F_KK = F_PB + RWKV_COLS
F_COLS = F_KK + RW
V_W0, V_A0, V_KK, V_KA, V_RK, V_LNG, V_LNB, V_CW, V_CB, V_BR, V_BI, V_LAM = 0, 1, 2, 3, 4, 5, 6, 7, 11, 12, 13, 14
VEC_ROWS = 16


def _mixer_kernel(*refs, d, n_chunks, line_is_chunk):
    it = iter(refs)
    p_ref, pp_ref, pn_ref = next(it), next(it), next(it)
    if d == 1:
        f_in_ref = next(it)
    s0_ref, h0_ref = next(it), next(it)
    vec_ref, mu_ref, conva_ref = next(it), next(it), next(it)
    w2_ref, a2_ref, g2_ref, wr_ref, wi_ref = next(it), next(it), next(it), next(it), next(it)
    out_ref, s_ref, h_ref = next(it), next(it), next(it)

    reverse = d == 1
    i = pl.program_id(1)
    ci = (n_chunks - 1 - i) if reverse else i
    keep_prev = jnp.where(ci == 0, 0.0, 1.0)
    keep_next = jnp.where(ci == n_chunks - 1, 0.0, 1.0)

    @pl.when(i == 0)
    def _():
        s_ref[...] = s0_ref[...]
        h_ref[...] = h0_ref[...]

    def vec(r, sl=slice(None)):
        return vec_ref[r:r + 1, sl]

    nb = p_ref.shape[0]
    grp = [slice(b * CHUNK, (b + 1) * CHUNK) for b in range(nb)]

    def stacked(ref, c0, c1):
        return jnp.concatenate([ref[b, :, c0:c1] for b in range(nb)], axis=0)

    if reverse:
        f_in = stacked(f_in_ref, 0, F_COLS)
        pb = f_in[:, F_PB:F_PB + RWKV_COLS]
    else:
        cur = stacked(p_ref, OFF_B, OFF_C)
        prev = stacked(pp_ref, OFF_B, OFF_C) * keep_prev
        nxt = stacked(pn_ref, OFF_B, OFF_C) * keep_next
        pb = (cur + mu_ref[0:1, :] * (_shift_dn(cur, prev, 1) - cur)
              + mu_ref[1:2, :] * (_shift_up(cur, nxt, 1) - cur))
    r = pb[:, 0:RW]
    k = pb[:, RW:2 * RW]
    v = pb[:, 2 * RW:3 * RW]
    dw = pb[:, 3 * RW:3 * RW + 64]
    da = pb[:, 3 * RW + 64:3 * RW + 128]
    dg = pb[:, 3 * RW + 128:3 * RW + 256]

    lane_p = lax.broadcasted_iota(jnp.int32, (PAIR, PAIR), 1)
    row_p = lax.broadcasted_iota(jnp.int32, (PAIR, PAIR), 0)
    same_head = (lane_p >= HEAD) == (row_p >= HEAD)
    head_ones = jnp.where(same_head, 1.0, 0.0).astype(BF16)

    def head_sum(x):
        return jnp.concatenate(
            [_dot_ones_rhs(x[:, PAIR * hp:PAIR * (hp + 1)], head_ones) for hp in range(RW // PAIR)],
            axis=1)

    if reverse:
        kk = f_in[:, F_KK:F_KK + RW]
    else:
        kkr = k * vec(V_KK)
        kk = kkr * lax.rsqrt(head_sum(kkr * kkr) + 1e-12)
        for b in range(nb):
            out_ref[b, :, F_PB:F_PB + RWKV_COLS] = pb[grp[b]]
            out_ref[b, :, F_KK:F_KK + RW] = kk[grp[b]]
    lw = -DECAY_SCALE * jax.nn.sigmoid(vec(V_W0) + _dot(jnp.tanh(dw), w2_ref[...]))
    a = jax.nn.sigmoid(vec(V_A0) + _dot(da, a2_ref[...]))
    kd = k * (1.0 + (a - 1.0) * vec(V_KA))
    beta = kk * a
    bonus = head_sum(r * kd * vec(V_RK)) * v

    row_c = lax.broadcasted_iota(jnp.int32, (CHUNK, CHUNK), 0)
    col_c = lax.broadcasted_iota(jnp.int32, (CHUNK, CHUNK), 1)
    tri = jnp.where((col_c >= row_c) if reverse else (col_c <= row_c), 1.0, 0.0).astype(BF16)
    c_inc = jnp.concatenate([_dot_exact_lhs(tri, lw[g]) for g in grp], axis=0)
    c_exc = c_inc - lw
    c_tot = [c_inc[g.start:g.start + 1, :] if reverse else c_inc[g.stop - 1:g.stop, :] for g in grp]
    e_neg = jnp.exp(-c_inc)
    gam = [jnp.exp(t) for t in c_tot]
    gam_rows = _rows_to_groups(gam)
    at = -(kk * jnp.exp(c_exc))
    rt = r * jnp.exp(c_inc)
    bh = beta * e_neg
    kh = kd * e_neg
    bg = bh * gam_rows
    kg = kh * gam_rows

    row_h = lax.broadcasted_iota(jnp.int32, (CHUNK, PAIR), 0)
    col_h = lax.broadcasted_iota(jnp.int32, (CHUNK, PAIR), 1) & (HEAD - 1)
    strict = (col_h > row_h) if reverse else (col_h < row_h)
    incl = (col_h >= row_h) if reverse else (col_h <= row_h)
    eye_h = jnp.where(col_h == row_h, 1.0, 0.0)
    eye_p = lane_p == row_p

    if reverse:
        g = _dot(jax.nn.sigmoid(dg), g2_ref[...])

    n_pair = RW // PAIR
    units = [(b, hp) for b in range(nb) for hp in range(n_pair)]
    pos = [(grp[b], slice(PAIR * hp, PAIR * (hp + 1))) for b, hp in units]

    def blk(sp, gs):
        return tuple(x[gs[0], gs[1]] for x in sp)

    s_sp = [_split2(s_ref[b, hp]) for b, hp in units]
    at_sp, bh_sp, kh_sp = _split2(at), _split2(bh), _split2(kh)
    bg_sp, kg_sp, v_sp = _split2(bg), _split2(kg), _split2(v)
    rt_hi = rt.astype(BF16)
    gm = [_mm3_and_1(blk(at_sp, gs), rt_hi[gs[0], gs[1]],
                     _cat([_bd(blk(bh_sp, gs)), _bd(blk(kh_sp, gs))], 0), _NT) for gs in pos]
    a_m = [jnp.where(strict, m[0][:, 0:PAIR], 0.0) for m in gm]
    b_m = [jnp.where(strict, m[0][:, PAIR:2 * PAIR], 0.0) for m in gm]
    p_hi = [jnp.where(incl, m[1][:, 0:PAIR], 0.0).astype(BF16) for m in gm]
    q_hi = [jnp.where(incl, m[1][:, PAIR:2 * PAIR], 0.0).astype(BF16) for m in gm]
    z = [_mm3_and_1(blk(at_sp, gs), rt_hi[gs[0], gs[1]], s_sp[u], _NN) for u, gs in enumerate(pos)]
    w_m = [z[u][0] + _mm3(_split2(b_m[u]), _bd(blk(v_sp, gs))) for u, gs in enumerate(pos)]

    lru = {}

    def lru_pieces():
        xr = stacked(p_ref, OFF_C, OFF_C + RW)
        if not reverse:
            xh = stacked(pp_ref, OFF_C, OFF_C + RW) * keep_prev
            u_l = (vec(V_CW + 3) * xr + vec(V_CW + 2) * _shift_dn(xr, xh, 1)
                   + vec(V_CW + 1) * _shift_dn(xr, xh, 2) + vec(V_CW) * _shift_dn(xr, xh, 3))
        else:
            xh = stacked(pn_ref, OFF_C, OFF_C + RW) * keep_next
            u_l = (vec(V_CW) * xr + vec(V_CW + 1) * _shift_up(xr, xh, 1)
                   + vec(V_CW + 2) * _shift_up(xr, xh, 2) + vec(V_CW + 3) * _shift_up(xr, xh, 3))
        u_l = u_l + vec(V_CB)
        yield
        rg = jax.nn.sigmoid(_dot(u_l, wr_ref[...]) + vec(V_BR))
        yield
        ig = jax.nn.sigmoid(_dot(u_l, wi_ref[...]) + vec(V_BI))
        yield
        log_a = -LRU_C * (_softplus(-vec(V_LAM)) * rg)
        a_l = jnp.exp(log_a)
        th = jnp.tanh(log_a)
        b_l = jnp.sqrt(-2.0 * th / (1.0 - th)) * (ig * u_l)
        yield
        step = 1
        while step < CHUNK:
            b_l = b_l + a_l * _scan_shift(b_l, step, 0.0, reverse)
            a_l = a_l * _scan_shift(a_l, step, 1.0, reverse)
            step *= 2
            yield
        lru["h"] = b_l + a_l * _rows_to_groups([h_ref[b] for b in range(nb)])

    lru_gen = lru_pieces()

    def lru_advance():
        next(lru_gen, None)

    blk_of = lambda x, w: x >> (w.bit_length() - 1)
    same = {w: blk_of(row_h, w) == blk_of(col_h, w) for w in (INV_BASE, 2 * INV_BASE, 4 * INV_BASE)}
    d_m = [jnp.where(same[INV_BASE], m, 0.0) for m in a_m]
    t_m = [eye_h + m for m in d_m]
    pw_sp = [_split2(_mm3(ps, _bd(ps))) for ps in [_split2(m) for m in d_m]]
    lru_advance()
    prod = [_mm3(_cat([ps, _split2(t)], 0), _bd(ps)) for ps, t in zip(pw_sp, t_m)]
    pw_sp = [_split2(p[0:CHUNK]) for p in prod]
    t_m = [t + p[CHUNK:2 * CHUNK] for t, p in zip(t_m, prod)]
    lru_advance()
    t_m = [t + _mm3(_split2(t), _bd(ps)) for t, ps in zip(t_m, pw_sp)]
    lru_advance()
    width = INV_BASE
    while width < CHUNK:
        if 2 * width < CHUNK:
            off = same[2 * width] & jnp.logical_not(same[width])
        else:
            off = jnp.logical_not(same[width])
        e_sp = [_split2(jnp.where(off, m, 0.0)) for m in a_m]
        t_sp = [_split2(t) for t in t_m]
        x_sp = [_split2(_mm3(e, _bd(ts))) for e, ts in zip(e_sp, t_sp)]
        t_m = [t + _mm3(ts, _bd(x)) for t, ts, x in zip(t_m, t_sp, x_sp)]
        lru_advance()
        width *= 2
    u_sp = [_split2(_mm3(_split2(t), _bd(_split2(w)))) for t, w in zip(t_m, w_m)]
    lru_advance()
    y_rw = [z[u][1] + jnp.dot(
        jnp.concatenate([p_hi[u], q_hi[u]], axis=1),
        jnp.concatenate([_blockdiag(u_sp[u][0]), _blockdiag(v_sp[0][gs[0], gs[1]])], axis=0),
        preferred_element_type=F32) for u, gs in enumerate(pos)]
    lru_advance()
    s_upd = []
    for u, (gs, (b, hp)) in enumerate(zip(pos, units)):
        dgam = jnp.where(eye_p, jnp.broadcast_to(gam[b][:, gs[1]], (PAIR, PAIR)), 0.0)
        s_new = _mm3(_cat([blk(bg_sp, gs), blk(kg_sp, gs), _split2(dgam)], 0),
                     _cat([u_sp[u], blk(v_sp, gs), s_sp[u]], 0), _TN)
        s_upd.append(jnp.where(same_head, s_new, 0.0))

    if not reverse:
        y_out = y_rw
    else:
        y_out = []
        for u, ((gr_, sl), (b, hp)) in enumerate(zip(pos, units)):
            ysum = y_rw[u] + f_in[gr_, sl]
            mean = _dot_ones_rhs(ysum, head_ones) * (1.0 / HEAD)
            cen = ysum - mean
            var = _dot_ones_rhs(cen * cen, head_ones) * (1.0 / HEAD)
            gn = cen * lax.rsqrt(var + GN_EPS) * vec(V_LNG, sl) + vec(V_LNB, sl)
            y_out.append((gn + (f_in[gr_, RW + PAIR * hp:RW + PAIR * (hp + 1)] + bonus[gr_, sl])) * g[gr_, sl])

    y_base = CONV_W if reverse else 0
    for u, (b, hp) in enumerate(units):
        s_ref[b, hp] = s_upd[u]
        out_ref[b, :, y_base + PAIR * hp:y_base + PAIR * (hp + 1)] = y_out[u]
    if not reverse:
        for b in range(nb):
            out_ref[b, :, RW:2 * RW] = bonus[grp[b]]

    for _ in lru_gen:
        pass
    h = lru["h"]
    for b in range(nb):
        hb = h[grp[b]]
        h_ref[b] = hb[0:1, :] if reverse else hb[CHUNK - 1:CHUNK, :]

    if not reverse:
        for b in range(nb):
            out_ref[b, :, 2 * RW:3 * RW] = h[grp[b]]
    else:
        gr = stacked(p_ref, OFF_C + RW, OFF_C + 2 * RW)
        y_lru = jax.nn.gelu(gr) * (f_in[:, 2 * RW:3 * RW] + h)
        bgate = stacked(p_ref, 0, CONV_W)
        uc = stacked(p_ref, CONV_W, 2 * CONV_W) * stacked(p_ref, 2 * CONV_W, 3 * CONV_W)
        if line_is_chunk:
            zero = jnp.zeros((nb * HALO, CONV_W), F32)
            up, un = zero, zero
        else:
            up = stacked(pp_ref, CONV_W, 2 * CONV_W) * stacked(pp_ref, 2 * CONV_W, 3 * CONV_W) * keep_prev
            un = stacked(pn_ref, CONV_W, 2 * CONV_W) * stacked(pn_ref, 2 * CONV_W, 3 * CONV_W) * keep_next
        yc = bgate * (conva_ref[0:1, :] * _shift_dn(uc, up, 1) + conva_ref[1:2, :] * uc
                      + conva_ref[2:3, :] * _shift_up(uc, un, 1))
        for b in range(nb):
            out_ref[b, :, CONV_W + RW:CONV_W + 2 * RW] = y_lru[grp[b]]
            out_ref[b, :, 0:CONV_W] = yc[grp[b]]


def _mixer_pass(p, f_in, s0, h0, prm, d, col_major, line_is_chunk):
    bsz = p.shape[0]
    t_len = p.shape[1] * p.shape[2] // P_COLS
    n = t_len // CHUNK
    reverse = d == 1
    out_cols = 4 * CONV_W if reverse else F_COLS

    def ci(i):
        return (n - 1 - i) if reverse else i

    if col_major:
        assert p.shape == (bsz, CHUNK, GRID_W * P_COLS) and n == GRID_W
        p_view = p
        cur_spec = pl.BlockSpec((MIX_NB, CHUNK,P_COLS), lambda b, i: (b, 0, ci(i)))
        prev_spec = pl.BlockSpec((MIX_NB, HALO,P_COLS),
                                 lambda b, i: (b, CHUNK // HALO - 1, jnp.maximum(ci(i) - 1, 0)))
        next_spec = pl.BlockSpec((MIX_NB, HALO,P_COLS), lambda b, i: (b, 0, jnp.minimum(ci(i) + 1, n - 1)))
    else:
        p_view = p
        per = CHUNK // HALO
        cur_spec = pl.BlockSpec((MIX_NB, CHUNK,P_COLS), lambda b, i: (b, ci(i), 0))
        prev_spec = pl.BlockSpec((MIX_NB, HALO,P_COLS), lambda b, i: (b, jnp.maximum(per * ci(i) - 1, 0), 0))
        next_spec = pl.BlockSpec((MIX_NB, HALO,P_COLS),
                                 lambda b, i: (b, jnp.minimum(per * ci(i) + per, per * n - 1), 0))
    full = lambda a: pl.BlockSpec(a.shape, lambda b, i: (0,) * a.ndim)
    in_specs = [cur_spec, prev_spec, next_spec]
    args = [p_view, p_view, p_view]
    if reverse:
        in_specs.append(pl.BlockSpec((MIX_NB, CHUNK,F_COLS), lambda b, i: (b, ci(i), 0)))
        args.append(f_in)
    state_specs = [pl.BlockSpec((MIX_NB,) + s0.shape[1:], lambda b, i: (b, 0, 0, 0)),
                   pl.BlockSpec((MIX_NB, 1, RW), lambda b, i: (b, 0, 0))]
    in_specs += state_specs
    args += [s0, h0]
    params = [prm["vec"][d], prm["mu"], prm["conv_a"], prm["w2"][d], prm["a2"][d], prm["g2"],
              prm["wr"][d], prm["wi"][d]]
    in_specs += [full(a) for a in params]
    args += params

    if reverse and col_major:
        out_shape = jax.ShapeDtypeStruct((bsz, CHUNK, n * out_cols), F32)
        out_spec = pl.BlockSpec((MIX_NB, CHUNK,out_cols), lambda b, i: (b, 0, ci(i)))
    else:
        out_shape = jax.ShapeDtypeStruct((bsz, t_len, out_cols), F32)
        out_spec = pl.BlockSpec((MIX_NB, CHUNK,out_cols), lambda b, i: (b, ci(i), 0))

    out, s_fin, h_fin = pl.pallas_call(
        functools.partial(_mixer_kernel, d=d, n_chunks=n, line_is_chunk=line_is_chunk),
        grid=(bsz // MIX_NB, n),
        in_specs=in_specs,
        out_specs=[out_spec] + state_specs,
        out_shape=[out_shape, jax.ShapeDtypeStruct(s0.shape, F32), jax.ShapeDtypeStruct(h0.shape, F32)],
        compiler_params=pltpu.CompilerParams(
            dimension_semantics=("parallel", "arbitrary"), vmem_limit_bytes=VMEM_LIMIT),
        name="mixer_bwd" if reverse else "mixer_fwd",
    )(*args)
    return out, s_fin, h_fin


def _mix_stream(p, prm, col_major, line_is_chunk, init):
    (s0f, h0f), (s0b, h0b) = init
    f, sf, hf = _mixer_pass(p, None, s0f, h0f, prm, 0, col_major, line_is_chunk)
    y, sb, hb = _mixer_pass(p, f, s0b, h0b, prm, 1, col_major, line_is_chunk)
    return y, ((sf, hf), (sb, hb))


def _mixer_params(l, conv_a, rwkv_mu, rwkv_w0, rwkv_w2, rwkv_a0, rwkv_a2, rwkv_g2, rwkv_kk, rwkv_ka,
                  rwkv_rk, rwkv_lnx_g, rwkv_lnx_b, lru_conv_w, lru_conv_b, lru_w_r, lru_b_r, lru_w_i,
                  lru_b_i, lru_lam):
    def both(a):
        return jnp.broadcast_to(a[None], (2,) + a.shape)

    rows = [rwkv_w0[l], rwkv_a0[l], both(rwkv_kk[l]), both(rwkv_ka[l]), both(rwkv_rk[l]),
            both(rwkv_lnx_g[l]), both(rwkv_lnx_b[l]),
            lru_conv_w[l][:, 0], lru_conv_w[l][:, 1], lru_conv_w[l][:, 2], lru_conv_w[l][:, 3],
            lru_conv_b[l], lru_b_r[l], lru_b_i[l], lru_lam[l], jnp.zeros((2, RW), F32)]
    assert len(rows) == VEC_ROWS
    vec = jnp.stack(rows, axis=1)
    nb = lru_w_r.shape[2]
    eye = jnp.eye(nb, dtype=F32)

    def bd(w):
        return jnp.einsum("dnij,nm->dnimj", w, eye).reshape(2, nb * HEAD, nb * HEAD).astype(BF16)

    return {
        "vec": vec, "mu": rwkv_mu[l], "conv_a": conv_a[l],
        "w2": rwkv_w2[l].astype(BF16), "a2": rwkv_a2[l].astype(BF16), "g2": rwkv_g2[l].astype(BF16),
        "wr": bd(lru_w_r[l]), "wi": bd(lru_w_i[l]),
    }


def kernel(x, c, ctx, c_ctx, w_mod, b_mod, g_ffn1, w_gu1, w_down1, g_mix, w_in, conv_a, rwkv_mu, rwkv_w0, rwkv_w2, rwkv_a0, rwkv_a2, rwkv_g2, rwkv_kk, rwkv_ka, rwkv_rk, rwkv_lnx_g, rwkv_lnx_b, lru_conv_w, lru_conv_b, lru_w_r, lru_b_r, lru_w_i, lru_b_i, lru_lam, w_out, g_ffn2, w_gu2, w_down2, g_final):
    bsz, t_len, d = x.shape
    depth = w_mod.shape[0]
    n_ctx = ctx.shape[1]
    assert w_in.shape[2] == P_COLS and t_len % CHUNK == 0 and n_ctx % CHUNK == 0

    mod_rows = -(-(bsz + 1) // 8) * 8
    cc = jnp.concatenate([c, c_ctx[None, :], jnp.zeros((mod_rows - bsz - 1, d), F32)], axis=0)
    mods = _modulation(cc, w_mod, b_mod)

    zero_state = ((jnp.zeros((bsz, RW // PAIR, PAIR, PAIR), F32), jnp.zeros((bsz, 1, RW), F32)),) * 2
    s_lat = x
    s_ctx = ctx.reshape(1, bsz * n_ctx, d)
    for l in range(depth):
        last = l == depth - 1
        m_lat = mods[l, :bsz].reshape(bsz, 9, d)
        m_ctx = mods[l, bsz:bsz + 1].reshape(1, 9, d)
        wgu1, wd1 = w_gu1[l].astype(BF16), w_down1[l].astype(BF16)
        wgu2, wd2 = w_gu2[l].astype(BF16), w_down2[l].astype(BF16)
        win, wout = w_in[l].astype(BF16), w_out[l].astype(BF16)
        prm = _mixer_params(l, conv_a, rwkv_mu, rwkv_w0, rwkv_w2, rwkv_a0, rwkv_a2, rwkv_g2, rwkv_kk,
                            rwkv_ka, rwkv_rk, rwkv_lnx_g, rwkv_lnx_b, lru_conv_w, lru_conv_b, lru_w_r,
                            lru_b_r, lru_w_i, lru_b_i, lru_lam)

        s_lat = _ffn(s_lat, m_lat, 0, g_ffn1[l], wgu1, wd1)
        s_ctx = _ffn(s_ctx, m_ctx, 0, g_ffn1[l], wgu1, wd1)
        col_major = l % 2 == 1
        p_lat = _inproj(s_lat, m_lat, 3, g_mix[l], win, col_major=col_major)
        p_ctx = _inproj(s_ctx, m_ctx, 3, g_mix[l], win)

        y_ctx, ctx_fin = _mix_stream(p_ctx.reshape(bsz, n_ctx, P_COLS), prm, False, False, zero_state)
        y_lat, _ = _mix_stream(p_lat, prm, col_major, True, ctx_fin)

        s_lat = _ffn(s_lat, m_lat, 6, g_ffn2[l], wgu2, wd2, y=y_lat, wout=wout, y_col_major=col_major,
                     gfin=g_final if last else None)
        if not last:
            s_ctx = _ffn(s_ctx, m_ctx, 6, g_ffn2[l], wgu2, wd2,
                         y=y_ctx.reshape(1, bsz * n_ctx, 4 * CONV_W), wout=wout)
    return s_lat
```

```python
import functools

import jax
import jax.numpy as jnp
from jax import lax
from jax.experimental import pallas as pl
from jax.experimental.pallas import tpu as pltpu

F32 = jnp.float32
BF16 = jnp.bfloat16

HEAD = 64
LANES = 128
PAIR = 2 * HEAD
CHUNK = 64
HALO = 8
INV_BASE = 8
MIX_NB = 8
GRID_W = 64
NORM_EPS = 1e-6
GN_EPS = 64e-5
LRU_C = 8.0
DECAY_SCALE = 0.6065306597126334
VMEM_LIMIT = 56 * 1024 * 1024


def _dot(a, b):
    return jnp.dot(a.astype(BF16), b.astype(BF16), preferred_element_type=F32)


def _split2(a):
    hi = a.astype(BF16)
    lo = (a - hi.astype(F32)).astype(BF16)
    return hi, lo


_NN = (((1,), (0,)), ((), ()))
_NT = (((1,), (1,)), ((), ()))
_TN = (((0,), (0,)), ((), ()))


def _mm3(a, b, dims=_NN):
    ah, al = a
    bh, bl = b
    d = functools.partial(lax.dot_general, dimension_numbers=dims, preferred_element_type=F32)
    free = 1 if dims == _TN else 0
    m = ah.shape[free]
    both = d(jnp.concatenate([ah, al], axis=free), bh)
    return both[:m] + (d(ah, bl) + both[m:])


def _mm3_and_1(a, c_hi, b, dims):
    ah, al = a
    bh, bl = b
    d = functools.partial(lax.dot_general, dimension_numbers=dims, preferred_element_type=F32)
    m = ah.shape[0]
    both = d(jnp.concatenate([ah, al, c_hi], axis=0), bh)
    return both[:m] + (d(ah, bl) + both[m:2 * m]), both[2 * m:]


def _cat(parts, axis):
    return tuple(jnp.concatenate([p[i] for p in parts], axis=axis) for i in range(2))


def _bd(sp):
    return tuple(_blockdiag(x) for x in sp)


def _dot_ones_rhs(a, ones_rhs):
    ah, al = _split2(a)
    d = functools.partial(jnp.dot, preferred_element_type=F32)
    return d(ah, ones_rhs) + d(al, ones_rhs)


def _dot_exact_lhs(ones_lhs, b):
    b1 = b.astype(BF16)
    r1 = b - b1.astype(F32)
    b2 = r1.astype(BF16)
    b3 = (r1 - b2.astype(F32)).astype(BF16)
    d = functools.partial(jnp.dot, preferred_element_type=F32)
    return d(ones_lhs, b1) + (d(ones_lhs, b2) + d(ones_lhs, b3))


def _softplus(x):
    return jnp.maximum(x, 0.0) + jnp.log1p(jnp.exp(-jnp.abs(x)))


def _rms_modulate(s, g, shift, scale):
    y = s * lax.rsqrt(jnp.mean(s * s, axis=-1, keepdims=True) + NORM_EPS)
    return (y * g) * (1.0 + scale) + shift


def _per_group(fn, x, halo, *args):
    nb = x.shape[0] // CHUNK
    if nb == 1:
        return fn(x, halo, *args)
    return jnp.concatenate(
        [fn(x[b * CHUNK:(b + 1) * CHUNK], None if halo is None else halo[b * HALO:(b + 1) * HALO], *args)
         for b in range(nb)], axis=0)


def _shift_dn(x, prev, k):
    return _per_group(_shift_dn1, x, prev, k)


def _shift_up(x, nxt, k):
    return _per_group(_shift_up1, x, nxt, k)


def _scan_shift(x, k, fill, reverse):
    return _per_group(_scan_shift1, x, None, k, fill, reverse)


def _rows_to_groups(rows):
    return jnp.concatenate([jnp.broadcast_to(r, (CHUNK, r.shape[1])) for r in rows], axis=0)


def _shift_dn1(x, prev, k):
    ch = x.shape[1]
    rolled = pltpu.roll(x, k, axis=0)
    pr = pltpu.roll(prev, k, axis=0)
    r8 = lax.broadcasted_iota(jnp.int32, (HALO, ch), 0)
    head = jnp.where(r8 < k, pr, rolled[:HALO])
    return jnp.concatenate([head, rolled[HALO:]], axis=0)


def _shift_up1(x, nxt, k):
    n, ch = x.shape
    rolled = pltpu.roll(x, n - k, axis=0)
    nr = pltpu.roll(nxt, HALO - k, axis=0)
    r8 = lax.broadcasted_iota(jnp.int32, (HALO, ch), 0)
    tail = jnp.where(r8 >= HALO - k, nr, rolled[n - HALO:])
    return jnp.concatenate([rolled[:n - HALO], tail], axis=0)


def _scan_shift1(x, _, k, fill, reverse):
    n, ch = x.shape
    if k % HALO == 0:
        pad = jnp.full((k, ch), fill, x.dtype)
        return jnp.concatenate([x[k:], pad] if reverse else [pad, x[:n - k]], axis=0)
    rows = lax.broadcasted_iota(jnp.int32, (n, ch), 0)
    if not reverse:
        return jnp.where(rows >= k, pltpu.roll(x, k, axis=0), fill)
    return jnp.where(rows < n - k, pltpu.roll(x, n - k, axis=0), fill)


def _blockdiag(xp):
    n = xp.shape[0]
    lane = lax.broadcasted_iota(jnp.int32, (n, PAIR), 1)
    zero = jnp.zeros_like(xp)
    top = jnp.where(lane < HEAD, xp, zero)
    bot = jnp.where(lane >= HEAD, xp, zero)
    return jnp.concatenate([top, bot], axis=0)


def _mod_kernel(c_ref, w_ref, b_ref, o_ref):
    cc = c_ref[...]
    act = cc * jax.nn.sigmoid(cc)
    o_ref[0] = _dot(act, w_ref[0]) + b_ref[0]


def _modulation(cc, w_mod, b_mod):
    depth, d, n = w_mod.shape
    rows = cc.shape[0]
    tn = n // 8
    return pl.pallas_call(
        _mod_kernel,
        grid=(depth, n // tn),
        in_specs=[
            pl.BlockSpec((rows, d), lambda l, j: (0, 0)),
            pl.BlockSpec((1, d, tn), lambda l, j: (l, 0, j)),
            pl.BlockSpec((1, 1, tn), lambda l, j: (l, 0, j)),
        ],
        out_specs=pl.BlockSpec((1, rows, tn), lambda l, j: (l, 0, j)),
        out_shape=jax.ShapeDtypeStruct((depth, rows, n), F32),
        compiler_params=pltpu.CompilerParams(
            dimension_semantics=("parallel", "parallel"), vmem_limit_bytes=VMEM_LIMIT),
        name="modulation",
    )(cc, w_mod, b_mod.reshape(depth, 1, n))


def _ffn_kernel(*refs, mod_base, has_pre, y_col_major, final_norm, d_ff, tf):
    it = iter(refs)
    s_ref = next(it)
    if has_pre:
        y_ref = next(it)
        wout_ref = next(it)
        if y_col_major:
            perm_ref = next(it)
    mod_ref = next(it)
    g_ref = next(it)
    wgu_ref = next(it)
    wdown_ref = next(it)
    if final_norm:
        gfin_ref = next(it)
    o_ref = next(it)
    acc_ref = next(it)

    s = s_ref[0]
    if has_pre:
        gate_mix = mod_ref[0, mod_base - 1:mod_base, :]
        if y_col_major:
            mix = wout_ref.shape[0]
            yc = jnp.concatenate([y_ref[0, :, c * mix:(c + 1) * mix] for c in range(GRID_W)], axis=0)
            yb = jnp.dot(perm_ref[...], yc.astype(BF16), preferred_element_type=F32).astype(BF16)
        else:
            yb = y_ref[0].astype(BF16)
        s = s + gate_mix * jnp.dot(yb, wout_ref[...], preferred_element_type=F32)
    shift = mod_ref[0, mod_base:mod_base + 1, :]
    scale = mod_ref[0, mod_base + 1:mod_base + 2, :]
    gate = mod_ref[0, mod_base + 2:mod_base + 3, :]
    hb = _rms_modulate(s, g_ref[...], shift, scale).astype(BF16)
    for j in range(d_ff // tf):
        gt = jnp.dot(hb, wgu_ref[:, j * tf:(j + 1) * tf], preferred_element_type=F32)
        up = jnp.dot(hb, wgu_ref[:, d_ff + j * tf:d_ff + (j + 1) * tf], preferred_element_type=F32)
        act = ((gt * jax.nn.sigmoid(gt)) * up).astype(BF16)
        part = jnp.dot(act, wdown_ref[j * tf:(j + 1) * tf, :], preferred_element_type=F32)
        if j == 0:
            acc_ref[...] = part
        else:
            acc_ref[...] += part
    out = s + (0.5 * gate) * acc_ref[...]
    if final_norm:
        out = out * lax.rsqrt(jnp.mean(out * out, axis=-1, keepdims=True) + NORM_EPS) * gfin_ref[...]
    o_ref[0] = out


def _ffn(s, mod, mod_base, g, wgu, wdown, y=None, wout=None, y_col_major=False, gfin=None, tm=512):
    bv, tv, d = s.shape
    d_ff = wdown.shape[0]
    tf = 256
    has_pre = y is not None
    final_norm = gfin is not None
    const = lambda b, i: (0, 0)
    tile = lambda b, i: (b, i, 0)
    in_specs = [pl.BlockSpec((1, tm, d), tile)]
    args = [s]
    if has_pre:
        y_rows = tm // GRID_W if y_col_major else tm
        in_specs += [pl.BlockSpec((1, y_rows, y.shape[2]), tile), pl.BlockSpec(wout.shape, const)]
        args += [y, wout]
        if y_col_major:
            src = (jnp.arange(tm) % GRID_W) * y_rows + jnp.arange(tm) // GRID_W
            in_specs.append(pl.BlockSpec((tm, tm), const))
            args.append(jax.nn.one_hot(src, tm, dtype=BF16))
    in_specs += [
        pl.BlockSpec((1, mod.shape[1], d), lambda b, i: (b, 0, 0)),
        pl.BlockSpec((1, d), const),
        pl.BlockSpec(wgu.shape, const),
        pl.BlockSpec(wdown.shape, const),
    ]
    args += [mod, g.reshape(1, d), wgu, wdown]
    if final_norm:
        in_specs.append(pl.BlockSpec((1, d), const))
        args.append(gfin.reshape(1, d))
    body = functools.partial(_ffn_kernel, mod_base=mod_base, has_pre=has_pre, y_col_major=y_col_major,
                             final_norm=final_norm, d_ff=d_ff, tf=tf)
    return pl.pallas_call(
        body,
        grid=(bv, tv // tm),
        in_specs=in_specs,
        out_specs=pl.BlockSpec((1, tm, d), tile),
        out_shape=jax.ShapeDtypeStruct(s.shape, F32),
        scratch_shapes=[pltpu.VMEM((tm, d), F32)],
        compiler_params=pltpu.CompilerParams(
            dimension_semantics=("parallel", "parallel"), vmem_limit_bytes=VMEM_LIMIT),
        name="ffn",
    )(*args)


def _inproj_kernel(s_ref, mod_ref, g_ref, w_ref, *rest, mod_base, col_major):
    shift = mod_ref[0, mod_base:mod_base + 1, :]
    scale = mod_ref[0, mod_base + 1:mod_base + 2, :]
    hb = _rms_modulate(s_ref[0], g_ref[...], shift, scale).astype(BF16)
    if not col_major:
        o_ref, = rest
        o_ref[0] = jnp.dot(hb, w_ref[...], preferred_element_type=F32)
        return
    perm_ref, o_ref = rest
    hp = jnp.dot(perm_ref[...], hb, preferred_element_type=F32).astype(BF16)
    res = jnp.dot(hp, w_ref[...], preferred_element_type=F32)
    rows = res.shape[0] // GRID_W
    n = res.shape[1]
    for c in range(GRID_W):
        o_ref[0, :, c * n:(c + 1) * n] = res[c * rows:(c + 1) * rows, :]


def _inproj(s, mod, mod_base, g, w, col_major=False, tm=512):
    bv, tv, d = s.shape
    n = w.shape[1]
    const = lambda b, i: (0, 0)
    in_specs = [
        pl.BlockSpec((1, tm, d), lambda b, i: (b, i, 0)),
        pl.BlockSpec((1, mod.shape[1], d), lambda b, i: (b, 0, 0)),
        pl.BlockSpec((1, d), const),
        pl.BlockSpec(w.shape, const),
    ]
    args = [s, mod, g.reshape(1, d), w]
    if col_major:
        rows = tm // GRID_W
        out_spec = pl.BlockSpec((1, rows, GRID_W * n), lambda b, i: (b, i, 0))
        out_shape = jax.ShapeDtypeStruct((bv, tv // GRID_W, GRID_W * n), F32)
        src = (jnp.arange(tm) % rows) * GRID_W + jnp.arange(tm) // rows
        in_specs.append(pl.BlockSpec((tm, tm), const))
        args.append(jax.nn.one_hot(src, tm, dtype=BF16))
    else:
        out_spec = pl.BlockSpec((1, tm, n), lambda b, i: (b, i, 0))
        out_shape = jax.ShapeDtypeStruct((bv, tv, n), F32)
    return pl.pallas_call(
        functools.partial(_inproj_kernel, mod_base=mod_base, col_major=col_major),
        grid=(bv, tv // tm),
        in_specs=in_specs,
        out_specs=out_spec,
        out_shape=out_shape,
        compiler_params=pltpu.CompilerParams(
            dimension_semantics=("parallel", "parallel"), vmem_limit_bytes=VMEM_LIMIT),
        name="inproj",
    )(*args)


CONV_W = 256
RW = 384
OFF_B = 3 * CONV_W
RWKV_COLS = 3 * RW + 64 + 64 + 128
OFF_C = OFF_B + RWKV_COLS
P_COLS = OFF_C + 2 * RW
F_PB = 3 * RW
F_KK = F_PB + RWKV_COLS
F_COLS = F_KK + RW
V_W0, V_A0, V_KK, V_KA, V_RK, V_LNG, V_LNB, V_CW, V_CB, V_BR, V_BI, V_LAM = 0, 1, 2, 3, 4, 5, 6, 7, 11, 12, 13, 14
VEC_ROWS = 16


def _mixer_kernel(*refs, d, n_chunks, line_is_chunk):
    it = iter(refs)
    p_ref, pp_ref, pn_ref = next(it), next(it), next(it)
    if d == 1:
        f_in_ref = next(it)
    s0_ref, h0_ref = next(it), next(it)
    vec_ref, mu_ref, conva_ref = next(it), next(it), next(it)
    w2_ref, a2_ref, g2_ref, wr_ref, wi_ref = next(it), next(it), next(it), next(it), next(it)
    out_ref, s_ref, h_ref = next(it), next(it), next(it)

    reverse = d == 1
    i = pl.program_id(1)
    ci = (n_chunks - 1 - i) if reverse else i
    keep_prev = jnp.where(ci == 0, 0.0, 1.0)
    keep_next = jnp.where(ci == n_chunks - 1, 0.0, 1.0)

    @pl.when(i == 0)
    def _():
        s_ref[...] = s0_ref[...]
        h_ref[...] = h0_ref[...]

    def vec(r, sl=slice(None)):
        return vec_ref[r:r + 1, sl]

    nb = p_ref.shape[0]
    grp = [slice(b * CHUNK, (b + 1) * CHUNK) for b in range(nb)]

    def stacked(ref, c0, c1):
        return jnp.concatenate([ref[b, :, c0:c1] for b in range(nb)], axis=0)

    lane_p = lax.broadcasted_iota(jnp.int32, (PAIR, PAIR), 1)
    row_p = lax.broadcasted_iota(jnp.int32, (PAIR, PAIR), 0)
    same_head = (lane_p >= HEAD) == (row_p >= HEAD)
    head_ones = jnp.where(same_head, 1.0, 0.0).astype(BF16)

    def head_sum(x):
        return jnp.concatenate(
            [_dot_ones_rhs(x[:, PAIR * hp:PAIR * (hp + 1)], head_ones) for hp in range(RW // PAIR)],
            axis=1)

    row_c = lax.broadcasted_iota(jnp.int32, (CHUNK, CHUNK), 0)
    col_c = lax.broadcasted_iota(jnp.int32, (CHUNK, CHUNK), 1)
    tri = jnp.where((col_c >= row_c) if reverse else (col_c <= row_c), 1.0, 0.0).astype(BF16)
    row_h = lax.broadcasted_iota(jnp.int32, (CHUNK, PAIR), 0)
    col_h = lax.broadcasted_iota(jnp.int32, (CHUNK, PAIR), 1) & (HEAD - 1)
    strict = (col_h > row_h) if reverse else (col_h < row_h)
    incl = (col_h >= row_h) if reverse else (col_h <= row_h)
    eye_h = jnp.where(col_h == row_h, 1.0, 0.0)
    eye_p = lane_p == row_p
    n_pair = RW // PAIR

    def blk(sp, gs):
        return tuple(x[gs[0], gs[1]] for x in sp)

    def rwkv_front(bs):
        gl = [slice(i * CHUNK, (i + 1) * CHUNK) for i in range(len(bs))]

        def st(ref, c0, c1):
            return jnp.concatenate([ref[b, :, c0:c1] for b in bs], axis=0)

        o = {}
        if reverse:
            f_in = st(f_in_ref, 0, F_COLS)
            pb = f_in[:, F_PB:F_PB + RWKV_COLS]
            kk = f_in[:, F_KK:F_KK + RW]
            o["f_in"] = f_in
        else:
            cur = st(p_ref, OFF_B, OFF_C)
            prev = st(pp_ref, OFF_B, OFF_C) * keep_prev
            nxt = st(pn_ref, OFF_B, OFF_C) * keep_next
            pb = (cur + mu_ref[0:1, :] * (_shift_dn(cur, prev, 1) - cur)
                  + mu_ref[1:2, :] * (_shift_up(cur, nxt, 1) - cur))
        r = pb[:, 0:RW]
        k = pb[:, RW:2 * RW]
        v = pb[:, 2 * RW:3 * RW]
        dw = pb[:, 3 * RW:3 * RW + 64]
        da = pb[:, 3 * RW + 64:3 * RW + 128]
        dg = pb[:, 3 * RW + 128:3 * RW + 256]
        if not reverse:
            kkr = k * vec(V_KK)
            kk = kkr * lax.rsqrt(head_sum(kkr * kkr) + 1e-12)
            for i, b in enumerate(bs):
                out_ref[b, :, F_PB:F_PB + RWKV_COLS] = pb[gl[i]]
                out_ref[b, :, F_KK:F_KK + RW] = kk[gl[i]]
        lw = -DECAY_SCALE * jax.nn.sigmoid(vec(V_W0) + _dot(jnp.tanh(dw), w2_ref[...]))
        a = jax.nn.sigmoid(vec(V_A0) + _dot(da, a2_ref[...]))
        kd = k * (1.0 + (a - 1.0) * vec(V_KA))
        beta = kk * a
        o["bonus"] = head_sum(r * kd * vec(V_RK)) * v
        if reverse:
            o["g"] = _dot(jax.nn.sigmoid(dg), g2_ref[...])

        c_inc = jnp.concatenate([_dot_exact_lhs(tri, lw[g]) for g in gl], axis=0)
        c_exc = c_inc - lw
        c_tot = [c_inc[g.start:g.start + 1, :] if reverse else c_inc[g.stop - 1:g.stop, :] for g in gl]
        e_neg = jnp.exp(-c_inc)
        gam = [jnp.exp(t) for t in c_tot]
        gam_rows = _rows_to_groups(gam)
        at = -(kk * jnp.exp(c_exc))
        rt = r * jnp.exp(c_inc)
        bh = beta * e_neg
        kh = kd * e_neg
        bg = bh * gam_rows
        kg = kh * gam_rows

        o["units"] = [(b, hp) for b in bs for hp in range(n_pair)]
        pos = [(gl[i], slice(PAIR * hp, PAIR * (hp + 1))) for i in range(len(bs)) for hp in range(n_pair)]
        o["gam"] = [gam[i][:, PAIR * hp:PAIR * (hp + 1)] for i in range(len(bs)) for hp in range(n_pair)]
        s_sp = [_split2(s_ref[b, hp]) for b, hp in o["units"]]
        at_sp, bh_sp, kh_sp = _split2(at), _split2(bh), _split2(kh)
        bg_sp, kg_sp, v_sp = _split2(bg), _split2(kg), _split2(v)
        rt_hi = rt.astype(BF16)
        gm = [_mm3_and_1(blk(at_sp, gs), rt_hi[gs[0], gs[1]],
                         _cat([_bd(blk(bh_sp, gs)), _bd(blk(kh_sp, gs))], 0), _NT) for gs in pos]
        o["a_m"] = [jnp.where(strict, m[0][:, 0:PAIR], 0.0) for m in gm]
        b_m = [jnp.where(strict, m[0][:, PAIR:2 * PAIR], 0.0) for m in gm]
        o["p_hi"] = [jnp.where(incl, m[1][:, 0:PAIR], 0.0).astype(BF16) for m in gm]
        o["q_hi"] = [jnp.where(incl, m[1][:, PAIR:2 * PAIR], 0.0).astype(BF16) for m in gm]
        z = [_mm3_and_1(blk(at_sp, gs), rt_hi[gs[0], gs[1]], s_sp[u], _NN) for u, gs in enumerate(pos)]
        o["z_r"] = [zz[1] for zz in z]
        o["w_m"] = [z[u][0] + _mm3(_split2(b_m[u]), _bd(blk(v_sp, gs))) for u, gs in enumerate(pos)]
        o["s_sp"] = s_sp
        o["v_sp"] = [blk(v_sp, gs) for gs in pos]
        o["bg_sp"] = [blk(bg_sp, gs) for gs in pos]
        o["kg_sp"] = [blk(kg_sp, gs) for gs in pos]
        return o

    fr = rwkv_front(list(range(nb)))
    units = fr["units"]
    pos = [(grp[b], slice(PAIR * hp, PAIR * (hp + 1))) for b, hp in units]
    a_m, p_hi, q_hi, z_r, w_m = (fr[k] for k in ("a_m", "p_hi", "q_hi", "z_r", "w_m"))
    s_sp, v_sp, bg_sp, kg_sp, gam = (fr[k] for k in ("s_sp", "v_sp", "bg_sp", "kg_sp", "gam"))
    bonus = fr["bonus"]
    if reverse:
        f_in, g = fr["f_in"], fr["g"]

    lru = {}

    def lru_pieces():
        xr = stacked(p_ref, OFF_C, OFF_C + RW)
        if not reverse:
            xh = stacked(pp_ref, OFF_C, OFF_C + RW) * keep_prev
            u_l = (vec(V_CW + 3) * xr + vec(V_CW + 2) * _shift_dn(xr, xh, 1)
                   + vec(V_CW + 1) * _shift_dn(xr, xh, 2) + vec(V_CW) * _shift_dn(xr, xh, 3))
        else:
            xh = stacked(pn_ref, OFF_C, OFF_C + RW) * keep_next
            u_l = (vec(V_CW) * xr + vec(V_CW + 1) * _shift_up(xr, xh, 1)
                   + vec(V_CW + 2) * _shift_up(xr, xh, 2) + vec(V_CW + 3) * _shift_up(xr, xh, 3))
        u_l = u_l + vec(V_CB)
        yield
        rg = jax.nn.sigmoid(_dot(u_l, wr_ref[...]) + vec(V_BR))
        yield
        ig = jax.nn.sigmoid(_dot(u_l, wi_ref[...]) + vec(V_BI))
        yield
        log_a = -LRU_C * (_softplus(-vec(V_LAM)) * rg)
        a_l = jnp.exp(log_a)
        th = jnp.tanh(log_a)
        b_l = jnp.sqrt(-2.0 * th / (1.0 - th)) * (ig * u_l)
        yield
        step = 1
        while step < CHUNK:
            b_l = b_l + a_l * _scan_shift(b_l, step, 0.0, reverse)
            a_l = a_l * _scan_shift(a_l, step, 1.0, reverse)
            step *= 2
            yield
        lru["h"] = b_l + a_l * _rows_to_groups([h_ref[b] for b in range(nb)])

    lru_gen = lru_pieces()

    def lru_advance():
        next(lru_gen, None)

    blk_of = lambda x, w: x >> (w.bit_length() - 1)
    same = {w: blk_of(row_h, w) == blk_of(col_h, w) for w in (INV_BASE, 2 * INV_BASE, 4 * INV_BASE)}
    d_m = [jnp.where(same[INV_BASE], m, 0.0) for m in a_m]
    t_m = [eye_h + m for m in d_m]
    pw_sp = [_split2(_mm3(ps, _bd(ps))) for ps in [_split2(m) for m in d_m]]
    lru_advance()
    prod = [_mm3(_cat([ps, _split2(t)], 0), _bd(ps)) for ps, t in zip(pw_sp, t_m)]
    pw_sp = [_split2(p[0:CHUNK]) for p in prod]
    t_m = [t + p[CHUNK:2 * CHUNK] for t, p in zip(t_m, prod)]
    lru_advance()
    t_m = [t + _mm3(_split2(t), _bd(ps)) for t, ps in zip(t_m, pw_sp)]
    lru_advance()
    width = INV_BASE
    while width < CHUNK:
        if 2 * width < CHUNK:
            off = same[2 * width] & jnp.logical_not(same[width])
        else:
            off = jnp.logical_not(same[width])
        e_sp = [_split2(jnp.where(off, m, 0.0)) for m in a_m]
        t_sp = [_split2(t) for t in t_m]
        x_sp = [_split2(_mm3(e, _bd(ts))) for e, ts in zip(e_sp, t_sp)]
        t_m = [t + _mm3(ts, _bd(x)) for t, ts, x in zip(t_m, t_sp, x_sp)]
        lru_advance()
        width *= 2
    u_sp = [_split2(_mm3(_split2(t), _bd(_split2(w)))) for t, w in zip(t_m, w_m)]
    lru_advance()
    y_rw = [z_r[u] + jnp.dot(
        jnp.concatenate([p_hi[u], q_hi[u]], axis=1),
        jnp.concatenate([_blockdiag(u_sp[u][0]), _blockdiag(v_sp[u][0])], axis=0),
        preferred_element_type=F32) for u in range(len(units))]
    lru_advance()
    s_upd = []
    for u in range(len(units)):
        dgam = jnp.where(eye_p, jnp.broadcast_to(gam[u], (PAIR, PAIR)), 0.0)
        s_new = _mm3(_cat([bg_sp[u], kg_sp[u], _split2(dgam)], 0),
                     _cat([u_sp[u], v_sp[u], s_sp[u]], 0), _TN)
        s_upd.append(jnp.where(same_head, s_new, 0.0))

    if not reverse:
        y_out = y_rw
    else:
        y_out = []
        for u, ((gr_, sl), (b, hp)) in enumerate(zip(pos, units)):
            ysum = y_rw[u] + f_in[gr_, sl]
            mean = _dot_ones_rhs(ysum, head_ones) * (1.0 / HEAD)
            cen = ysum - mean
            var = _dot_ones_rhs(cen * cen, head_ones) * (1.0 / HEAD)
            gn = cen * lax.rsqrt(var + GN_EPS) * vec(V_LNG, sl) + vec(V_LNB, sl)
            y_out.append((gn + (f_in[gr_, RW + PAIR * hp:RW + PAIR * (hp + 1)] + bonus[gr_, sl])) * g[gr_, sl])

    y_base = CONV_W if reverse else 0
    for u, (b, hp) in enumerate(units):
        s_ref[b, hp] = s_upd[u]
        out_ref[b, :, y_base + PAIR * hp:y_base + PAIR * (hp + 1)] = y_out[u]
    if not reverse:
        for b in range(nb):
            out_ref[b, :, RW:2 * RW] = bonus[grp[b]]

    for _ in lru_gen:
        pass
    h = lru["h"]
    for b in range(nb):
        hb = h[grp[b]]
        h_ref[b] = hb[0:1, :] if reverse else hb[CHUNK - 1:CHUNK, :]

    if not reverse:
        for b in range(nb):
            out_ref[b, :, 2 * RW:3 * RW] = h[grp[b]]
    else:
        gr = stacked(p_ref, OFF_C + RW, OFF_C + 2 * RW)
        y_lru = jax.nn.gelu(gr) * (f_in[:, 2 * RW:3 * RW] + h)
        bgate = stacked(p_ref, 0, CONV_W)
        uc = stacked(p_ref, CONV_W, 2 * CONV_W) * stacked(p_ref, 2 * CONV_W, 3 * CONV_W)
        if line_is_chunk:
            zero = jnp.zeros((nb * HALO, CONV_W), F32)
            up, un = zero, zero
        else:
            up = stacked(pp_ref, CONV_W, 2 * CONV_W) * stacked(pp_ref, 2 * CONV_W, 3 * CONV_W) * keep_prev
            un = stacked(pn_ref, CONV_W, 2 * CONV_W) * stacked(pn_ref, 2 * CONV_W, 3 * CONV_W) * keep_next
        yc = bgate * (conva_ref[0:1, :] * _shift_dn(uc, up, 1) + conva_ref[1:2, :] * uc
                      + conva_ref[2:3, :] * _shift_up(uc, un, 1))
        for b in range(nb):
            out_ref[b, :, CONV_W + RW:CONV_W + 2 * RW] = y_lru[grp[b]]
            out_ref[b, :, 0:CONV_W] = yc[grp[b]]


def _mixer_pass(p, f_in, s0, h0, prm, d, col_major, line_is_chunk):
    bsz = p.shape[0]
    t_len = p.shape[1] * p.shape[2] // P_COLS
    n = t_len // CHUNK
    reverse = d == 1
    out_cols = 4 * CONV_W if reverse else F_COLS

    def ci(i):
        return (n - 1 - i) if reverse else i

    if col_major:
        assert p.shape == (bsz, CHUNK, GRID_W * P_COLS) and n == GRID_W
        p_view = p
        cur_spec = pl.BlockSpec((MIX_NB, CHUNK,P_COLS), lambda b, i: (b, 0, ci(i)))
        prev_spec = pl.BlockSpec((MIX_NB, HALO,P_COLS),
                                 lambda b, i: (b, CHUNK // HALO - 1, jnp.maximum(ci(i) - 1, 0)))
        next_spec = pl.BlockSpec((MIX_NB, HALO,P_COLS), lambda b, i: (b, 0, jnp.minimum(ci(i) + 1, n - 1)))
    else:
        p_view = p
        per = CHUNK // HALO
        cur_spec = pl.BlockSpec((MIX_NB, CHUNK,P_COLS), lambda b, i: (b, ci(i), 0))
        prev_spec = pl.BlockSpec((MIX_NB, HALO,P_COLS), lambda b, i: (b, jnp.maximum(per * ci(i) - 1, 0), 0))
        next_spec = pl.BlockSpec((MIX_NB, HALO,P_COLS),
                                 lambda b, i: (b, jnp.minimum(per * ci(i) + per, per * n - 1), 0))
    full = lambda a: pl.BlockSpec(a.shape, lambda b, i: (0,) * a.ndim)
    in_specs = [cur_spec, prev_spec, next_spec]
    args = [p_view, p_view, p_view]
    if reverse:
        in_specs.append(pl.BlockSpec((MIX_NB, CHUNK,F_COLS), lambda b, i: (b, ci(i), 0)))
        args.append(f_in)
    state_specs = [pl.BlockSpec((MIX_NB,) + s0.shape[1:], lambda b, i: (b, 0, 0, 0)),
                   pl.BlockSpec((MIX_NB, 1, RW), lambda b, i: (b, 0, 0))]
    in_specs += state_specs
    args += [s0, h0]
    params = [prm["vec"][d], prm["mu"], prm["conv_a"], prm["w2"][d], prm["a2"][d], prm["g2"],
              prm["wr"][d], prm["wi"][d]]
    in_specs += [full(a) for a in params]
    args += params

    if reverse and col_major:
        out_shape = jax.ShapeDtypeStruct((bsz, CHUNK, n * out_cols), F32)
        out_spec = pl.BlockSpec((MIX_NB, CHUNK,out_cols), lambda b, i: (b, 0, ci(i)))
    else:
        out_shape = jax.ShapeDtypeStruct((bsz, t_len, out_cols), F32)
        out_spec = pl.BlockSpec((MIX_NB, CHUNK,out_cols), lambda b, i: (b, ci(i), 0))

    out, s_fin, h_fin = pl.pallas_call(
        functools.partial(_mixer_kernel, d=d, n_chunks=n, line_is_chunk=line_is_chunk),
        grid=(bsz // MIX_NB, n),
        in_specs=in_specs,
        out_specs=[out_spec] + state_specs,
        out_shape=[out_shape, jax.ShapeDtypeStruct(s0.shape, F32), jax.ShapeDtypeStruct(h0.shape, F32)],
        compiler_params=pltpu.CompilerParams(
            dimension_semantics=("parallel", "arbitrary"), vmem_limit_bytes=VMEM_LIMIT),
        name="mixer_bwd" if reverse else "mixer_fwd",
    )(*args)
    return out, s_fin, h_fin


def _mix_stream(p, prm, col_major, line_is_chunk, init):
    (s0f, h0f), (s0b, h0b) = init
    f, sf, hf = _mixer_pass(p, None, s0f, h0f, prm, 0, col_major, line_is_chunk)
    y, sb, hb = _mixer_pass(p, f, s0b, h0b, prm, 1, col_major, line_is_chunk)
    return y, ((sf, hf), (sb, hb))


def _mixer_params(l, conv_a, rwkv_mu, rwkv_w0, rwkv_w2, rwkv_a0, rwkv_a2, rwkv_g2, rwkv_kk, rwkv_ka,
                  rwkv_rk, rwkv_lnx_g, rwkv_lnx_b, lru_conv_w, lru_conv_b, lru_w_r, lru_b_r, lru_w_i,
                  lru_b_i, lru_lam):
    def both(a):
        return jnp.broadcast_to(a[None], (2,) + a.shape)

    rows = [rwkv_w0[l], rwkv_a0[l], both(rwkv_kk[l]), both(rwkv_ka[l]), both(rwkv_rk[l]),
            both(rwkv_lnx_g[l]), both(rwkv_lnx_b[l]),
            lru_conv_w[l][:, 0], lru_conv_w[l][:, 1], lru_conv_w[l][:, 2], lru_conv_w[l][:, 3],
            lru_conv_b[l], lru_b_r[l], lru_b_i[l], lru_lam[l], jnp.zeros((2, RW), F32)]
    assert len(rows) == VEC_ROWS
    vec = jnp.stack(rows, axis=1)
    nb = lru_w_r.shape[2]
    eye = jnp.eye(nb, dtype=F32)

    def bd(w):
        return jnp.einsum("dnij,nm->dnimj", w, eye).reshape(2, nb * HEAD, nb * HEAD).astype(BF16)

    return {
        "vec": vec, "mu": rwkv_mu[l], "conv_a": conv_a[l],
        "w2": rwkv_w2[l].astype(BF16), "a2": rwkv_a2[l].astype(BF16), "g2": rwkv_g2[l].astype(BF16),
        "wr": bd(lru_w_r[l]), "wi": bd(lru_w_i[l]),
    }


def kernel(x, c, ctx, c_ctx, w_mod, b_mod, g_ffn1, w_gu1, w_down1, g_mix, w_in, conv_a, rwkv_mu, rwkv_w0, rwkv_w2, rwkv_a0, rwkv_a2, rwkv_g2, rwkv_kk, rwkv_ka, rwkv_rk, rwkv_lnx_g, rwkv_lnx_b, lru_conv_w, lru_conv_b, lru_w_r, lru_b_r, lru_w_i, lru_b_i, lru_lam, w_out, g_ffn2, w_gu2, w_down2, g_final):
    bsz, t_len, d = x.shape
    depth = w_mod.shape[0]
    n_ctx = ctx.shape[1]
    assert w_in.shape[2] == P_COLS and t_len % CHUNK == 0 and n_ctx % CHUNK == 0

    mod_rows = -(-(bsz + 1) // 8) * 8
    cc = jnp.concatenate([c, c_ctx[None, :], jnp.zeros((mod_rows - bsz - 1, d), F32)], axis=0)
    mods = _modulation(cc, w_mod, b_mod)

    zero_state = ((jnp.zeros((bsz, RW // PAIR, PAIR, PAIR), F32), jnp.zeros((bsz, 1, RW), F32)),) * 2
    s_lat = x
    s_ctx = ctx.reshape(1, bsz * n_ctx, d)
    for l in range(depth):
        last = l == depth - 1
        m_lat = mods[l, :bsz].reshape(bsz, 9, d)
        m_ctx = mods[l, bsz:bsz + 1].reshape(1, 9, d)
        wgu1, wd1 = w_gu1[l].astype(BF16), w_down1[l].astype(BF16)
        wgu2, wd2 = w_gu2[l].astype(BF16), w_down2[l].astype(BF16)
        win, wout = w_in[l].astype(BF16), w_out[l].astype(BF16)
        prm = _mixer_params(l, conv_a, rwkv_mu, rwkv_w0, rwkv_w2, rwkv_a0, rwkv_a2, rwkv_g2, rwkv_kk,
                            rwkv_ka, rwkv_rk, rwkv_lnx_g, rwkv_lnx_b, lru_conv_w, lru_conv_b, lru_w_r,
                            lru_b_r, lru_w_i, lru_b_i, lru_lam)

        s_lat = _ffn(s_lat, m_lat, 0, g_ffn1[l], wgu1, wd1)
        s_ctx = _ffn(s_ctx, m_ctx, 0, g_ffn1[l], wgu1, wd1)
        col_major = l % 2 == 1
        p_lat = _inproj(s_lat, m_lat, 3, g_mix[l], win, col_major=col_major)
        p_ctx = _inproj(s_ctx, m_ctx, 3, g_mix[l], win)

        y_ctx, ctx_fin = _mix_stream(p_ctx.reshape(bsz, n_ctx, P_COLS), prm, False, False, zero_state)
        y_lat, _ = _mix_stream(p_lat, prm, col_major, True, ctx_fin)

        s_lat = _ffn(s_lat, m_lat, 6, g_ffn2[l], wgu2, wd2, y=y_lat, wout=wout, y_col_major=col_major,
                     gfin=g_final if last else None)
        if not last:
            s_ctx = _ffn(s_ctx, m_ctx, 6, g_ffn2[l], wgu2, wd2,
                         y=y_ctx.reshape(1, bsz * n_ctx, 4 * CONV_W), wout=wout)
    return s_lat
```

```python
import functools

import jax
import jax.numpy as jnp
from jax import lax
from jax.experimental import pallas as pl
from jax.experimental.pallas import tpu as pltpu

F32 = jnp.float32
BF16 = jnp.bfloat16

HEAD = 64
LANES = 128
PAIR = 2 * HEAD
CHUNK = 64
HALO = 8
INV_BASE = 8
MIX_NB = 8
GRID_W = 64
NORM_EPS = 1e-6
GN_EPS = 64e-5
LRU_C = 8.0
DECAY_SCALE = 0.6065306597126334
VMEM_LIMIT = 56 * 1024 * 1024


def _dot(a, b):
    return jnp.dot(a.astype(BF16), b.astype(BF16), preferred_element_type=F32)


def _split2(a):
    hi = a.astype(BF16)
    lo = (a - hi.astype(F32)).astype(BF16)
    return hi, lo


_NN = (((1,), (0,)), ((), ()))
_NT = (((1,), (1,)), ((), ()))
_TN = (((0,), (0,)), ((), ()))


def _mm3(a, b, dims=_NN):
    ah, al = a
    bh, bl = b
    d = functools.partial(lax.dot_general, dimension_numbers=dims, preferred_element_type=F32)
    free = 1 if dims == _TN else 0
    m = ah.shape[free]
    both = d(jnp.concatenate([ah, al], axis=free), bh)
    return both[:m] + (d(ah, bl) + both[m:])


def _mm3_and_1(a, c_hi, b, dims):
    ah, al = a
    bh, bl = b
    d = functools.partial(lax.dot_general, dimension_numbers=dims, preferred_element_type=F32)
    m = ah.shape[0]
    both = d(jnp.concatenate([ah, al, c_hi], axis=0), bh)
    return both[:m] + (d(ah, bl) + both[m:2 * m]), both[2 * m:]


def _cat(parts, axis):
    return tuple(jnp.concatenate([p[i] for p in parts], axis=axis) for i in range(2))


def _bd(sp):
    return tuple(_blockdiag(x) for x in sp)


def _dot_ones_rhs(a, ones_rhs):
    ah, al = _split2(a)
    d = functools.partial(jnp.dot, preferred_element_type=F32)
    return d(ah, ones_rhs) + d(al, ones_rhs)


def _dot_exact_lhs(ones_lhs, b):
    b1 = b.astype(BF16)
    r1 = b - b1.astype(F32)
    b2 = r1.astype(BF16)
    b3 = (r1 - b2.astype(F32)).astype(BF16)
    d = functools.partial(jnp.dot, preferred_element_type=F32)
    return d(ones_lhs, b1) + (d(ones_lhs, b2) + d(ones_lhs, b3))


def _softplus(x):
    return jnp.maximum(x, 0.0) + jnp.log1p(jnp.exp(-jnp.abs(x)))


def _rms_modulate(s, g, shift, scale):
    y = s * lax.rsqrt(jnp.mean(s * s, axis=-1, keepdims=True) + NORM_EPS)
    return (y * g) * (1.0 + scale) + shift


def _per_group(fn, x, halo, *args):
    nb = x.shape[0] // CHUNK
    if nb == 1:
        return fn(x, halo, *args)
    return jnp.concatenate(
        [fn(x[b * CHUNK:(b + 1) * CHUNK], None if halo is None else halo[b * HALO:(b + 1) * HALO], *args)
         for b in range(nb)], axis=0)


def _shift_dn(x, prev, k):
    return _per_group(_shift_dn1, x, prev, k)


def _shift_up(x, nxt, k):
    return _per_group(_shift_up1, x, nxt, k)


def _scan_shift(x, k, fill, reverse):
    return _per_group(_scan_shift1, x, None, k, fill, reverse)


def _rows_to_groups(rows):
    return jnp.concatenate([jnp.broadcast_to(r, (CHUNK, r.shape[1])) for r in rows], axis=0)


def _shift_dn1(x, prev, k):
    ch = x.shape[1]
    rolled = pltpu.roll(x, k, axis=0)
    pr = pltpu.roll(prev, k, axis=0)
    r8 = lax.broadcasted_iota(jnp.int32, (HALO, ch), 0)
    head = jnp.where(r8 < k, pr, rolled[:HALO])
    return jnp.concatenate([head, rolled[HALO:]], axis=0)


def _shift_up1(x, nxt, k):
    n, ch = x.shape
    rolled = pltpu.roll(x, n - k, axis=0)
    nr = pltpu.roll(nxt, HALO - k, axis=0)
    r8 = lax.broadcasted_iota(jnp.int32, (HALO, ch), 0)
    tail = jnp.where(r8 >= HALO - k, nr, rolled[n - HALO:])
    return jnp.concatenate([rolled[:n - HALO], tail], axis=0)


def _scan_shift1(x, _, k, fill, reverse):
    n, ch = x.shape
    if k % HALO == 0:
        pad = jnp.full((k, ch), fill, x.dtype)
        return jnp.concatenate([x[k:], pad] if reverse else [pad, x[:n - k]], axis=0)
    rows = lax.broadcasted_iota(jnp.int32, (n, ch), 0)
    if not reverse:
        return jnp.where(rows >= k, pltpu.roll(x, k, axis=0), fill)
    return jnp.where(rows < n - k, pltpu.roll(x, n - k, axis=0), fill)


def _blockdiag(xp):
    n = xp.shape[0]
    lane = lax.broadcasted_iota(jnp.int32, (n, PAIR), 1)
    zero = jnp.zeros_like(xp)
    top = jnp.where(lane < HEAD, xp, zero)
    bot = jnp.where(lane >= HEAD, xp, zero)
    return jnp.concatenate([top, bot], axis=0)


def _mod_kernel(c_ref, w_ref, b_ref, o_ref):
    cc = c_ref[...]
    act = cc * jax.nn.sigmoid(cc)
    o_ref[0] = _dot(act, w_ref[0]) + b_ref[0]


def _modulation(cc, w_mod, b_mod):
    depth, d, n = w_mod.shape
    rows = cc.shape[0]
    tn = n // 8
    return pl.pallas_call(
        _mod_kernel,
        grid=(depth, n // tn),
        in_specs=[
            pl.BlockSpec((rows, d), lambda l, j: (0, 0)),
            pl.BlockSpec((1, d, tn), lambda l, j: (l, 0, j)),
            pl.BlockSpec((1, 1, tn), lambda l, j: (l, 0, j)),
        ],
        out_specs=pl.BlockSpec((1, rows, tn), lambda l, j: (l, 0, j)),
        out_shape=jax.ShapeDtypeStruct((depth, rows, n), F32),
        compiler_params=pltpu.CompilerParams(
            dimension_semantics=("parallel", "parallel"), vmem_limit_bytes=VMEM_LIMIT),
        name="modulation",
    )(cc, w_mod, b_mod.reshape(depth, 1, n))


def _ffn_kernel(*refs, mod_base, has_pre, y_col_major, final_norm, d_ff, tf):
    it = iter(refs)
    s_ref = next(it)
    if has_pre:
        y_ref = next(it)
        wout_ref = next(it)
        if y_col_major:
            perm_ref = next(it)
    mod_ref = next(it)
    g_ref = next(it)
    wgu_ref = next(it)
    wdown_ref = next(it)
    if final_norm:
        gfin_ref = next(it)
    o_ref = next(it)
    acc_ref = next(it)

    s = s_ref[0]
    if has_pre:
        gate_mix = mod_ref[0, mod_base - 1:mod_base, :]
        if y_col_major:
            mix = wout_ref.shape[1]
            yc = jnp.concatenate([y_ref[0, :, c * mix:(c + 1) * mix] for c in range(GRID_W)], axis=0)
            yb = jnp.dot(perm_ref[...], yc.astype(BF16), preferred_element_type=F32).astype(BF16)
        else:
            yb = y_ref[0].astype(BF16)
        s = s + gate_mix * jnp.dot(yb, wout_ref[0], preferred_element_type=F32)
    shift = mod_ref[0, mod_base:mod_base + 1, :]
    scale = mod_ref[0, mod_base + 1:mod_base + 2, :]
    gate = mod_ref[0, mod_base + 2:mod_base + 3, :]
    hb = _rms_modulate(s, g_ref[...], shift, scale).astype(BF16)
    for j in range(d_ff // tf):
        gt = jnp.dot(hb, wgu_ref[0, :, j * tf:(j + 1) * tf], preferred_element_type=F32)
        up = jnp.dot(hb, wgu_ref[0, :, d_ff + j * tf:d_ff + (j + 1) * tf], preferred_element_type=F32)
        act = ((gt * jax.nn.sigmoid(gt)) * up).astype(BF16)
        part = jnp.dot(act, wdown_ref[0, j * tf:(j + 1) * tf, :], preferred_element_type=F32)
        if j == 0:
            acc_ref[...] = part
        else:
            acc_ref[...] += part
    out = s + (0.5 * gate) * acc_ref[...]
    if final_norm:
        out = out * lax.rsqrt(jnp.mean(out * out, axis=-1, keepdims=True) + NORM_EPS) * gfin_ref[...]
    o_ref[0] = out


def _layer_spec(w, layer):
    return pl.BlockSpec((1,) + w.shape[1:], lambda b, i: (layer, 0, 0))


def _ffn(s, mod, mod_base, g, wgu, wdown, layer, y=None, wout=None, y_col_major=False, gfin=None, tm=512):
    bv, tv, d = s.shape
    d_ff = wdown.shape[1]
    tf = 256
    has_pre = y is not None
    final_norm = gfin is not None
    const = lambda b, i: (0, 0)
    tile = lambda b, i: (b, i, 0)
    in_specs = [pl.BlockSpec((1, tm, d), tile)]
    args = [s]
    if has_pre:
        y_rows = tm // GRID_W if y_col_major else tm
        in_specs += [pl.BlockSpec((1, y_rows, y.shape[2]), tile), _layer_spec(wout, layer)]
        args += [y, wout]
        if y_col_major:
            src = (jnp.arange(tm) % GRID_W) * y_rows + jnp.arange(tm) // GRID_W
            in_specs.append(pl.BlockSpec((tm, tm), const))
            args.append(jax.nn.one_hot(src, tm, dtype=BF16))
    in_specs += [
        pl.BlockSpec((1, mod.shape[1], d), lambda b, i: (b, 0, 0)),
        pl.BlockSpec((1, d), const),
        _layer_spec(wgu, layer),
        _layer_spec(wdown, layer),
    ]
    args += [mod, g.reshape(1, d), wgu, wdown]
    if final_norm:
        in_specs.append(pl.BlockSpec((1, d), const))
        args.append(gfin.reshape(1, d))
    body = functools.partial(_ffn_kernel, mod_base=mod_base, has_pre=has_pre, y_col_major=y_col_major,
                             final_norm=final_norm, d_ff=d_ff, tf=tf)
    return pl.pallas_call(
        body,
        grid=(bv, tv // tm),
        in_specs=in_specs,
        out_specs=pl.BlockSpec((1, tm, d), tile),
        out_shape=jax.ShapeDtypeStruct(s.shape, F32),
        scratch_shapes=[pltpu.VMEM((tm, d), F32)],
        compiler_params=pltpu.CompilerParams(
            dimension_semantics=("parallel", "parallel"), vmem_limit_bytes=VMEM_LIMIT),
        name="ffn",
    )(*args)


def _inproj_kernel(s_ref, mod_ref, g_ref, w_ref, *rest, mod_base, col_major):
    shift = mod_ref[0, mod_base:mod_base + 1, :]
    scale = mod_ref[0, mod_base + 1:mod_base + 2, :]
    hb = _rms_modulate(s_ref[0], g_ref[...], shift, scale).astype(BF16)
    if not col_major:
        o_ref, = rest
        o_ref[0] = jnp.dot(hb, w_ref[0], preferred_element_type=F32)
        return
    perm_ref, o_ref = rest
    hp = jnp.dot(perm_ref[...], hb, preferred_element_type=F32).astype(BF16)
    res = jnp.dot(hp, w_ref[0], preferred_element_type=F32)
    rows = res.shape[0] // GRID_W
    n = res.shape[1]
    for c in range(GRID_W):
        o_ref[0, :, c * n:(c + 1) * n] = res[c * rows:(c + 1) * rows, :]


def _inproj(s, mod, mod_base, g, w, layer, col_major=False, tm=512):
    bv, tv, d = s.shape
    n = w.shape[2]
    const = lambda b, i: (0, 0)
    in_specs = [
        pl.BlockSpec((1, tm, d), lambda b, i: (b, i, 0)),
        pl.BlockSpec((1, mod.shape[1], d), lambda b, i: (b, 0, 0)),
        pl.BlockSpec((1, d), const),
        _layer_spec(w, layer),
    ]
    args = [s, mod, g.reshape(1, d), w]
    if col_major:
        rows = tm // GRID_W
        out_spec = pl.BlockSpec((1, rows, GRID_W * n), lambda b, i: (b, i, 0))
        out_shape = jax.ShapeDtypeStruct((bv, tv // GRID_W, GRID_W * n), F32)
        src = (jnp.arange(tm) % rows) * GRID_W + jnp.arange(tm) // rows
        in_specs.append(pl.BlockSpec((tm, tm), const))
        args.append(jax.nn.one_hot(src, tm, dtype=BF16))
    else:
        out_spec = pl.BlockSpec((1, tm, n), lambda b, i: (b, i, 0))
        out_shape = jax.ShapeDtypeStruct((bv, tv, n), F32)
    return pl.pallas_call(
        functools.partial(_inproj_kernel, mod_base=mod_base, col_major=col_major),
        grid=(bv, tv // tm),
        in_specs=in_specs,
        out_specs=out_spec,
        out_shape=out_shape,
        compiler_params=pltpu.CompilerParams(
            dimension_semantics=("parallel", "parallel"), vmem_limit_bytes=VMEM_LIMIT),
        name="inproj",
    )(*args)


CONV_W = 256
RW = 384
OFF_B = 3 * CONV_W
RWKV_COLS = 3 * RW + 64 + 64 + 128
OFF_C = OFF_B + RWKV_COLS
P_COLS = OFF_C + 2 * RW
F_PB = 3 * RW
F_KK = F_PB + RWKV_COLS
F_COLS = F_KK + RW
V_W0, V_A0, V_KK, V_KA, V_RK, V_LNG, V_LNB, V_CW, V_CB, V_BR, V_BI, V_LAM = 0, 1, 2, 3, 4, 5, 6, 7, 11, 12, 13, 14
VEC_ROWS = 16


def _mixer_kernel(*refs, d, n_chunks, line_is_chunk):
    it = iter(refs)
    p_ref, pp_ref, pn_ref = next(it), next(it), next(it)
    if d == 1:
        f_in_ref = next(it)
    s0_ref, h0_ref = next(it), next(it)
    vec_ref, mu_ref, conva_ref = next(it), next(it), next(it)
    w2_ref, a2_ref, g2_ref, wr_ref, wi_ref = next(it), next(it), next(it), next(it), next(it)
    out_ref, s_ref, h_ref = next(it), next(it), next(it)

    reverse = d == 1
    i = pl.program_id(1)
    ci = (n_chunks - 1 - i) if reverse else i
    keep_prev = jnp.where(ci == 0, 0.0, 1.0)
    keep_next = jnp.where(ci == n_chunks - 1, 0.0, 1.0)

    @pl.when(i == 0)
    def _():
        s_ref[...] = s0_ref[...]
        h_ref[...] = h0_ref[...]

    def vec(r, sl=slice(None)):
        return vec_ref[r:r + 1, sl]

    nb = p_ref.shape[0]
    grp = [slice(b * CHUNK, (b + 1) * CHUNK) for b in range(nb)]

    def stacked(ref, c0, c1):
        return jnp.concatenate([ref[b, :, c0:c1] for b in range(nb)], axis=0)

    lane_p = lax.broadcasted_iota(jnp.int32, (PAIR, PAIR), 1)
    row_p = lax.broadcasted_iota(jnp.int32, (PAIR, PAIR), 0)
    same_head = (lane_p >= HEAD) == (row_p >= HEAD)
    head_ones = jnp.where(same_head, 1.0, 0.0).astype(BF16)

    def head_sum(x):
        return jnp.concatenate(
            [_dot_ones_rhs(x[:, PAIR * hp:PAIR * (hp + 1)], head_ones) for hp in range(RW // PAIR)],
            axis=1)

    row_c = lax.broadcasted_iota(jnp.int32, (CHUNK, CHUNK), 0)
    col_c = lax.broadcasted_iota(jnp.int32, (CHUNK, CHUNK), 1)
    tri = jnp.where((col_c >= row_c) if reverse else (col_c <= row_c), 1.0, 0.0).astype(BF16)
    row_h = lax.broadcasted_iota(jnp.int32, (CHUNK, PAIR), 0)
    col_h = lax.broadcasted_iota(jnp.int32, (CHUNK, PAIR), 1) & (HEAD - 1)
    strict = (col_h > row_h) if reverse else (col_h < row_h)
    incl = (col_h >= row_h) if reverse else (col_h <= row_h)
    eye_h = jnp.where(col_h == row_h, 1.0, 0.0)
    eye_p = lane_p == row_p
    n_pair = RW // PAIR

    def blk(sp, gs):
        return tuple(x[gs[0], gs[1]] for x in sp)

    def rwkv_front(bs):
        gl = [slice(i * CHUNK, (i + 1) * CHUNK) for i in range(len(bs))]

        def st(ref, c0, c1):
            return jnp.concatenate([ref[b, :, c0:c1] for b in bs], axis=0)

        o = {}
        if reverse:
            f_in = st(f_in_ref, 0, F_COLS)
            pb = f_in[:, F_PB:F_PB + RWKV_COLS]
            kk = f_in[:, F_KK:F_KK + RW]
            o["f_in"] = f_in
        else:
            cur = st(p_ref, OFF_B, OFF_C)
            prev = st(pp_ref, OFF_B, OFF_C) * keep_prev
            nxt = st(pn_ref, OFF_B, OFF_C) * keep_next
            pb = (cur + mu_ref[0:1, :] * (_shift_dn(cur, prev, 1) - cur)
                  + mu_ref[1:2, :] * (_shift_up(cur, nxt, 1) - cur))
        r = pb[:, 0:RW]
        k = pb[:, RW:2 * RW]
        v = pb[:, 2 * RW:3 * RW]
        dw = pb[:, 3 * RW:3 * RW + 64]
        da = pb[:, 3 * RW + 64:3 * RW + 128]
        dg = pb[:, 3 * RW + 128:3 * RW + 256]
        if not reverse:
            kkr = k * vec(V_KK)
            kk = kkr * lax.rsqrt(head_sum(kkr * kkr) + 1e-12)
            for i, b in enumerate(bs):
                out_ref[b, :, F_PB:F_PB + RWKV_COLS] = pb[gl[i]]
                out_ref[b, :, F_KK:F_KK + RW] = kk[gl[i]]
        lw = -DECAY_SCALE * jax.nn.sigmoid(vec(V_W0) + _dot(jnp.tanh(dw), w2_ref[...]))
        a = jax.nn.sigmoid(vec(V_A0) + _dot(da, a2_ref[...]))
        kd = k * (1.0 + (a - 1.0) * vec(V_KA))
        beta = kk * a
        o["bonus"] = head_sum(r * kd * vec(V_RK)) * v
        if reverse:
            o["g"] = _dot(jax.nn.sigmoid(dg), g2_ref[...])

        c_inc = jnp.concatenate([_dot_exact_lhs(tri, lw[g]) for g in gl], axis=0)
        c_exc = c_inc - lw
        c_tot = [c_inc[g.start:g.start + 1, :] if reverse else c_inc[g.stop - 1:g.stop, :] for g in gl]
        e_neg = jnp.exp(-c_inc)
        gam = [jnp.exp(t) for t in c_tot]
        gam_rows = _rows_to_groups(gam)
        at = -(kk * jnp.exp(c_exc))
        rt = r * jnp.exp(c_inc)
        bh = beta * e_neg
        kh = kd * e_neg
        bg = bh * gam_rows
        kg = kh * gam_rows

        o["units"] = [(b, hp) for b in bs for hp in range(n_pair)]
        pos = [(gl[i], slice(PAIR * hp, PAIR * (hp + 1))) for i in range(len(bs)) for hp in range(n_pair)]
        o["gam"] = [gam[i][:, PAIR * hp:PAIR * (hp + 1)] for i in range(len(bs)) for hp in range(n_pair)]
        s_sp = [_split2(s_ref[b, hp]) for b, hp in o["units"]]
        at_sp, bh_sp, kh_sp = _split2(at), _split2(bh), _split2(kh)
        bg_sp, kg_sp, v_sp = _split2(bg), _split2(kg), _split2(v)
        rt_hi = rt.astype(BF16)
        gm = [_mm3_and_1(blk(at_sp, gs), rt_hi[gs[0], gs[1]],
                         _cat([_bd(blk(bh_sp, gs)), _bd(blk(kh_sp, gs))], 0), _NT) for gs in pos]
        o["a_m"] = [jnp.where(strict, m[0][:, 0:PAIR], 0.0) for m in gm]
        b_m = [jnp.where(strict, m[0][:, PAIR:2 * PAIR], 0.0) for m in gm]
        o["p_hi"] = [jnp.where(incl, m[1][:, 0:PAIR], 0.0).astype(BF16) for m in gm]
        o["q_hi"] = [jnp.where(incl, m[1][:, PAIR:2 * PAIR], 0.0).astype(BF16) for m in gm]
        z = [_mm3_and_1(blk(at_sp, gs), rt_hi[gs[0], gs[1]], s_sp[u], _NN) for u, gs in enumerate(pos)]
        o["z_r"] = [zz[1] for zz in z]
        o["w_m"] = [z[u][0] + _mm3(_split2(b_m[u]), _bd(blk(v_sp, gs))) for u, gs in enumerate(pos)]
        o["s_sp"] = s_sp
        o["v_sp"] = [blk(v_sp, gs) for gs in pos]
        o["bg_sp"] = [blk(bg_sp, gs) for gs in pos]
        o["kg_sp"] = [blk(kg_sp, gs) for gs in pos]
        return o

    fr = rwkv_front(list(range(nb)))
    units = fr["units"]
    pos = [(grp[b], slice(PAIR * hp, PAIR * (hp + 1))) for b, hp in units]
    a_m, p_hi, q_hi, z_r, w_m = (fr[k] for k in ("a_m", "p_hi", "q_hi", "z_r", "w_m"))
    s_sp, v_sp, bg_sp, kg_sp, gam = (fr[k] for k in ("s_sp", "v_sp", "bg_sp", "kg_sp", "gam"))
    bonus = fr["bonus"]
    if reverse:
        f_in, g = fr["f_in"], fr["g"]

    lru = {}

    def lru_pieces():
        xr = stacked(p_ref, OFF_C, OFF_C + RW)
        if not reverse:
            xh = stacked(pp_ref, OFF_C, OFF_C + RW) * keep_prev
            u_l = (vec(V_CW + 3) * xr + vec(V_CW + 2) * _shift_dn(xr, xh, 1)
                   + vec(V_CW + 1) * _shift_dn(xr, xh, 2) + vec(V_CW) * _shift_dn(xr, xh, 3))
        else:
            xh = stacked(pn_ref, OFF_C, OFF_C + RW) * keep_next
            u_l = (vec(V_CW) * xr + vec(V_CW + 1) * _shift_up(xr, xh, 1)
                   + vec(V_CW + 2) * _shift_up(xr, xh, 2) + vec(V_CW + 3) * _shift_up(xr, xh, 3))
        u_l = u_l + vec(V_CB)
        yield
        rg = jax.nn.sigmoid(_dot(u_l, wr_ref[...]) + vec(V_BR))
        yield
        ig = jax.nn.sigmoid(_dot(u_l, wi_ref[...]) + vec(V_BI))
        yield
        log_a = -LRU_C * (_softplus(-vec(V_LAM)) * rg)
        a_l = jnp.exp(log_a)
        th = jnp.tanh(log_a)
        b_l = jnp.sqrt(-2.0 * th / (1.0 - th)) * (ig * u_l)
        yield
        step = 1
        while step < CHUNK:
            b_l = b_l + a_l * _scan_shift(b_l, step, 0.0, reverse)
            a_l = a_l * _scan_shift(a_l, step, 1.0, reverse)
            step *= 2
            yield
        lru["h"] = b_l + a_l * _rows_to_groups([h_ref[b] for b in range(nb)])

    lru_gen = lru_pieces()

    def lru_advance():
        next(lru_gen, None)

    blk_of = lambda x, w: x >> (w.bit_length() - 1)
    same = {w: blk_of(row_h, w) == blk_of(col_h, w) for w in (INV_BASE, 2 * INV_BASE, 4 * INV_BASE)}
    d_m = [jnp.where(same[INV_BASE], m, 0.0) for m in a_m]
    t_m = [eye_h + m for m in d_m]
    pw_sp = [_split2(_mm3(ps, _bd(ps))) for ps in [_split2(m) for m in d_m]]
    lru_advance()
    prod = [_mm3(_cat([ps, _split2(t)], 0), _bd(ps)) for ps, t in zip(pw_sp, t_m)]
    pw_sp = [_split2(p[0:CHUNK]) for p in prod]
    t_m = [t + p[CHUNK:2 * CHUNK] for t, p in zip(t_m, prod)]
    lru_advance()
    t_m = [t + _mm3(_split2(t), _bd(ps)) for t, ps in zip(t_m, pw_sp)]
    lru_advance()
    width = INV_BASE
    while width < CHUNK:
        if 2 * width < CHUNK:
            off = same[2 * width] & jnp.logical_not(same[width])
        else:
            off = jnp.logical_not(same[width])
        e_sp = [_split2(jnp.where(off, m, 0.0)) for m in a_m]
        t_sp = [_split2(t) for t in t_m]
        x_sp = [_split2(_mm3(e, _bd(ts))) for e, ts in zip(e_sp, t_sp)]
        t_m = [t + _mm3(ts, _bd(x)) for t, ts, x in zip(t_m, t_sp, x_sp)]
        lru_advance()
        width *= 2
    u_sp = [_split2(_mm3(_split2(t), _bd(_split2(w)))) for t, w in zip(t_m, w_m)]
    lru_advance()
    y_rw = [z_r[u] + jnp.dot(
        jnp.concatenate([p_hi[u], q_hi[u]], axis=1),
        jnp.concatenate([_blockdiag(u_sp[u][0]), _blockdiag(v_sp[u][0])], axis=0),
        preferred_element_type=F32) for u in range(len(units))]
    lru_advance()
    s_upd = []
    for u in range(len(units)):
        dgam = jnp.where(eye_p, jnp.broadcast_to(gam[u], (PAIR, PAIR)), 0.0)
        s_new = _mm3(_cat([bg_sp[u], kg_sp[u], _split2(dgam)], 0),
                     _cat([u_sp[u], v_sp[u], s_sp[u]], 0), _TN)
        s_upd.append(jnp.where(same_head, s_new, 0.0))

    if not reverse:
        y_out = y_rw
    else:
        y_out = []
        for u, ((gr_, sl), (b, hp)) in enumerate(zip(pos, units)):
            ysum = y_rw[u] + f_in[gr_, sl]
            mean = _dot_ones_rhs(ysum, head_ones) * (1.0 / HEAD)
            cen = ysum - mean
            var = _dot_ones_rhs(cen * cen, head_ones) * (1.0 / HEAD)
            gn = cen * lax.rsqrt(var + GN_EPS) * vec(V_LNG, sl) + vec(V_LNB, sl)
            y_out.append((gn + (f_in[gr_, RW + PAIR * hp:RW + PAIR * (hp + 1)] + bonus[gr_, sl])) * g[gr_, sl])

    y_base = CONV_W if reverse else 0
    for u, (b, hp) in enumerate(units):
        s_ref[b, hp] = s_upd[u]
        out_ref[b, :, y_base + PAIR * hp:y_base + PAIR * (hp + 1)] = y_out[u]
    if not reverse:
        for b in range(nb):
            out_ref[b, :, RW:2 * RW] = bonus[grp[b]]

    for _ in lru_gen:
        pass
    h = lru["h"]
    for b in range(nb):
        hb = h[grp[b]]
        h_ref[b] = hb[0:1, :] if reverse else hb[CHUNK - 1:CHUNK, :]

    if not reverse:
        for b in range(nb):
            out_ref[b, :, 2 * RW:3 * RW] = h[grp[b]]
    else:
        gr = stacked(p_ref, OFF_C + RW, OFF_C + 2 * RW)
        y_lru = jax.nn.gelu(gr) * (f_in[:, 2 * RW:3 * RW] + h)
        bgate = stacked(p_ref, 0, CONV_W)
        uc = stacked(p_ref, CONV_W, 2 * CONV_W) * stacked(p_ref, 2 * CONV_W, 3 * CONV_W)
        if line_is_chunk:
            zero = jnp.zeros((nb * HALO, CONV_W), F32)
            up, un = zero, zero
        else:
            up = stacked(pp_ref, CONV_W, 2 * CONV_W) * stacked(pp_ref, 2 * CONV_W, 3 * CONV_W) * keep_prev
            un = stacked(pn_ref, CONV_W, 2 * CONV_W) * stacked(pn_ref, 2 * CONV_W, 3 * CONV_W) * keep_next
        yc = bgate * (conva_ref[0:1, :] * _shift_dn(uc, up, 1) + conva_ref[1:2, :] * uc
                      + conva_ref[2:3, :] * _shift_up(uc, un, 1))
        for b in range(nb):
            out_ref[b, :, CONV_W + RW:CONV_W + 2 * RW] = y_lru[grp[b]]
            out_ref[b, :, 0:CONV_W] = yc[grp[b]]


def _mixer_pass(p, f_in, s0, h0, prm, d, col_major, line_is_chunk):
    bsz = p.shape[0]
    t_len = p.shape[1] * p.shape[2] // P_COLS
    n = t_len // CHUNK
    reverse = d == 1
    out_cols = 4 * CONV_W if reverse else F_COLS

    def ci(i):
        return (n - 1 - i) if reverse else i

    if col_major:
        assert p.shape == (bsz, CHUNK, GRID_W * P_COLS) and n == GRID_W
        p_view = p
        cur_spec = pl.BlockSpec((MIX_NB, CHUNK,P_COLS), lambda b, i: (b, 0, ci(i)))
        prev_spec = pl.BlockSpec((MIX_NB, HALO,P_COLS),
                                 lambda b, i: (b, CHUNK // HALO - 1, jnp.maximum(ci(i) - 1, 0)))
        next_spec = pl.BlockSpec((MIX_NB, HALO,P_COLS), lambda b, i: (b, 0, jnp.minimum(ci(i) + 1, n - 1)))
    else:
        p_view = p
        per = CHUNK // HALO
        cur_spec = pl.BlockSpec((MIX_NB, CHUNK,P_COLS), lambda b, i: (b, ci(i), 0))
        prev_spec = pl.BlockSpec((MIX_NB, HALO,P_COLS), lambda b, i: (b, jnp.maximum(per * ci(i) - 1, 0), 0))
        next_spec = pl.BlockSpec((MIX_NB, HALO,P_COLS),
                                 lambda b, i: (b, jnp.minimum(per * ci(i) + per, per * n - 1), 0))
    full = lambda a: pl.BlockSpec(a.shape, lambda b, i: (0,) * a.ndim)
    in_specs = [cur_spec, prev_spec, next_spec]
    args = [p_view, p_view, p_view]
    if reverse:
        in_specs.append(pl.BlockSpec((MIX_NB, CHUNK,F_COLS), lambda b, i: (b, ci(i), 0)))
        args.append(f_in)
    state_specs = [pl.BlockSpec((MIX_NB,) + s0.shape[1:], lambda b, i: (b, 0, 0, 0)),
                   pl.BlockSpec((MIX_NB, 1, RW), lambda b, i: (b, 0, 0))]
    in_specs += state_specs
    args += [s0, h0]
    params = [prm["vec"][d], prm["mu"], prm["conv_a"], prm["w2"][d], prm["a2"][d], prm["g2"],
              prm["wr"][d], prm["wi"][d]]
    in_specs += [full(a) for a in params]
    args += params

    if reverse and col_major:
        out_shape = jax.ShapeDtypeStruct((bsz, CHUNK, n * out_cols), F32)
        out_spec = pl.BlockSpec((MIX_NB, CHUNK,out_cols), lambda b, i: (b, 0, ci(i)))
    else:
        out_shape = jax.ShapeDtypeStruct((bsz, t_len, out_cols), F32)
        out_spec = pl.BlockSpec((MIX_NB, CHUNK,out_cols), lambda b, i: (b, ci(i), 0))

    out, s_fin, h_fin = pl.pallas_call(
        functools.partial(_mixer_kernel, d=d, n_chunks=n, line_is_chunk=line_is_chunk),
        grid=(bsz // MIX_NB, n),
        in_specs=in_specs,
        out_specs=[out_spec] + state_specs,
        out_shape=[out_shape, jax.ShapeDtypeStruct(s0.shape, F32), jax.ShapeDtypeStruct(h0.shape, F32)],
        compiler_params=pltpu.CompilerParams(
            dimension_semantics=("parallel", "arbitrary"), vmem_limit_bytes=VMEM_LIMIT),
        name="mixer_bwd" if reverse else "mixer_fwd",
    )(*args)
    return out, s_fin, h_fin


def _mix_stream(p, prm, col_major, line_is_chunk, init):
    (s0f, h0f), (s0b, h0b) = init
    f, sf, hf = _mixer_pass(p, None, s0f, h0f, prm, 0, col_major, line_is_chunk)
    y, sb, hb = _mixer_pass(p, f, s0b, h0b, prm, 1, col_major, line_is_chunk)
    return y, ((sf, hf), (sb, hb))


def _mixer_params(l, conv_a, rwkv_mu, rwkv_w0, rwkv_w2, rwkv_a0, rwkv_a2, rwkv_g2, rwkv_kk, rwkv_ka,
                  rwkv_rk, rwkv_lnx_g, rwkv_lnx_b, lru_conv_w, lru_conv_b, lru_w_r, lru_b_r, lru_w_i,
                  lru_b_i, lru_lam):
    def both(a):
        return jnp.broadcast_to(a[None], (2,) + a.shape)

    rows = [rwkv_w0[l], rwkv_a0[l], both(rwkv_kk[l]), both(rwkv_ka[l]), both(rwkv_rk[l]),
            both(rwkv_lnx_g[l]), both(rwkv_lnx_b[l]),
            lru_conv_w[l][:, 0], lru_conv_w[l][:, 1], lru_conv_w[l][:, 2], lru_conv_w[l][:, 3],
            lru_conv_b[l], lru_b_r[l], lru_b_i[l], lru_lam[l], jnp.zeros((2, RW), F32)]
    assert len(rows) == VEC_ROWS
    vec = jnp.stack(rows, axis=1)
    nb = lru_w_r.shape[2]
    eye = jnp.eye(nb, dtype=F32)

    def bd(w):
        return jnp.einsum("dnij,nm->dnimj", w, eye).reshape(2, nb * HEAD, nb * HEAD).astype(BF16)

    return {
        "vec": vec, "mu": rwkv_mu[l], "conv_a": conv_a[l],
        "w2": rwkv_w2[l].astype(BF16), "a2": rwkv_a2[l].astype(BF16), "g2": rwkv_g2[l].astype(BF16),
        "wr": bd(lru_w_r[l]), "wi": bd(lru_w_i[l]),
    }


def kernel(x, c, ctx, c_ctx, w_mod, b_mod, g_ffn1, w_gu1, w_down1, g_mix, w_in, conv_a, rwkv_mu, rwkv_w0, rwkv_w2, rwkv_a0, rwkv_a2, rwkv_g2, rwkv_kk, rwkv_ka, rwkv_rk, rwkv_lnx_g, rwkv_lnx_b, lru_conv_w, lru_conv_b, lru_w_r, lru_b_r, lru_w_i, lru_b_i, lru_lam, w_out, g_ffn2, w_gu2, w_down2, g_final):
    bsz, t_len, d = x.shape
    depth = w_mod.shape[0]
    n_ctx = ctx.shape[1]
    assert w_in.shape[2] == P_COLS and t_len % CHUNK == 0 and n_ctx % CHUNK == 0

    mod_rows = -(-(bsz + 1) // 8) * 8
    cc = jnp.concatenate([c, c_ctx[None, :], jnp.zeros((mod_rows - bsz - 1, d), F32)], axis=0)
    mods = _modulation(cc, w_mod, b_mod)

    zero_state = ((jnp.zeros((bsz, RW // PAIR, PAIR, PAIR), F32), jnp.zeros((bsz, 1, RW), F32)),) * 2
    wgu1, wd1, wgu2, wd2 = (w.astype(BF16) for w in (w_gu1, w_down1, w_gu2, w_down2))
    win, wout = w_in.astype(BF16), w_out.astype(BF16)
    s_lat = x
    s_ctx = ctx.reshape(1, bsz * n_ctx, d)
    for l in range(depth):
        last = l == depth - 1
        m_lat = mods[l, :bsz].reshape(bsz, 9, d)
        m_ctx = mods[l, bsz:bsz + 1].reshape(1, 9, d)
        prm = _mixer_params(l, conv_a, rwkv_mu, rwkv_w0, rwkv_w2, rwkv_a0, rwkv_a2, rwkv_g2, rwkv_kk,
                            rwkv_ka, rwkv_rk, rwkv_lnx_g, rwkv_lnx_b, lru_conv_w, lru_conv_b, lru_w_r,
                            lru_b_r, lru_w_i, lru_b_i, lru_lam)

        s_lat = _ffn(s_lat, m_lat, 0, g_ffn1[l], wgu1, wd1, l)
        s_ctx = _ffn(s_ctx, m_ctx, 0, g_ffn1[l], wgu1, wd1, l)
        col_major = l % 2 == 1
        p_lat = _inproj(s_lat, m_lat, 3, g_mix[l], win, l, col_major=col_major)
        p_ctx = _inproj(s_ctx, m_ctx, 3, g_mix[l], win, l)

        y_ctx, ctx_fin = _mix_stream(p_ctx.reshape(bsz, n_ctx, P_COLS), prm, False, False, zero_state)
        y_lat, _ = _mix_stream(p_lat, prm, col_major, True, ctx_fin)

        s_lat = _ffn(s_lat, m_lat, 6, g_ffn2[l], wgu2, wd2, l, y=y_lat, wout=wout, y_col_major=col_major,
                     gfin=g_final if last else None)
        if not last:
            s_ctx = _ffn(s_ctx, m_ctx, 6, g_ffn2[l], wgu2, wd2, l,
                         y=y_ctx.reshape(1, bsz * n_ctx, 4 * CONV_W), wout=wout)
    return s_lat
```

```python
import functools

import jax
import jax.numpy as jnp
from jax import lax
from jax.experimental import pallas as pl
from jax.experimental.pallas import tpu as pltpu

F32 = jnp.float32
BF16 = jnp.bfloat16

V7X_VMEM_BYTES = 64 * 1024 * 1024
V7X_MXU_DIM = 256
VMEM_LIMIT = V7X_VMEM_BYTES * 7 // 8

HEAD = 64
PAIR = 2 * HEAD
CHUNK = 64
HALO = 8
INV_BASE = 8
MIX_NB = 8
TOKEN_TILE = 512
MOD_COL_TILES = 8
GRID_W = 64
NORM_EPS = 1e-6
GN_EPS = 64e-5
LRU_C = 8.0
DECAY_SCALE = 0.6065306597126334


def _dot(a, b):
    return jnp.dot(a.astype(BF16), b.astype(BF16), preferred_element_type=F32)


def _split2(a):
    hi = a.astype(BF16)
    lo = (a - hi.astype(F32)).astype(BF16)
    return hi, lo


_NN = (((1,), (0,)), ((), ()))
_NT = (((1,), (1,)), ((), ()))
_TN = (((0,), (0,)), ((), ()))


def _mm3(a, b, dims=_NN):
    ah, al = a
    bh, bl = b
    d = functools.partial(lax.dot_general, dimension_numbers=dims, preferred_element_type=F32)
    free = 1 if dims == _TN else 0
    m = ah.shape[free]
    both = d(jnp.concatenate([ah, al], axis=free), bh)
    return both[:m] + (d(ah, bl) + both[m:])


def _mm3_and_1(a, c_hi, b, dims):
    ah, al = a
    bh, bl = b
    d = functools.partial(lax.dot_general, dimension_numbers=dims, preferred_element_type=F32)
    m = ah.shape[0]
    both = d(jnp.concatenate([ah, al, c_hi], axis=0), bh)
    return both[:m] + (d(ah, bl) + both[m:2 * m]), both[2 * m:]


def _cat(parts, axis):
    return tuple(jnp.concatenate([p[i] for p in parts], axis=axis) for i in range(2))


def _bd(sp):
    return tuple(_blockdiag(x) for x in sp)


def _dot_ones_rhs(a, ones_rhs):
    ah, al = _split2(a)
    d = functools.partial(jnp.dot, preferred_element_type=F32)
    return d(ah, ones_rhs) + d(al, ones_rhs)


def _dot_exact_lhs(ones_lhs, b):
    b1 = b.astype(BF16)
    r1 = b - b1.astype(F32)
    b2 = r1.astype(BF16)
    b3 = (r1 - b2.astype(F32)).astype(BF16)
    d = functools.partial(jnp.dot, preferred_element_type=F32)
    return d(ones_lhs, b1) + (d(ones_lhs, b2) + d(ones_lhs, b3))


def _softplus(x):
    return jnp.maximum(x, 0.0) + jnp.log1p(jnp.exp(-jnp.abs(x)))


def _rms_modulate(s, g, shift, scale):
    y = s * lax.rsqrt(jnp.mean(s * s, axis=-1, keepdims=True) + NORM_EPS)
    return (y * g) * (1.0 + scale) + shift


def _per_group(fn, x, halo, *args):
    nb = x.shape[0] // CHUNK
    if nb == 1:
        return fn(x, halo, *args)
    return jnp.concatenate(
        [fn(x[b * CHUNK:(b + 1) * CHUNK], None if halo is None else halo[b * HALO:(b + 1) * HALO], *args)
         for b in range(nb)], axis=0)


def _shift_dn(x, prev, k):
    return _per_group(_shift_dn1, x, prev, k)


def _shift_up(x, nxt, k):
    return _per_group(_shift_up1, x, nxt, k)


def _scan_shift(x, k, fill, reverse):
    return _per_group(_scan_shift1, x, None, k, fill, reverse)


def _rows_to_groups(rows):
    return jnp.concatenate([jnp.broadcast_to(r, (CHUNK, r.shape[1])) for r in rows], axis=0)


def _shift_dn1(x, prev, k):
    ch = x.shape[1]
    rolled = pltpu.roll(x, k, axis=0)
    pr = pltpu.roll(prev, k, axis=0)
    r8 = lax.broadcasted_iota(jnp.int32, (HALO, ch), 0)
    head = jnp.where(r8 < k, pr, rolled[:HALO])
    return jnp.concatenate([head, rolled[HALO:]], axis=0)


def _shift_up1(x, nxt, k):
    n, ch = x.shape
    rolled = pltpu.roll(x, n - k, axis=0)
    nr = pltpu.roll(nxt, HALO - k, axis=0)
    r8 = lax.broadcasted_iota(jnp.int32, (HALO, ch), 0)
    tail = jnp.where(r8 >= HALO - k, nr, rolled[n - HALO:])
    return jnp.concatenate([rolled[:n - HALO], tail], axis=0)


def _scan_shift1(x, _, k, fill, reverse):
    n, ch = x.shape
    rows = lax.broadcasted_iota(jnp.int32, (n, ch), 0)
    if not reverse:
        return jnp.where(rows >= k, pltpu.roll(x, k, axis=0), fill)
    return jnp.where(rows < n - k, pltpu.roll(x, n - k, axis=0), fill)


def _blockdiag(xp):
    n = xp.shape[0]
    lane = lax.broadcasted_iota(jnp.int32, (n, PAIR), 1)
    zero = jnp.zeros_like(xp)
    top = jnp.where(lane < HEAD, xp, zero)
    bot = jnp.where(lane >= HEAD, xp, zero)
    return jnp.concatenate([top, bot], axis=0)


def _mod_kernel(c_ref, w_ref, b_ref, o_ref):
    cc = c_ref[...]
    act = cc * jax.nn.sigmoid(cc)
    o_ref[0] = _dot(act, w_ref[0]) + b_ref[0]


def _modulation(cc, w_mod, b_mod):
    depth, d, n = w_mod.shape
    rows = cc.shape[0]
    tn = n // MOD_COL_TILES
    return pl.pallas_call(
        _mod_kernel,
        grid=(depth, n // tn),
        in_specs=[
            pl.BlockSpec((rows, d), lambda l, j: (0, 0)),
            pl.BlockSpec((1, d, tn), lambda l, j: (l, 0, j)),
            pl.BlockSpec((1, 1, tn), lambda l, j: (l, 0, j)),
        ],
        out_specs=pl.BlockSpec((1, rows, tn), lambda l, j: (l, 0, j)),
        out_shape=jax.ShapeDtypeStruct((depth, rows, n), F32),
        compiler_params=pltpu.CompilerParams(
            dimension_semantics=("parallel", "parallel"), vmem_limit_bytes=VMEM_LIMIT),
        name="modulation",
    )(cc, w_mod, b_mod.reshape(depth, 1, n))


def _ffn_kernel(*refs, mod_base, has_pre, y_col_major, final_norm, d_ff, tf):
    it = iter(refs)
    s_ref = next(it)
    if has_pre:
        y_ref = next(it)
        wout_ref = next(it)
        if y_col_major:
            perm_ref = next(it)
    mod_ref = next(it)
    g_ref = next(it)
    wgu_ref = next(it)
    wdown_ref = next(it)
    if final_norm:
        gfin_ref = next(it)
    o_ref = next(it)
    acc_ref = next(it)

    s = s_ref[0]
    if has_pre:
        gate_mix = mod_ref[0, mod_base - 1:mod_base, :]
        if y_col_major:
            mix = wout_ref.shape[1]
            yc = jnp.concatenate([y_ref[0, :, c * mix:(c + 1) * mix] for c in range(GRID_W)], axis=0)
            yb = jnp.dot(perm_ref[...], yc.astype(BF16), preferred_element_type=F32).astype(BF16)
        else:
            yb = y_ref[0].astype(BF16)
        s = s + gate_mix * jnp.dot(yb, wout_ref[0], preferred_element_type=F32)
    shift = mod_ref[0, mod_base:mod_base + 1, :]
    scale = mod_ref[0, mod_base + 1:mod_base + 2, :]
    gate = mod_ref[0, mod_base + 2:mod_base + 3, :]
    hb = _rms_modulate(s, g_ref[...], shift, scale).astype(BF16)
    for j in range(d_ff // tf):
        gt = jnp.dot(hb, wgu_ref[0, :, j * tf:(j + 1) * tf], preferred_element_type=F32)
        up = jnp.dot(hb, wgu_ref[0, :, d_ff + j * tf:d_ff + (j + 1) * tf], preferred_element_type=F32)
        act = ((gt * jax.nn.sigmoid(gt)) * up).astype(BF16)
        part = jnp.dot(act, wdown_ref[0, j * tf:(j + 1) * tf, :], preferred_element_type=F32)
        if j == 0:
            acc_ref[...] = part
        else:
            acc_ref[...] += part
    out = s + (0.5 * gate) * acc_ref[...]
    if final_norm:
        out = out * lax.rsqrt(jnp.mean(out * out, axis=-1, keepdims=True) + NORM_EPS) * gfin_ref[...]
    o_ref[0] = out


def _layer_spec(w, layer):
    return pl.BlockSpec((1,) + w.shape[1:], lambda b, i: (layer, 0, 0))


def _ffn(s, mod, mod_base, g, wgu, wdown, layer, y=None, wout=None, y_col_major=False, gfin=None):
    bv, tv, d = s.shape
    d_ff = wdown.shape[1]
    tf = V7X_MXU_DIM
    tm = min(TOKEN_TILE, tv)
    has_pre = y is not None
    final_norm = gfin is not None
    const = lambda b, i: (0, 0)
    tile = lambda b, i: (b, i, 0)
    in_specs = [pl.BlockSpec((1, tm, d), tile)]
    args = [s]
    if has_pre:
        y_rows = tm // GRID_W if y_col_major else tm
        in_specs += [pl.BlockSpec((1, y_rows, y.shape[2]), tile), _layer_spec(wout, layer)]
        args += [y, wout]
        if y_col_major:
            src = (jnp.arange(tm) % GRID_W) * y_rows + jnp.arange(tm) // GRID_W
            in_specs.append(pl.BlockSpec((tm, tm), const))
            args.append(jax.nn.one_hot(src, tm, dtype=BF16))
    in_specs += [
        pl.BlockSpec((1, mod.shape[1], d), lambda b, i: (b, 0, 0)),
        pl.BlockSpec((1, d), const),
        _layer_spec(wgu, layer),
        _layer_spec(wdown, layer),
    ]
    args += [mod, g.reshape(1, d), wgu, wdown]
    if final_norm:
        in_specs.append(pl.BlockSpec((1, d), const))
        args.append(gfin.reshape(1, d))
    body = functools.partial(_ffn_kernel, mod_base=mod_base, has_pre=has_pre, y_col_major=y_col_major,
                             final_norm=final_norm, d_ff=d_ff, tf=tf)
    return pl.pallas_call(
        body,
        grid=(bv, tv // tm),
        in_specs=in_specs,
        out_specs=pl.BlockSpec((1, tm, d), tile),
        out_shape=jax.ShapeDtypeStruct(s.shape, F32),
        scratch_shapes=[pltpu.VMEM((tm, d), F32)],
        compiler_params=pltpu.CompilerParams(
            dimension_semantics=("parallel", "parallel"), vmem_limit_bytes=VMEM_LIMIT),
        name="ffn",
    )(*args)


def _inproj_kernel(s_ref, mod_ref, g_ref, w_ref, *rest, mod_base, col_major):
    shift = mod_ref[0, mod_base:mod_base + 1, :]
    scale = mod_ref[0, mod_base + 1:mod_base + 2, :]
    hb = _rms_modulate(s_ref[0], g_ref[...], shift, scale).astype(BF16)
    if not col_major:
        o_ref, = rest
        o_ref[0] = jnp.dot(hb, w_ref[0], preferred_element_type=F32)
        return
    perm_ref, o_ref = rest
    hp = jnp.dot(perm_ref[...], hb, preferred_element_type=F32).astype(BF16)
    res = jnp.dot(hp, w_ref[0], preferred_element_type=F32)
    rows = res.shape[0] // GRID_W
    n = res.shape[1]
    for c in range(GRID_W):
        o_ref[0, :, c * n:(c + 1) * n] = res[c * rows:(c + 1) * rows, :]


def _inproj(s, mod, mod_base, g, w, layer, col_major=False):
    bv, tv, d = s.shape
    tm = min(TOKEN_TILE, tv)
    n = w.shape[2]
    const = lambda b, i: (0, 0)
    in_specs = [
        pl.BlockSpec((1, tm, d), lambda b, i: (b, i, 0)),
        pl.BlockSpec((1, mod.shape[1], d), lambda b, i: (b, 0, 0)),
        pl.BlockSpec((1, d), const),
        _layer_spec(w, layer),
    ]
    args = [s, mod, g.reshape(1, d), w]
    if col_major:
        rows = tm // GRID_W
        out_spec = pl.BlockSpec((1, rows, GRID_W * n), lambda b, i: (b, i, 0))
        out_shape = jax.ShapeDtypeStruct((bv, tv // GRID_W, GRID_W * n), F32)
        src = (jnp.arange(tm) % rows) * GRID_W + jnp.arange(tm) // rows
        in_specs.append(pl.BlockSpec((tm, tm), const))
        args.append(jax.nn.one_hot(src, tm, dtype=BF16))
    else:
        out_spec = pl.BlockSpec((1, tm, n), lambda b, i: (b, i, 0))
        out_shape = jax.ShapeDtypeStruct((bv, tv, n), F32)
    return pl.pallas_call(
        functools.partial(_inproj_kernel, mod_base=mod_base, col_major=col_major),
        grid=(bv, tv // tm),
        in_specs=in_specs,
        out_specs=out_spec,
        out_shape=out_shape,
        compiler_params=pltpu.CompilerParams(
            dimension_semantics=("parallel", "parallel"), vmem_limit_bytes=VMEM_LIMIT),
        name="inproj",
    )(*args)


CONV_W = 256
RW = 384
OFF_B = 3 * CONV_W
RWKV_COLS = 3 * RW + 64 + 64 + 128
OFF_C = OFF_B + RWKV_COLS
P_COLS = OFF_C + 2 * RW
F_PB = 3 * RW
F_KK = F_PB + RWKV_COLS
F_COLS = F_KK + RW
V_W0, V_A0, V_KK, V_KA, V_RK, V_LNG, V_LNB, V_CW, V_CB, V_BR, V_BI, V_LAM = 0, 1, 2, 3, 4, 5, 6, 7, 11, 12, 13, 14
VEC_ROWS = 16


def _mixer_kernel(*refs, d, n_chunks, line_is_chunk):
    it = iter(refs)
    p_ref, pp_ref, pn_ref = next(it), next(it), next(it)
    if d == 1:
        f_in_ref = next(it)
    s0_ref, h0_ref = next(it), next(it)
    vec_ref, mu_ref, conva_ref = next(it), next(it), next(it)
    w2_ref, a2_ref, g2_ref, wr_ref, wi_ref = next(it), next(it), next(it), next(it), next(it)
    out_ref, s_ref, h_ref = next(it), next(it), next(it)

    reverse = d == 1
    i = pl.program_id(1)
    ci = (n_chunks - 1 - i) if reverse else i
    keep_prev = jnp.where(ci == 0, 0.0, 1.0)
    keep_next = jnp.where(ci == n_chunks - 1, 0.0, 1.0)

    @pl.when(i == 0)
    def _():
        s_ref[...] = s0_ref[...]
        h_ref[...] = h0_ref[...]

    def vec(r, sl=slice(None)):
        return vec_ref[r:r + 1, sl]

    nb = p_ref.shape[0]
    grp = [slice(b * CHUNK, (b + 1) * CHUNK) for b in range(nb)]

    def stacked(ref, c0, c1):
        return jnp.concatenate([ref[b, :, c0:c1] for b in range(nb)], axis=0)

    lane_p = lax.broadcasted_iota(jnp.int32, (PAIR, PAIR), 1)
    row_p = lax.broadcasted_iota(jnp.int32, (PAIR, PAIR), 0)
    same_head = (lane_p >= HEAD) == (row_p >= HEAD)
    head_ones = jnp.where(same_head, 1.0, 0.0).astype(BF16)

    def head_sum(x):
        return jnp.concatenate(
            [_dot_ones_rhs(x[:, PAIR * hp:PAIR * (hp + 1)], head_ones) for hp in range(RW // PAIR)],
            axis=1)

    row_c = lax.broadcasted_iota(jnp.int32, (CHUNK, CHUNK), 0)
    col_c = lax.broadcasted_iota(jnp.int32, (CHUNK, CHUNK), 1)
    tri = jnp.where((col_c >= row_c) if reverse else (col_c <= row_c), 1.0, 0.0).astype(BF16)
    row_h = lax.broadcasted_iota(jnp.int32, (CHUNK, PAIR), 0)
    col_h = lax.broadcasted_iota(jnp.int32, (CHUNK, PAIR), 1) & (HEAD - 1)
    strict = (col_h > row_h) if reverse else (col_h < row_h)
    incl = (col_h >= row_h) if reverse else (col_h <= row_h)
    eye_h = jnp.where(col_h == row_h, 1.0, 0.0)
    eye_p = lane_p == row_p
    n_pair = RW // PAIR

    def blk(sp, gs):
        return tuple(x[gs[0], gs[1]] for x in sp)

    def rwkv_front(bs):
        gl = [slice(i * CHUNK, (i + 1) * CHUNK) for i in range(len(bs))]

        def st(ref, c0, c1):
            return jnp.concatenate([ref[b, :, c0:c1] for b in bs], axis=0)

        o = {}
        if reverse:
            f_in = st(f_in_ref, 0, F_COLS)
            pb = f_in[:, F_PB:F_PB + RWKV_COLS]
            kk = f_in[:, F_KK:F_KK + RW]
            o["f_in"] = f_in
        else:
            cur = st(p_ref, OFF_B, OFF_C)
            prev = st(pp_ref, OFF_B, OFF_C) * keep_prev
            nxt = st(pn_ref, OFF_B, OFF_C) * keep_next
            pb = (cur + mu_ref[0:1, :] * (_shift_dn(cur, prev, 1) - cur)
                  + mu_ref[1:2, :] * (_shift_up(cur, nxt, 1) - cur))
        r = pb[:, 0:RW]
        k = pb[:, RW:2 * RW]
        v = pb[:, 2 * RW:3 * RW]
        dw = pb[:, 3 * RW:3 * RW + 64]
        da = pb[:, 3 * RW + 64:3 * RW + 128]
        dg = pb[:, 3 * RW + 128:3 * RW + 256]
        if not reverse:
            kkr = k * vec(V_KK)
            kk = kkr * lax.rsqrt(head_sum(kkr * kkr) + 1e-12)
            for i, b in enumerate(bs):
                out_ref[b, :, F_PB:F_PB + RWKV_COLS] = pb[gl[i]]
                out_ref[b, :, F_KK:F_KK + RW] = kk[gl[i]]
        lw = -DECAY_SCALE * jax.nn.sigmoid(vec(V_W0) + _dot(jnp.tanh(dw), w2_ref[...]))
        a = jax.nn.sigmoid(vec(V_A0) + _dot(da, a2_ref[...]))
        kd = k * (1.0 + (a - 1.0) * vec(V_KA))
        beta = kk * a
        o["bonus"] = head_sum(r * kd * vec(V_RK)) * v
        if reverse:
            o["g"] = _dot(jax.nn.sigmoid(dg), g2_ref[...])

        c_inc = jnp.concatenate([_dot_exact_lhs(tri, lw[g]) for g in gl], axis=0)
        c_exc = c_inc - lw
        c_tot = [c_inc[g.start:g.start + 1, :] if reverse else c_inc[g.stop - 1:g.stop, :] for g in gl]
        e_neg = jnp.exp(-c_inc)
        gam = [jnp.exp(t) for t in c_tot]
        gam_rows = _rows_to_groups(gam)
        at = -(kk * jnp.exp(c_exc))
        rt = r * jnp.exp(c_inc)
        bh = beta * e_neg
        kh = kd * e_neg
        bg = bh * gam_rows
        kg = kh * gam_rows

        o["units"] = [(b, hp) for b in bs for hp in range(n_pair)]
        pos = [(gl[i], slice(PAIR * hp, PAIR * (hp + 1))) for i in range(len(bs)) for hp in range(n_pair)]
        o["gam"] = [gam[i][:, PAIR * hp:PAIR * (hp + 1)] for i in range(len(bs)) for hp in range(n_pair)]
        s_sp = [_split2(s_ref[b, hp]) for b, hp in o["units"]]
        at_sp, bh_sp, kh_sp = _split2(at), _split2(bh), _split2(kh)
        bg_sp, kg_sp, v_sp = _split2(bg), _split2(kg), _split2(v)
        rt_hi = rt.astype(BF16)
        gm = [_mm3_and_1(blk(at_sp, gs), rt_hi[gs[0], gs[1]],
                         _cat([_bd(blk(bh_sp, gs)), _bd(blk(kh_sp, gs))], 0), _NT) for gs in pos]
        o["a_m"] = [jnp.where(strict, m[0][:, 0:PAIR], 0.0) for m in gm]
        b_m = [jnp.where(strict, m[0][:, PAIR:2 * PAIR], 0.0) for m in gm]
        o["p_hi"] = [jnp.where(incl, m[1][:, 0:PAIR], 0.0).astype(BF16) for m in gm]
        o["q_hi"] = [jnp.where(incl, m[1][:, PAIR:2 * PAIR], 0.0).astype(BF16) for m in gm]
        z = [_mm3_and_1(blk(at_sp, gs), rt_hi[gs[0], gs[1]], s_sp[u], _NN) for u, gs in enumerate(pos)]
        o["z_r"] = [zz[1] for zz in z]
        o["w_m"] = [z[u][0] + _mm3(_split2(b_m[u]), _bd(blk(v_sp, gs))) for u, gs in enumerate(pos)]
        o["s_sp"] = s_sp
        o["v_sp"] = [blk(v_sp, gs) for gs in pos]
        o["bg_sp"] = [blk(bg_sp, gs) for gs in pos]
        o["kg_sp"] = [blk(kg_sp, gs) for gs in pos]
        return o

    fr = rwkv_front(list(range(nb)))
    units = fr["units"]
    pos = [(grp[b], slice(PAIR * hp, PAIR * (hp + 1))) for b, hp in units]
    a_m, p_hi, q_hi, z_r, w_m = (fr[k] for k in ("a_m", "p_hi", "q_hi", "z_r", "w_m"))
    s_sp, v_sp, bg_sp, kg_sp, gam = (fr[k] for k in ("s_sp", "v_sp", "bg_sp", "kg_sp", "gam"))
    bonus = fr["bonus"]
    if reverse:
        f_in, g = fr["f_in"], fr["g"]

    def rglru():
        xr = stacked(p_ref, OFF_C, OFF_C + RW)
        if not reverse:
            xh = stacked(pp_ref, OFF_C, OFF_C + RW) * keep_prev
            u_l = (vec(V_CW + 3) * xr + vec(V_CW + 2) * _shift_dn(xr, xh, 1)
                   + vec(V_CW + 1) * _shift_dn(xr, xh, 2) + vec(V_CW) * _shift_dn(xr, xh, 3))
        else:
            xh = stacked(pn_ref, OFF_C, OFF_C + RW) * keep_next
            u_l = (vec(V_CW) * xr + vec(V_CW + 1) * _shift_up(xr, xh, 1)
                   + vec(V_CW + 2) * _shift_up(xr, xh, 2) + vec(V_CW + 3) * _shift_up(xr, xh, 3))
        u_l = u_l + vec(V_CB)
        rg = jax.nn.sigmoid(_dot(u_l, wr_ref[...]) + vec(V_BR))
        ig = jax.nn.sigmoid(_dot(u_l, wi_ref[...]) + vec(V_BI))
        log_a = -LRU_C * (_softplus(-vec(V_LAM)) * rg)
        a_l = jnp.exp(log_a)
        th = jnp.tanh(log_a)
        b_l = jnp.sqrt(-2.0 * th / (1.0 - th)) * (ig * u_l)
        step = 1
        while step < CHUNK:
            b_l = b_l + a_l * _scan_shift(b_l, step, 0.0, reverse)
            a_l = a_l * _scan_shift(a_l, step, 1.0, reverse)
            step *= 2
        return b_l + a_l * _rows_to_groups([h_ref[b] for b in range(nb)])

    assert INV_BASE == 8
    blk_of = lambda x, w: x >> (w.bit_length() - 1)
    same = {w: blk_of(row_h, w) == blk_of(col_h, w) for w in (INV_BASE, 2 * INV_BASE, 4 * INV_BASE)}
    d_m = [jnp.where(same[INV_BASE], m, 0.0) for m in a_m]
    t_m = [eye_h + m for m in d_m]
    pw_sp = [_split2(_mm3(ps, _bd(ps))) for ps in [_split2(m) for m in d_m]]
    prod = [_mm3(_cat([ps, _split2(t)], 0), _bd(ps)) for ps, t in zip(pw_sp, t_m)]
    pw_sp = [_split2(p[0:CHUNK]) for p in prod]
    t_m = [t + p[CHUNK:2 * CHUNK] for t, p in zip(t_m, prod)]
    t_m = [t + _mm3(_split2(t), _bd(ps)) for t, ps in zip(t_m, pw_sp)]
    width = INV_BASE
    while width < CHUNK:
        if 2 * width < CHUNK:
            off = same[2 * width] & jnp.logical_not(same[width])
        else:
            off = jnp.logical_not(same[width])
        e_sp = [_split2(jnp.where(off, m, 0.0)) for m in a_m]
        t_sp = [_split2(t) for t in t_m]
        x_sp = [_split2(_mm3(e, _bd(ts))) for e, ts in zip(e_sp, t_sp)]
        t_m = [t + _mm3(ts, _bd(x)) for t, ts, x in zip(t_m, t_sp, x_sp)]
        width *= 2
    u_sp = [_split2(_mm3(_split2(t), _bd(_split2(w)))) for t, w in zip(t_m, w_m)]
    y_rw = [z_r[u] + jnp.dot(
        jnp.concatenate([p_hi[u], q_hi[u]], axis=1),
        jnp.concatenate([_blockdiag(u_sp[u][0]), _blockdiag(v_sp[u][0])], axis=0),
        preferred_element_type=F32) for u in range(len(units))]
    s_upd = []
    for u in range(len(units)):
        dgam = jnp.where(eye_p, jnp.broadcast_to(gam[u], (PAIR, PAIR)), 0.0)
        s_new = _mm3(_cat([bg_sp[u], kg_sp[u], _split2(dgam)], 0),
                     _cat([u_sp[u], v_sp[u], s_sp[u]], 0), _TN)
        s_upd.append(jnp.where(same_head, s_new, 0.0))

    if not reverse:
        y_out = y_rw
    else:
        y_out = []
        for u, ((gr_, sl), (b, hp)) in enumerate(zip(pos, units)):
            ysum = y_rw[u] + f_in[gr_, sl]
            mean = _dot_ones_rhs(ysum, head_ones) * (1.0 / HEAD)
            cen = ysum - mean
            var = _dot_ones_rhs(cen * cen, head_ones) * (1.0 / HEAD)
            gn = cen * lax.rsqrt(var + GN_EPS) * vec(V_LNG, sl) + vec(V_LNB, sl)
            y_out.append((gn + (f_in[gr_, RW + PAIR * hp:RW + PAIR * (hp + 1)] + bonus[gr_, sl])) * g[gr_, sl])

    y_base = CONV_W if reverse else 0
    for u, (b, hp) in enumerate(units):
        s_ref[b, hp] = s_upd[u]
        out_ref[b, :, y_base + PAIR * hp:y_base + PAIR * (hp + 1)] = y_out[u]
    if not reverse:
        for b in range(nb):
            out_ref[b, :, RW:2 * RW] = bonus[grp[b]]

    h = rglru()
    for b in range(nb):
        hb = h[grp[b]]
        h_ref[b] = hb[0:1, :] if reverse else hb[CHUNK - 1:CHUNK, :]

    if not reverse:
        for b in range(nb):
            out_ref[b, :, 2 * RW:3 * RW] = h[grp[b]]
    else:
        gr = stacked(p_ref, OFF_C + RW, OFF_C + 2 * RW)
        y_lru = jax.nn.gelu(gr) * (f_in[:, 2 * RW:3 * RW] + h)
        bgate = stacked(p_ref, 0, CONV_W)
        uc = stacked(p_ref, CONV_W, 2 * CONV_W) * stacked(p_ref, 2 * CONV_W, 3 * CONV_W)
        if line_is_chunk:
            zero = jnp.zeros((nb * HALO, CONV_W), F32)
            up, un = zero, zero
        else:
            up = stacked(pp_ref, CONV_W, 2 * CONV_W) * stacked(pp_ref, 2 * CONV_W, 3 * CONV_W) * keep_prev
            un = stacked(pn_ref, CONV_W, 2 * CONV_W) * stacked(pn_ref, 2 * CONV_W, 3 * CONV_W) * keep_next
        yc = bgate * (conva_ref[0:1, :] * _shift_dn(uc, up, 1) + conva_ref[1:2, :] * uc
                      + conva_ref[2:3, :] * _shift_up(uc, un, 1))
        for b in range(nb):
            out_ref[b, :, CONV_W + RW:CONV_W + 2 * RW] = y_lru[grp[b]]
            out_ref[b, :, 0:CONV_W] = yc[grp[b]]


def _mixer_pass(p, f_in, s0, h0, prm, d, col_major, line_is_chunk):
    bsz = p.shape[0]
    t_len = p.shape[1] * p.shape[2] // P_COLS
    n = t_len // CHUNK
    reverse = d == 1
    out_cols = 4 * CONV_W if reverse else F_COLS

    def ci(i):
        return (n - 1 - i) if reverse else i

    if col_major:
        assert p.shape == (bsz, CHUNK, GRID_W * P_COLS) and n == GRID_W
        p_view = p
        cur_spec = pl.BlockSpec((MIX_NB, CHUNK,P_COLS), lambda b, i: (b, 0, ci(i)))
        prev_spec = pl.BlockSpec((MIX_NB, HALO,P_COLS),
                                 lambda b, i: (b, CHUNK // HALO - 1, jnp.maximum(ci(i) - 1, 0)))
        next_spec = pl.BlockSpec((MIX_NB, HALO,P_COLS), lambda b, i: (b, 0, jnp.minimum(ci(i) + 1, n - 1)))
    else:
        p_view = p
        per = CHUNK // HALO
        cur_spec = pl.BlockSpec((MIX_NB, CHUNK,P_COLS), lambda b, i: (b, ci(i), 0))
        prev_spec = pl.BlockSpec((MIX_NB, HALO,P_COLS), lambda b, i: (b, jnp.maximum(per * ci(i) - 1, 0), 0))
        next_spec = pl.BlockSpec((MIX_NB, HALO,P_COLS),
                                 lambda b, i: (b, jnp.minimum(per * ci(i) + per, per * n - 1), 0))
    full = lambda a: pl.BlockSpec(a.shape, lambda b, i: (0,) * a.ndim)
    in_specs = [cur_spec, prev_spec, next_spec]
    args = [p_view, p_view, p_view]
    if reverse:
        in_specs.append(pl.BlockSpec((MIX_NB, CHUNK,F_COLS), lambda b, i: (b, ci(i), 0)))
        args.append(f_in)
    state_specs = [pl.BlockSpec((MIX_NB,) + s0.shape[1:], lambda b, i: (b, 0, 0, 0)),
                   pl.BlockSpec((MIX_NB, 1, RW), lambda b, i: (b, 0, 0))]
    in_specs += state_specs
    args += [s0, h0]
    params = [prm["vec"][d], prm["mu"], prm["conv_a"], prm["w2"][d], prm["a2"][d], prm["g2"],
              prm["wr"][d], prm["wi"][d]]
    in_specs += [full(a) for a in params]
    args += params

    if reverse and col_major:
        out_shape = jax.ShapeDtypeStruct((bsz, CHUNK, n * out_cols), F32)
        out_spec = pl.BlockSpec((MIX_NB, CHUNK,out_cols), lambda b, i: (b, 0, ci(i)))
    else:
        out_shape = jax.ShapeDtypeStruct((bsz, t_len, out_cols), F32)
        out_spec = pl.BlockSpec((MIX_NB, CHUNK,out_cols), lambda b, i: (b, ci(i), 0))

    out, s_fin, h_fin = pl.pallas_call(
        functools.partial(_mixer_kernel, d=d, n_chunks=n, line_is_chunk=line_is_chunk),
        grid=(bsz // MIX_NB, n),
        in_specs=in_specs,
        out_specs=[out_spec] + state_specs,
        out_shape=[out_shape, jax.ShapeDtypeStruct(s0.shape, F32), jax.ShapeDtypeStruct(h0.shape, F32)],
        compiler_params=pltpu.CompilerParams(
            dimension_semantics=("parallel", "arbitrary"), vmem_limit_bytes=VMEM_LIMIT),
        name="mixer_bwd" if reverse else "mixer_fwd",
    )(*args)
    return out, s_fin, h_fin


def _mix_stream(p, prm, col_major, line_is_chunk, init):
    (s0f, h0f), (s0b, h0b) = init
    f, sf, hf = _mixer_pass(p, None, s0f, h0f, prm, 0, col_major, line_is_chunk)
    y, sb, hb = _mixer_pass(p, f, s0b, h0b, prm, 1, col_major, line_is_chunk)
    return y, ((sf, hf), (sb, hb))


def _mixer_params(l, conv_a, rwkv_mu, rwkv_w0, rwkv_w2, rwkv_a0, rwkv_a2, rwkv_g2, rwkv_kk, rwkv_ka,
                  rwkv_rk, rwkv_lnx_g, rwkv_lnx_b, lru_conv_w, lru_conv_b, lru_w_r, lru_b_r, lru_w_i,
                  lru_b_i, lru_lam):
    def both(a):
        return jnp.broadcast_to(a[None], (2,) + a.shape)

    rows = [rwkv_w0[l], rwkv_a0[l], both(rwkv_kk[l]), both(rwkv_ka[l]), both(rwkv_rk[l]),
            both(rwkv_lnx_g[l]), both(rwkv_lnx_b[l]),
            lru_conv_w[l][:, 0], lru_conv_w[l][:, 1], lru_conv_w[l][:, 2], lru_conv_w[l][:, 3],
            lru_conv_b[l], lru_b_r[l], lru_b_i[l], lru_lam[l], jnp.zeros((2, RW), F32)]
    assert len(rows) == VEC_ROWS
    vec = jnp.stack(rows, axis=1)
    nb = lru_w_r.shape[2]
    eye = jnp.eye(nb, dtype=F32)

    def bd(w):
        return jnp.einsum("dnij,nm->dnimj", w, eye).reshape(2, nb * HEAD, nb * HEAD).astype(BF16)

    return {
        "vec": vec, "mu": rwkv_mu[l], "conv_a": conv_a[l],
        "w2": rwkv_w2[l].astype(BF16), "a2": rwkv_a2[l].astype(BF16), "g2": rwkv_g2[l].astype(BF16),
        "wr": bd(lru_w_r[l]), "wi": bd(lru_w_i[l]),
    }


def kernel(x, c, ctx, c_ctx, w_mod, b_mod, g_ffn1, w_gu1, w_down1, g_mix, w_in, conv_a, rwkv_mu, rwkv_w0, rwkv_w2, rwkv_a0, rwkv_a2, rwkv_g2, rwkv_kk, rwkv_ka, rwkv_rk, rwkv_lnx_g, rwkv_lnx_b, lru_conv_w, lru_conv_b, lru_w_r, lru_b_r, lru_w_i, lru_b_i, lru_lam, w_out, g_ffn2, w_gu2, w_down2, g_final):
    bsz, t_len, d = x.shape
    depth = w_mod.shape[0]
    n_ctx = ctx.shape[1]
    assert w_in.shape[2] == P_COLS and t_len == GRID_W * CHUNK and n_ctx % CHUNK == 0
    assert bsz % MIX_NB == 0 and t_len % TOKEN_TILE == 0 and (bsz * n_ctx) % TOKEN_TILE == 0

    mod_rows = -(-(bsz + 1) // 8) * 8
    cc = jnp.concatenate([c, c_ctx[None, :], jnp.zeros((mod_rows - bsz - 1, d), F32)], axis=0)
    mods = _modulation(cc, w_mod, b_mod)

    zero_state = ((jnp.zeros((bsz, RW // PAIR, PAIR, PAIR), F32), jnp.zeros((bsz, 1, RW), F32)),) * 2
    wgu1, wd1, wgu2, wd2 = (w.astype(BF16) for w in (w_gu1, w_down1, w_gu2, w_down2))
    win, wout = w_in.astype(BF16), w_out.astype(BF16)
    s_lat = x
    s_ctx = ctx.reshape(1, bsz * n_ctx, d)
    for l in range(depth):
        last = l == depth - 1
        m_lat = mods[l, :bsz].reshape(bsz, 9, d)
        m_ctx = mods[l, bsz:bsz + 1].reshape(1, 9, d)
        prm = _mixer_params(l, conv_a, rwkv_mu, rwkv_w0, rwkv_w2, rwkv_a0, rwkv_a2, rwkv_g2, rwkv_kk,
                            rwkv_ka, rwkv_rk, rwkv_lnx_g, rwkv_lnx_b, lru_conv_w, lru_conv_b, lru_w_r,
                            lru_b_r, lru_w_i, lru_b_i, lru_lam)

        s_lat = _ffn(s_lat, m_lat, 0, g_ffn1[l], wgu1, wd1, l)
        s_ctx = _ffn(s_ctx, m_ctx, 0, g_ffn1[l], wgu1, wd1, l)
        col_major = l % 2 == 1
        p_lat = _inproj(s_lat, m_lat, 3, g_mix[l], win, l, col_major=col_major)
        p_ctx = _inproj(s_ctx, m_ctx, 3, g_mix[l], win, l)

        y_ctx, ctx_fin = _mix_stream(p_ctx.reshape(bsz, n_ctx, P_COLS), prm, False, False, zero_state)
        y_lat, _ = _mix_stream(p_lat, prm, col_major, True, ctx_fin)

        s_lat = _ffn(s_lat, m_lat, 6, g_ffn2[l], wgu2, wd2, l, y=y_lat, wout=wout, y_col_major=col_major,
                     gfin=g_final if last else None)
        if not last:
            s_ctx = _ffn(s_ctx, m_ctx, 6, g_ffn2[l], wgu2, wd2, l,
                         y=y_ctx.reshape(1, bsz * n_ctx, 4 * CONV_W), wout=wout)
    return s_lat
```

```python
import functools

import jax
import jax.numpy as jnp
from jax import lax
from jax.experimental import pallas as pl
from jax.experimental.pallas import tpu as pltpu

F32 = jnp.float32
BF16 = jnp.bfloat16

V7X_VMEM_BYTES = 64 * 1024 * 1024
V7X_MXU_DIM = 256
VMEM_LIMIT = V7X_VMEM_BYTES * 7 // 8

HEAD = 64
PAIR = 2 * HEAD
CHUNK = 64
HALO = 8
INV_BASE = 8
MIX_NB = 8
TOKEN_TILE = 512
MOD_COL_TILES = 8
GRID_W = 64
NORM_EPS = 1e-6
GN_EPS = 64e-5
LRU_C = 8.0
DECAY_SCALE = 0.6065306597126334


def _dot(a, b):
    return jnp.dot(a.astype(BF16), b.astype(BF16), preferred_element_type=F32)


def _split2(a):
    hi = a.astype(BF16)
    lo = (a - hi.astype(F32)).astype(BF16)
    return hi, lo


_NN = (((1,), (0,)), ((), ()))
_NT = (((1,), (1,)), ((), ()))
_TN = (((0,), (0,)), ((), ()))


def _mm3(a, b, dims=_NN):
    ah, al = a
    bh, bl = b
    d = functools.partial(lax.dot_general, dimension_numbers=dims, preferred_element_type=F32)
    free = 1 if dims == _TN else 0
    m = ah.shape[free]
    both = d(jnp.concatenate([ah, al], axis=free), bh)
    return both[:m] + (d(ah, bl) + both[m:])


def _mm3_and_1(a, c_hi, b, dims):
    ah, al = a
    bh, bl = b
    d = functools.partial(lax.dot_general, dimension_numbers=dims, preferred_element_type=F32)
    m = ah.shape[0]
    both = d(jnp.concatenate([ah, al, c_hi], axis=0), bh)
    return both[:m] + (d(ah, bl) + both[m:2 * m]), both[2 * m:]


def _cat(parts, axis):
    return tuple(jnp.concatenate([p[i] for p in parts], axis=axis) for i in range(2))


def _bd(sp):
    return tuple(_blockdiag(x) for x in sp)


def _dot_ones_rhs(a, ones_rhs):
    return jnp.dot(a.astype(BF16), ones_rhs, preferred_element_type=F32)


def _dot_exact_lhs(ones_lhs, b):
    b1 = b.astype(BF16)
    r1 = b - b1.astype(F32)
    b2 = r1.astype(BF16)
    b3 = (r1 - b2.astype(F32)).astype(BF16)
    d = functools.partial(jnp.dot, preferred_element_type=F32)
    return d(ones_lhs, b1) + (d(ones_lhs, b2) + d(ones_lhs, b3))


def _softplus(x):
    return jnp.maximum(x, 0.0) + jnp.log1p(jnp.exp(-jnp.abs(x)))


def _rms_modulate(s, g, shift, scale):
    y = s * lax.rsqrt(jnp.mean(s * s, axis=-1, keepdims=True) + NORM_EPS)
    return (y * g) * (1.0 + scale) + shift


def _per_group(fn, x, halo, *args):
    nb = x.shape[0] // CHUNK
    if nb == 1:
        return fn(x, halo, *args)
    return jnp.concatenate(
        [fn(x[b * CHUNK:(b + 1) * CHUNK], None if halo is None else halo[b * HALO:(b + 1) * HALO], *args)
         for b in range(nb)], axis=0)


def _shift_dn(x, prev, k):
    return _per_group(_shift_dn1, x, prev, k)


def _shift_up(x, nxt, k):
    return _per_group(_shift_up1, x, nxt, k)


def _scan_shift(x, k, fill, reverse):
    return _per_group(_scan_shift1, x, None, k, fill, reverse)


def _rows_to_groups(rows):
    return jnp.concatenate([jnp.broadcast_to(r, (CHUNK, r.shape[1])) for r in rows], axis=0)


def _shift_dn1(x, prev, k):
    ch = x.shape[1]
    rolled = pltpu.roll(x, k, axis=0)
    pr = pltpu.roll(prev, k, axis=0)
    r8 = lax.broadcasted_iota(jnp.int32, (HALO, ch), 0)
    head = jnp.where(r8 < k, pr, rolled[:HALO])
    return jnp.concatenate([head, rolled[HALO:]], axis=0)


def _shift_up1(x, nxt, k):
    n, ch = x.shape
    rolled = pltpu.roll(x, n - k, axis=0)
    nr = pltpu.roll(nxt, HALO - k, axis=0)
    r8 = lax.broadcasted_iota(jnp.int32, (HALO, ch), 0)
    tail = jnp.where(r8 >= HALO - k, nr, rolled[n - HALO:])
    return jnp.concatenate([rolled[:n - HALO], tail], axis=0)


def _scan_shift1(x, _, k, fill, reverse):
    n, ch = x.shape
    rows = lax.broadcasted_iota(jnp.int32, (n, ch), 0)
    if not reverse:
        return jnp.where(rows >= k, pltpu.roll(x, k, axis=0), fill)
    return jnp.where(rows < n - k, pltpu.roll(x, n - k, axis=0), fill)


def _blockdiag(xp):
    n = xp.shape[0]
    lane = lax.broadcasted_iota(jnp.int32, (n, PAIR), 1)
    zero = jnp.zeros_like(xp)
    top = jnp.where(lane < HEAD, xp, zero)
    bot = jnp.where(lane >= HEAD, xp, zero)
    return jnp.concatenate([top, bot], axis=0)


def _mod_kernel(c_ref, w_ref, b_ref, o_ref):
    cc = c_ref[...]
    act = cc * jax.nn.sigmoid(cc)
    o_ref[0] = _dot(act, w_ref[0]) + b_ref[0]


def _modulation(cc, w_mod, b_mod):
    depth, d, n = w_mod.shape
    rows = cc.shape[0]
    tn = n // MOD_COL_TILES
    return pl.pallas_call(
        _mod_kernel,
        grid=(depth, n // tn),
        in_specs=[
            pl.BlockSpec((rows, d), lambda l, j: (0, 0)),
            pl.BlockSpec((1, d, tn), lambda l, j: (l, 0, j)),
            pl.BlockSpec((1, 1, tn), lambda l, j: (l, 0, j)),
        ],
        out_specs=pl.BlockSpec((1, rows, tn), lambda l, j: (l, 0, j)),
        out_shape=jax.ShapeDtypeStruct((depth, rows, n), F32),
        compiler_params=pltpu.CompilerParams(
            dimension_semantics=("parallel", "parallel"), vmem_limit_bytes=VMEM_LIMIT),
        name="modulation",
    )(cc, w_mod, b_mod.reshape(depth, 1, n))


def _ffn_kernel(*refs, mod_base, has_pre, y_col_major, final_norm, d_ff, tf):
    it = iter(refs)
    s_ref = next(it)
    if has_pre:
        y_ref = next(it)
        wout_ref = next(it)
        if y_col_major:
            perm_ref = next(it)
    mod_ref = next(it)
    g_ref = next(it)
    wgu_ref = next(it)
    wdown_ref = next(it)
    if final_norm:
        gfin_ref = next(it)
    o_ref = next(it)
    acc_ref = next(it)

    s = s_ref[0]
    if has_pre:
        gate_mix = mod_ref[0, mod_base - 1:mod_base, :]
        if y_col_major:
            mix = wout_ref.shape[1]
            yc = jnp.concatenate([y_ref[0, :, c * mix:(c + 1) * mix] for c in range(GRID_W)], axis=0)
            yb = jnp.dot(perm_ref[...], yc.astype(BF16), preferred_element_type=F32).astype(BF16)
        else:
            yb = y_ref[0].astype(BF16)
        s = s + gate_mix * jnp.dot(yb, wout_ref[0], preferred_element_type=F32)
    shift = mod_ref[0, mod_base:mod_base + 1, :]
    scale = mod_ref[0, mod_base + 1:mod_base + 2, :]
    gate = mod_ref[0, mod_base + 2:mod_base + 3, :]
    hb = _rms_modulate(s, g_ref[...], shift, scale).astype(BF16)
    for j in range(d_ff // tf):
        gt = jnp.dot(hb, wgu_ref[0, :, j * tf:(j + 1) * tf], preferred_element_type=F32)
        up = jnp.dot(hb, wgu_ref[0, :, d_ff + j * tf:d_ff + (j + 1) * tf], preferred_element_type=F32)
        act = ((gt * jax.nn.sigmoid(gt)) * up).astype(BF16)
        part = jnp.dot(act, wdown_ref[0, j * tf:(j + 1) * tf, :], preferred_element_type=F32)
        if j == 0:
            acc_ref[...] = part
        else:
            acc_ref[...] += part
    out = s + (0.5 * gate) * acc_ref[...]
    if final_norm:
        out = out * lax.rsqrt(jnp.mean(out * out, axis=-1, keepdims=True) + NORM_EPS) * gfin_ref[...]
    o_ref[0] = out


def _layer_spec(w, layer):
    return pl.BlockSpec((1,) + w.shape[1:], lambda b, i: (layer, 0, 0))


def _ffn(s, mod, mod_base, g, wgu, wdown, layer, y=None, wout=None, y_col_major=False, gfin=None):
    bv, tv, d = s.shape
    d_ff = wdown.shape[1]
    tf = V7X_MXU_DIM
    tm = min(TOKEN_TILE, tv)
    has_pre = y is not None
    final_norm = gfin is not None
    const = lambda b, i: (0, 0)
    tile = lambda b, i: (b, i, 0)
    in_specs = [pl.BlockSpec((1, tm, d), tile)]
    args = [s]
    if has_pre:
        y_rows = tm // GRID_W if y_col_major else tm
        in_specs += [pl.BlockSpec((1, y_rows, y.shape[2]), tile), _layer_spec(wout, layer)]
        args += [y, wout]
        if y_col_major:
            src = (jnp.arange(tm) % GRID_W) * y_rows + jnp.arange(tm) // GRID_W
            in_specs.append(pl.BlockSpec((tm, tm), const))
            args.append(jax.nn.one_hot(src, tm, dtype=BF16))
    in_specs += [
        pl.BlockSpec((1, mod.shape[1], d), lambda b, i: (b, 0, 0)),
        pl.BlockSpec((1, d), const),
        _layer_spec(wgu, layer),
        _layer_spec(wdown, layer),
    ]
    args += [mod, g.reshape(1, d), wgu, wdown]
    if final_norm:
        in_specs.append(pl.BlockSpec((1, d), const))
        args.append(gfin.reshape(1, d))
    body = functools.partial(_ffn_kernel, mod_base=mod_base, has_pre=has_pre, y_col_major=y_col_major,
                             final_norm=final_norm, d_ff=d_ff, tf=tf)
    return pl.pallas_call(
        body,
        grid=(bv, tv // tm),
        in_specs=in_specs,
        out_specs=pl.BlockSpec((1, tm, d), tile),
        out_shape=jax.ShapeDtypeStruct(s.shape, F32),
        scratch_shapes=[pltpu.VMEM((tm, d), F32)],
        compiler_params=pltpu.CompilerParams(
            dimension_semantics=("parallel", "parallel"), vmem_limit_bytes=VMEM_LIMIT),
        name="ffn",
    )(*args)


def _inproj_kernel(s_ref, mod_ref, g_ref, w_ref, *rest, mod_base, col_major):
    shift = mod_ref[0, mod_base:mod_base + 1, :]
    scale = mod_ref[0, mod_base + 1:mod_base + 2, :]
    hb = _rms_modulate(s_ref[0], g_ref[...], shift, scale).astype(BF16)
    if not col_major:
        o_ref, = rest
        o_ref[0] = jnp.dot(hb, w_ref[0], preferred_element_type=F32)
        return
    perm_ref, o_ref = rest
    hp = jnp.dot(perm_ref[...], hb, preferred_element_type=F32).astype(BF16)
    res = jnp.dot(hp, w_ref[0], preferred_element_type=F32)
    rows = res.shape[0] // GRID_W
    n = res.shape[1]
    for c in range(GRID_W):
        o_ref[0, :, c * n:(c + 1) * n] = res[c * rows:(c + 1) * rows, :]


def _inproj(s, mod, mod_base, g, w, layer, col_major=False):
    bv, tv, d = s.shape
    tm = min(TOKEN_TILE, tv)
    n = w.shape[2]
    const = lambda b, i: (0, 0)
    in_specs = [
        pl.BlockSpec((1, tm, d), lambda b, i: (b, i, 0)),
        pl.BlockSpec((1, mod.shape[1], d), lambda b, i: (b, 0, 0)),
        pl.BlockSpec((1, d), const),
        _layer_spec(w, layer),
    ]
    args = [s, mod, g.reshape(1, d), w]
    if col_major:
        rows = tm // GRID_W
        out_spec = pl.BlockSpec((1, rows, GRID_W * n), lambda b, i: (b, i, 0))
        out_shape = jax.ShapeDtypeStruct((bv, tv // GRID_W, GRID_W * n), F32)
        src = (jnp.arange(tm) % rows) * GRID_W + jnp.arange(tm) // rows
        in_specs.append(pl.BlockSpec((tm, tm), const))
        args.append(jax.nn.one_hot(src, tm, dtype=BF16))
    else:
        out_spec = pl.BlockSpec((1, tm, n), lambda b, i: (b, i, 0))
        out_shape = jax.ShapeDtypeStruct((bv, tv, n), F32)
    return pl.pallas_call(
        functools.partial(_inproj_kernel, mod_base=mod_base, col_major=col_major),
        grid=(bv, tv // tm),
        in_specs=in_specs,
        out_specs=out_spec,
        out_shape=out_shape,
        compiler_params=pltpu.CompilerParams(
            dimension_semantics=("parallel", "parallel"), vmem_limit_bytes=VMEM_LIMIT),
        name="inproj",
    )(*args)


CONV_W = 256
RW = 384
OFF_B = 3 * CONV_W
RWKV_COLS = 3 * RW + 64 + 64 + 128
OFF_C = OFF_B + RWKV_COLS
P_COLS = OFF_C + 2 * RW
F_PB = 3 * RW
F_KK = F_PB + RWKV_COLS
F_COLS = F_KK + RW
V_W0, V_A0, V_KK, V_KA, V_RK, V_LNG, V_LNB, V_CW, V_CB, V_BR, V_BI, V_LAM = 0, 1, 2, 3, 4, 5, 6, 7, 11, 12, 13, 14
VEC_ROWS = 16


def _mixer_kernel(*refs, d, n_chunks, line_is_chunk):
    it = iter(refs)
    p_ref, pp_ref, pn_ref = next(it), next(it), next(it)
    if d == 1:
        f_in_ref = next(it)
    s0_ref, h0_ref = next(it), next(it)
    vec_ref, mu_ref, conva_ref = next(it), next(it), next(it)
    w2_ref, a2_ref, g2_ref, wr_ref, wi_ref = next(it), next(it), next(it), next(it), next(it)
    out_ref, s_ref, h_ref = next(it), next(it), next(it)

    reverse = d == 1
    i = pl.program_id(1)
    ci = (n_chunks - 1 - i) if reverse else i
    keep_prev = jnp.where(ci == 0, 0.0, 1.0)
    keep_next = jnp.where(ci == n_chunks - 1, 0.0, 1.0)

    @pl.when(i == 0)
    def _():
        s_ref[...] = s0_ref[...]
        h_ref[...] = h0_ref[...]

    def vec(r, sl=slice(None)):
        return vec_ref[r:r + 1, sl]

    nb = p_ref.shape[0]
    grp = [slice(b * CHUNK, (b + 1) * CHUNK) for b in range(nb)]

    def stacked(ref, c0, c1):
        return jnp.concatenate([ref[b, :, c0:c1] for b in range(nb)], axis=0)

    lane_p = lax.broadcasted_iota(jnp.int32, (PAIR, PAIR), 1)
    row_p = lax.broadcasted_iota(jnp.int32, (PAIR, PAIR), 0)
    same_head = (lane_p >= HEAD) == (row_p >= HEAD)
    head_ones = jnp.where(same_head, 1.0, 0.0).astype(BF16)

    def head_sum(x):
        return jnp.concatenate(
            [_dot_ones_rhs(x[:, PAIR * hp:PAIR * (hp + 1)], head_ones) for hp in range(RW // PAIR)],
            axis=1)

    row_c = lax.broadcasted_iota(jnp.int32, (CHUNK, CHUNK), 0)
    col_c = lax.broadcasted_iota(jnp.int32, (CHUNK, CHUNK), 1)
    tri = jnp.where((col_c >= row_c) if reverse else (col_c <= row_c), 1.0, 0.0).astype(BF16)
    row_h = lax.broadcasted_iota(jnp.int32, (CHUNK, PAIR), 0)
    col_h = lax.broadcasted_iota(jnp.int32, (CHUNK, PAIR), 1) & (HEAD - 1)
    strict = (col_h > row_h) if reverse else (col_h < row_h)
    incl = (col_h >= row_h) if reverse else (col_h <= row_h)
    eye_h = jnp.where(col_h == row_h, 1.0, 0.0)
    eye_p = lane_p == row_p
    n_pair = RW // PAIR

    def blk(sp, gs):
        return tuple(x[gs[0], gs[1]] for x in sp)

    def rwkv_front(bs):
        gl = [slice(i * CHUNK, (i + 1) * CHUNK) for i in range(len(bs))]

        def st(ref, c0, c1):
            return jnp.concatenate([ref[b, :, c0:c1] for b in bs], axis=0)

        o = {}
        if reverse:
            f_in = st(f_in_ref, 0, F_COLS)
            pb = f_in[:, F_PB:F_PB + RWKV_COLS]
            kk = f_in[:, F_KK:F_KK + RW]
            o["f_in"] = f_in
        else:
            cur = st(p_ref, OFF_B, OFF_C)
            prev = st(pp_ref, OFF_B, OFF_C) * keep_prev
            nxt = st(pn_ref, OFF_B, OFF_C) * keep_next
            pb = (cur + mu_ref[0:1, :] * (_shift_dn(cur, prev, 1) - cur)
                  + mu_ref[1:2, :] * (_shift_up(cur, nxt, 1) - cur))
        r = pb[:, 0:RW]
        k = pb[:, RW:2 * RW]
        v = pb[:, 2 * RW:3 * RW]
        dw = pb[:, 3 * RW:3 * RW + 64]
        da = pb[:, 3 * RW + 64:3 * RW + 128]
        dg = pb[:, 3 * RW + 128:3 * RW + 256]
        if not reverse:
            kkr = k * vec(V_KK)
            kk = kkr * lax.rsqrt(head_sum(kkr * kkr) + 1e-12)
            for i, b in enumerate(bs):
                out_ref[b, :, F_PB:F_PB + RWKV_COLS] = pb[gl[i]]
                out_ref[b, :, F_KK:F_KK + RW] = kk[gl[i]]
        lw = -DECAY_SCALE * jax.nn.sigmoid(vec(V_W0) + _dot(jnp.tanh(dw), w2_ref[...]))
        a = jax.nn.sigmoid(vec(V_A0) + _dot(da, a2_ref[...]))
        kd = k * (1.0 + (a - 1.0) * vec(V_KA))
        beta = kk * a
        o["bonus"] = head_sum(r * kd * vec(V_RK)) * v
        if reverse:
            o["g"] = _dot(jax.nn.sigmoid(dg), g2_ref[...])

        c_inc = jnp.concatenate([_dot_exact_lhs(tri, lw[g]) for g in gl], axis=0)
        c_exc = c_inc - lw
        c_tot = [c_inc[g.start:g.start + 1, :] if reverse else c_inc[g.stop - 1:g.stop, :] for g in gl]
        e_neg = jnp.exp(-c_inc)
        gam = [jnp.exp(t) for t in c_tot]
        gam_rows = _rows_to_groups(gam)
        at = -(kk * jnp.exp(c_exc))
        rt = r * jnp.exp(c_inc)
        bh = beta * e_neg
        kh = kd * e_neg
        bg = bh * gam_rows
        kg = kh * gam_rows

        o["units"] = [(b, hp) for b in bs for hp in range(n_pair)]
        pos = [(gl[i], slice(PAIR * hp, PAIR * (hp + 1))) for i in range(len(bs)) for hp in range(n_pair)]
        o["gam"] = [gam[i][:, PAIR * hp:PAIR * (hp + 1)] for i in range(len(bs)) for hp in range(n_pair)]
        s_sp = [_split2(s_ref[b, hp]) for b, hp in o["units"]]
        at_sp, bh_sp, kh_sp = _split2(at), _split2(bh), _split2(kh)
        bg_sp, kg_sp, v_sp = _split2(bg), _split2(kg), _split2(v)
        rt_hi = rt.astype(BF16)
        gm = [_mm3_and_1(blk(at_sp, gs), rt_hi[gs[0], gs[1]],
                         _cat([_bd(blk(bh_sp, gs)), _bd(blk(kh_sp, gs))], 0), _NT) for gs in pos]
        o["a_m"] = [jnp.where(strict, m[0][:, 0:PAIR], 0.0) for m in gm]
        b_m = [jnp.where(strict, m[0][:, PAIR:2 * PAIR], 0.0) for m in gm]
        o["p_hi"] = [jnp.where(incl, m[1][:, 0:PAIR], 0.0).astype(BF16) for m in gm]
        o["q_hi"] = [jnp.where(incl, m[1][:, PAIR:2 * PAIR], 0.0).astype(BF16) for m in gm]
        z = [_mm3_and_1(blk(at_sp, gs), rt_hi[gs[0], gs[1]], s_sp[u], _NN) for u, gs in enumerate(pos)]
        o["z_r"] = [zz[1] for zz in z]
        o["w_m"] = [z[u][0] + _mm3(_split2(b_m[u]), _bd(blk(v_sp, gs))) for u, gs in enumerate(pos)]
        o["s_sp"] = s_sp
        o["v_sp"] = [blk(v_sp, gs) for gs in pos]
        o["bg_sp"] = [blk(bg_sp, gs) for gs in pos]
        o["kg_sp"] = [blk(kg_sp, gs) for gs in pos]
        return o

    fr = rwkv_front(list(range(nb)))
    units = fr["units"]
    pos = [(grp[b], slice(PAIR * hp, PAIR * (hp + 1))) for b, hp in units]
    a_m, p_hi, q_hi, z_r, w_m = (fr[k] for k in ("a_m", "p_hi", "q_hi", "z_r", "w_m"))
    s_sp, v_sp, bg_sp, kg_sp, gam = (fr[k] for k in ("s_sp", "v_sp", "bg_sp", "kg_sp", "gam"))
    bonus = fr["bonus"]
    if reverse:
        f_in, g = fr["f_in"], fr["g"]

    def rglru():
        xr = stacked(p_ref, OFF_C, OFF_C + RW)
        if not reverse:
            xh = stacked(pp_ref, OFF_C, OFF_C + RW) * keep_prev
            u_l = (vec(V_CW + 3) * xr + vec(V_CW + 2) * _shift_dn(xr, xh, 1)
                   + vec(V_CW + 1) * _shift_dn(xr, xh, 2) + vec(V_CW) * _shift_dn(xr, xh, 3))
        else:
            xh = stacked(pn_ref, OFF_C, OFF_C + RW) * keep_next
            u_l = (vec(V_CW) * xr + vec(V_CW + 1) * _shift_up(xr, xh, 1)
                   + vec(V_CW + 2) * _shift_up(xr, xh, 2) + vec(V_CW + 3) * _shift_up(xr, xh, 3))
        u_l = u_l + vec(V_CB)
        rg = jax.nn.sigmoid(_dot(u_l, wr_ref[...]) + vec(V_BR))
        ig = jax.nn.sigmoid(_dot(u_l, wi_ref[...]) + vec(V_BI))
        log_a = -LRU_C * (_softplus(-vec(V_LAM)) * rg)
        a_l = jnp.exp(log_a)
        th = jnp.tanh(log_a)
        b_l = jnp.sqrt(-2.0 * th / (1.0 - th)) * (ig * u_l)
        step = 1
        while step < CHUNK:
            b_l = b_l + a_l * _scan_shift(b_l, step, 0.0, reverse)
            a_l = a_l * _scan_shift(a_l, step, 1.0, reverse)
            step *= 2
        return b_l + a_l * _rows_to_groups([h_ref[b] for b in range(nb)])

    assert INV_BASE == 8
    blk_of = lambda x, w: x >> (w.bit_length() - 1)
    same = {w: blk_of(row_h, w) == blk_of(col_h, w) for w in (INV_BASE, 2 * INV_BASE, 4 * INV_BASE)}
    d_m = [jnp.where(same[INV_BASE], m, 0.0) for m in a_m]
    t_m = [eye_h + m for m in d_m]
    pw_sp = [_split2(_mm3(ps, _bd(ps))) for ps in [_split2(m) for m in d_m]]
    prod = [_mm3(_cat([ps, _split2(t)], 0), _bd(ps)) for ps, t in zip(pw_sp, t_m)]
    pw_sp = [_split2(p[0:CHUNK]) for p in prod]
    t_m = [t + p[CHUNK:2 * CHUNK] for t, p in zip(t_m, prod)]
    t_m = [t + _mm3(_split2(t), _bd(ps)) for t, ps in zip(t_m, pw_sp)]
    width = INV_BASE
    while width < CHUNK:
        if 2 * width < CHUNK:
            off = same[2 * width] & jnp.logical_not(same[width])
        else:
            off = jnp.logical_not(same[width])
        e_sp = [_split2(jnp.where(off, m, 0.0)) for m in a_m]
        t_sp = [_split2(t) for t in t_m]
        x_sp = [_split2(_mm3(e, _bd(ts))) for e, ts in zip(e_sp, t_sp)]
        t_m = [t + _mm3(ts, _bd(x)) for t, ts, x in zip(t_m, t_sp, x_sp)]
        width *= 2
    u_sp = [_split2(_mm3(_split2(t), _bd(_split2(w)))) for t, w in zip(t_m, w_m)]
    y_rw = [z_r[u] + jnp.dot(
        jnp.concatenate([p_hi[u], q_hi[u]], axis=1),
        jnp.concatenate([_blockdiag(u_sp[u][0]), _blockdiag(v_sp[u][0])], axis=0),
        preferred_element_type=F32) for u in range(len(units))]
    s_upd = []
    for u in range(len(units)):
        dgam = jnp.where(eye_p, jnp.broadcast_to(gam[u], (PAIR, PAIR)), 0.0)
        s_new = _mm3(_cat([bg_sp[u], kg_sp[u], _split2(dgam)], 0),
                     _cat([u_sp[u], v_sp[u], s_sp[u]], 0), _TN)
        s_upd.append(jnp.where(same_head, s_new, 0.0))

    if not reverse:
        y_out = y_rw
    else:
        y_out = []
        for u, ((gr_, sl), (b, hp)) in enumerate(zip(pos, units)):
            ysum = y_rw[u] + f_in[gr_, sl]
            mean = _dot_ones_rhs(ysum, head_ones) * (1.0 / HEAD)
            cen = ysum - mean
            var = _dot_ones_rhs(cen * cen, head_ones) * (1.0 / HEAD)
            gn = cen * lax.rsqrt(var + GN_EPS) * vec(V_LNG, sl) + vec(V_LNB, sl)
            y_out.append((gn + (f_in[gr_, RW + PAIR * hp:RW + PAIR * (hp + 1)] + bonus[gr_, sl])) * g[gr_, sl])

    y_base = CONV_W if reverse else 0
    for u, (b, hp) in enumerate(units):
        s_ref[b, hp] = s_upd[u]
        out_ref[b, :, y_base + PAIR * hp:y_base + PAIR * (hp + 1)] = y_out[u]
    if not reverse:
        for b in range(nb):
            out_ref[b, :, RW:2 * RW] = bonus[grp[b]]

    h = rglru()
    for b in range(nb):
        hb = h[grp[b]]
        h_ref[b] = hb[0:1, :] if reverse else hb[CHUNK - 1:CHUNK, :]

    if not reverse:
        for b in range(nb):
            out_ref[b, :, 2 * RW:3 * RW] = h[grp[b]]
    else:
        gr = stacked(p_ref, OFF_C + RW, OFF_C + 2 * RW)
        y_lru = jax.nn.gelu(gr) * (f_in[:, 2 * RW:3 * RW] + h)
        bgate = stacked(p_ref, 0, CONV_W)
        uc = stacked(p_ref, CONV_W, 2 * CONV_W) * stacked(p_ref, 2 * CONV_W, 3 * CONV_W)
        if line_is_chunk:
            zero = jnp.zeros((nb * HALO, CONV_W), F32)
            up, un = zero, zero
        else:
            up = stacked(pp_ref, CONV_W, 2 * CONV_W) * stacked(pp_ref, 2 * CONV_W, 3 * CONV_W) * keep_prev
            un = stacked(pn_ref, CONV_W, 2 * CONV_W) * stacked(pn_ref, 2 * CONV_W, 3 * CONV_W) * keep_next
        yc = bgate * (conva_ref[0:1, :] * _shift_dn(uc, up, 1) + conva_ref[1:2, :] * uc
                      + conva_ref[2:3, :] * _shift_up(uc, un, 1))
        for b in range(nb):
            out_ref[b, :, CONV_W + RW:CONV_W + 2 * RW] = y_lru[grp[b]]
            out_ref[b, :, 0:CONV_W] = yc[grp[b]]


def _mixer_pass(p, f_in, s0, h0, prm, d, col_major, line_is_chunk):
    bsz = p.shape[0]
    t_len = p.shape[1] * p.shape[2] // P_COLS
    n = t_len // CHUNK
    reverse = d == 1
    out_cols = 4 * CONV_W if reverse else F_COLS

    def ci(i):
        return (n - 1 - i) if reverse else i

    if col_major:
        assert p.shape == (bsz, CHUNK, GRID_W * P_COLS) and n == GRID_W
        p_view = p
        cur_spec = pl.BlockSpec((MIX_NB, CHUNK,P_COLS), lambda b, i: (b, 0, ci(i)))
        prev_spec = pl.BlockSpec((MIX_NB, HALO,P_COLS),
                                 lambda b, i: (b, CHUNK // HALO - 1, jnp.maximum(ci(i) - 1, 0)))
        next_spec = pl.BlockSpec((MIX_NB, HALO,P_COLS), lambda b, i: (b, 0, jnp.minimum(ci(i) + 1, n - 1)))
    else:
        p_view = p
        per = CHUNK // HALO
        cur_spec = pl.BlockSpec((MIX_NB, CHUNK,P_COLS), lambda b, i: (b, ci(i), 0))
        prev_spec = pl.BlockSpec((MIX_NB, HALO,P_COLS), lambda b, i: (b, jnp.maximum(per * ci(i) - 1, 0), 0))
        next_spec = pl.BlockSpec((MIX_NB, HALO,P_COLS),
                                 lambda b, i: (b, jnp.minimum(per * ci(i) + per, per * n - 1), 0))
    full = lambda a: pl.BlockSpec(a.shape, lambda b, i: (0,) * a.ndim)
    in_specs = [cur_spec, prev_spec, next_spec]
    args = [p_view, p_view, p_view]
    if reverse:
        in_specs.append(pl.BlockSpec((MIX_NB, CHUNK,F_COLS), lambda b, i: (b, ci(i), 0)))
        args.append(f_in)
    state_specs = [pl.BlockSpec((MIX_NB,) + s0.shape[1:], lambda b, i: (b, 0, 0, 0)),
                   pl.BlockSpec((MIX_NB, 1, RW), lambda b, i: (b, 0, 0))]
    in_specs += state_specs
    args += [s0, h0]
    params = [prm["vec"][d], prm["mu"], prm["conv_a"], prm["w2"][d], prm["a2"][d], prm["g2"],
              prm["wr"][d], prm["wi"][d]]
    in_specs += [full(a) for a in params]
    args += params

    if reverse and col_major:
        out_shape = jax.ShapeDtypeStruct((bsz, CHUNK, n * out_cols), F32)
        out_spec = pl.BlockSpec((MIX_NB, CHUNK,out_cols), lambda b, i: (b, 0, ci(i)))
    else:
        out_shape = jax.ShapeDtypeStruct((bsz, t_len, out_cols), F32)
        out_spec = pl.BlockSpec((MIX_NB, CHUNK,out_cols), lambda b, i: (b, ci(i), 0))

    out, s_fin, h_fin = pl.pallas_call(
        functools.partial(_mixer_kernel, d=d, n_chunks=n, line_is_chunk=line_is_chunk),
        grid=(bsz // MIX_NB, n),
        in_specs=in_specs,
        out_specs=[out_spec] + state_specs,
        out_shape=[out_shape, jax.ShapeDtypeStruct(s0.shape, F32), jax.ShapeDtypeStruct(h0.shape, F32)],
        compiler_params=pltpu.CompilerParams(
            dimension_semantics=("parallel", "arbitrary"), vmem_limit_bytes=VMEM_LIMIT),
        name="mixer_bwd" if reverse else "mixer_fwd",
    )(*args)
    return out, s_fin, h_fin


def _mix_stream(p, prm, col_major, line_is_chunk, init):
    (s0f, h0f), (s0b, h0b) = init
    f, sf, hf = _mixer_pass(p, None, s0f, h0f, prm, 0, col_major, line_is_chunk)
    y, sb, hb = _mixer_pass(p, f, s0b, h0b, prm, 1, col_major, line_is_chunk)
    return y, ((sf, hf), (sb, hb))


def _mixer_params(l, conv_a, rwkv_mu, rwkv_w0, rwkv_w2, rwkv_a0, rwkv_a2, rwkv_g2, rwkv_kk, rwkv_ka,
                  rwkv_rk, rwkv_lnx_g, rwkv_lnx_b, lru_conv_w, lru_conv_b, lru_w_r, lru_b_r, lru_w_i,
                  lru_b_i, lru_lam):
    def both(a):
        return jnp.broadcast_to(a[None], (2,) + a.shape)

    rows = [rwkv_w0[l], rwkv_a0[l], both(rwkv_kk[l]), both(rwkv_ka[l]), both(rwkv_rk[l]),
            both(rwkv_lnx_g[l]), both(rwkv_lnx_b[l]),
            lru_conv_w[l][:, 0], lru_conv_w[l][:, 1], lru_conv_w[l][:, 2], lru_conv_w[l][:, 3],
            lru_conv_b[l], lru_b_r[l], lru_b_i[l], lru_lam[l], jnp.zeros((2, RW), F32)]
    assert len(rows) == VEC_ROWS
    vec = jnp.stack(rows, axis=1)
    nb = lru_w_r.shape[2]
    eye = jnp.eye(nb, dtype=F32)

    def bd(w):
        return jnp.einsum("dnij,nm->dnimj", w, eye).reshape(2, nb * HEAD, nb * HEAD).astype(BF16)

    return {
        "vec": vec, "mu": rwkv_mu[l], "conv_a": conv_a[l],
        "w2": rwkv_w2[l].astype(BF16), "a2": rwkv_a2[l].astype(BF16), "g2": rwkv_g2[l].astype(BF16),
        "wr": bd(lru_w_r[l]), "wi": bd(lru_w_i[l]),
    }


def kernel(x, c, ctx, c_ctx, w_mod, b_mod, g_ffn1, w_gu1, w_down1, g_mix, w_in, conv_a, rwkv_mu, rwkv_w0, rwkv_w2, rwkv_a0, rwkv_a2, rwkv_g2, rwkv_kk, rwkv_ka, rwkv_rk, rwkv_lnx_g, rwkv_lnx_b, lru_conv_w, lru_conv_b, lru_w_r, lru_b_r, lru_w_i, lru_b_i, lru_lam, w_out, g_ffn2, w_gu2, w_down2, g_final):
    bsz, t_len, d = x.shape
    depth = w_mod.shape[0]
    n_ctx = ctx.shape[1]
    assert w_in.shape[2] == P_COLS and t_len == GRID_W * CHUNK and n_ctx % CHUNK == 0
    assert bsz % MIX_NB == 0 and t_len % TOKEN_TILE == 0 and (bsz * n_ctx) % TOKEN_TILE == 0

    mod_rows = -(-(bsz + 1) // 8) * 8
    cc = jnp.concatenate([c, c_ctx[None, :], jnp.zeros((mod_rows - bsz - 1, d), F32)], axis=0)
    mods = _modulation(cc, w_mod, b_mod)

    zero_state = ((jnp.zeros((bsz, RW // PAIR, PAIR, PAIR), F32), jnp.zeros((bsz, 1, RW), F32)),) * 2
    wgu1, wd1, wgu2, wd2 = (w.astype(BF16) for w in (w_gu1, w_down1, w_gu2, w_down2))
    win, wout = w_in.astype(BF16), w_out.astype(BF16)
    s_lat = x
    s_ctx = ctx.reshape(1, bsz * n_ctx, d)
    for l in range(depth):
        last = l == depth - 1
        m_lat = mods[l, :bsz].reshape(bsz, 9, d)
        m_ctx = mods[l, bsz:bsz + 1].reshape(1, 9, d)
        prm = _mixer_params(l, conv_a, rwkv_mu, rwkv_w0, rwkv_w2, rwkv_a0, rwkv_a2, rwkv_g2, rwkv_kk,
                            rwkv_ka, rwkv_rk, rwkv_lnx_g, rwkv_lnx_b, lru_conv_w, lru_conv_b, lru_w_r,
                            lru_b_r, lru_w_i, lru_b_i, lru_lam)

        s_lat = _ffn(s_lat, m_lat, 0, g_ffn1[l], wgu1, wd1, l)
        s_ctx = _ffn(s_ctx, m_ctx, 0, g_ffn1[l], wgu1, wd1, l)
        col_major = l % 2 == 1
        p_lat = _inproj(s_lat, m_lat, 3, g_mix[l], win, l, col_major=col_major)
        p_ctx = _inproj(s_ctx, m_ctx, 3, g_mix[l], win, l)

        y_ctx, ctx_fin = _mix_stream(p_ctx.reshape(bsz, n_ctx, P_COLS), prm, False, False, zero_state)
        y_lat, _ = _mix_stream(p_lat, prm, col_major, True, ctx_fin)

        s_lat = _ffn(s_lat, m_lat, 6, g_ffn2[l], wgu2, wd2, l, y=y_lat, wout=wout, y_col_major=col_major,
                     gfin=g_final if last else None)
        if not last:
            s_ctx = _ffn(s_ctx, m_ctx, 6, g_ffn2[l], wgu2, wd2, l,
                         y=y_ctx.reshape(1, bsz * n_ctx, 4 * CONV_W), wout=wout)
    return s_lat
```
